```python
import math
import jax
import jax.numpy as jnp
from jax import lax
import numpy as np

D_MODEL = 2048
BATCH = 2
SEQ = 4096
DEPTH = 2
DEC_BATCH = 128
DEC_SEQ = 4
PAST_LEN = 2048
PAGE_SIZE = 128

DN_HEADS = 8
DN_DK = 128
DN_DV = 128
DN_WIDTH = DN_HEADS * DN_DV
DN_CONV = 4
DN_CHUNK = 64
NSA_HEADS = 8
NSA_KV_HEADS = 2
NSA_GROUP = NSA_HEADS // NSA_KV_HEADS
HEAD_DIM = 128
NSA_WIDTH = NSA_HEADS * HEAD_DIM
KV_WIDTH = NSA_KV_HEADS * HEAD_DIM
CMP_LEN = 32
CMP_STRIDE = 16
SEL_BLOCK = 64
N_SEL = 16
WINDOW = 512
Q_BLOCK = 128
D_FF = 5632
FFN_CONV = 3
DEEPNORM_ALPHA = (2 * DEPTH) ** 0.25
DEEPNORM_BETA = (8 * DEPTH) ** -0.25
LN_EPS = 1e-5
RMS_EPS = 1e-6
NEG_INF = -1e30
FORCE_SCORE = 1e4
IN_SPLITS = (3 * DN_WIDTH, DN_HEADS, DN_HEADS, DN_WIDTH, NSA_WIDTH) + (KV_WIDTH,) * 6 + (3 * NSA_HEADS, 2 * D_MODEL)
IN_WIDTH = sum(IN_SPLITS)

kernel_name = 'hybrid_gdn_nsa_convffn_step'


def _split(z, sizes):
    return jnp.split(z, np.cumsum(sizes)[:-1].tolist(), axis=-1)


def _layernorm(x):
    xf = x.astype(jnp.float32)
    mu = jnp.mean(xf, axis=-1, keepdims=True)
    var = jnp.mean(jnp.square(xf - mu), axis=-1, keepdims=True)
    return ((xf - mu) * lax.rsqrt(var + LN_EPS)).astype(x.dtype)


def _l2norm(x):
    xf = x.astype(jnp.float32)
    return xf * lax.rsqrt(jnp.sum(xf * xf, axis=-1, keepdims=True) + RMS_EPS)


def _causal_dwconv(x, buf, w):
    t = x.shape[1]
    xc = jnp.concatenate([buf.astype(x.dtype), x], axis=1)
    y = sum(xc[:, j:j + t] * w[j] for j in range(w.shape[0]))
    return y, xc[:, t:]


def _alibi_slopes():
    h = jnp.arange(1, NSA_HEADS + 1, dtype=jnp.float32)
    return jnp.exp2(-(8.0 / NSA_HEADS) * h).reshape(NSA_KV_HEADS, NSA_GROUP)


def _chunk_gated_delta(q, k, v, g, beta, s0):
    b, t, h, dk = q.shape
    dv = v.shape[-1]
    c = math.gcd(DN_CHUNK, t)
    n = t // c
    f32 = jnp.float32
    q = _l2norm(q) * (dk ** -0.5)
    k = _l2norm(k)
    v = v.astype(f32)

    def chunks(z):
        z = z.reshape((b, n, c, h) + z.shape[3:])
        return jnp.moveaxis(z, (1, 3), (0, 2))

    qc, kc, vc = chunks(q), chunks(k), chunks(v)
    gc = jnp.cumsum(chunks(g), axis=-1)
    bc = chunks(beta)
    idx = jnp.arange(c)
    causal = idx[:, None] >= idx[None, :]
    strict = idx[:, None] > idx[None, :]
    diff = gc[..., :, None] - gc[..., None, :]
    decay = jnp.where(causal, jnp.exp(jnp.where(causal, diff, 0.0)), 0.0)
    kb = kc * bc[..., None]
    lower = jnp.where(strict, jnp.einsum('nbhid,nbhjd->nbhij', kb, kc) * decay, 0.0)
    eye = jnp.eye(c, dtype=f32)
    a_mat = lower + eye
    tmat = lax.linalg.triangular_solve(a_mat, jnp.broadcast_to(eye, a_mat.shape), left_side=True, lower=True, unit_diagonal=True)
    u = jnp.einsum('nbhij,nbhjd->nbhid', tmat, vc * bc[..., None])
    w = jnp.einsum('nbhij,nbhjd->nbhid', tmat, kb * jnp.exp(gc)[..., None])
    intra = jnp.where(causal, jnp.einsum('nbhid,nbhjd->nbhij', qc, kc) * decay, 0.0)

    def step(s, inp):
        q_i, k_i, u_i, w_i, g_i, a_i = inp
        v_new = u_i - jnp.einsum('bhcd,bhde->bhce', w_i, s)
        o_i = jnp.einsum('bhcd,bhde->bhce', q_i * jnp.exp(g_i)[..., None], s) + jnp.einsum('bhij,bhje->bhie', a_i, v_new)
        g_last = g_i[..., -1:]
        s = s * jnp.exp(g_last)[..., None] + jnp.einsum('bhcd,bhce->bhde', k_i * jnp.exp(g_last - g_i)[..., None], v_new)
        return s, o_i

    s, o = lax.scan(step, s0.astype(f32), (qc, kc, u, w, gc, intra))
    o = jnp.moveaxis(o, (0, 2), (1, 3)).reshape(b, t, h, dv)
    return o, s


def _gdn_branch(qkv, beta_raw, a_raw, gate_raw, conv_buf, s0, p):
    b, t, _ = qkv.shape
    y, new_buf = _causal_dwconv(qkv, conv_buf, p['dn_conv_w'])
    q, k, v = jnp.split(jax.nn.silu(y), 3, axis=-1)
    q = q.reshape(b, t, DN_HEADS, DN_DK)
    k = k.reshape(b, t, DN_HEADS, DN_DK)
    v = v.reshape(b, t, DN_HEADS, DN_DV)
    beta = jax.nn.sigmoid(beta_raw.astype(jnp.float32))
    g = -jnp.exp(p['dn_a_log'].astype(jnp.float32)) * jax.nn.softplus(a_raw.astype(jnp.float32) + p['dn_dt_bias'].astype(jnp.float32))
    o, s_new = _chunk_gated_delta(q, k, v, g, beta, s0)
    o = o * lax.rsqrt(jnp.mean(o * o, axis=-1, keepdims=True) + RMS_EPS) * p['dn_norm_w'].astype(jnp.float32)
    o = o * jax.nn.silu(gate_raw.astype(jnp.float32)).reshape(b, t, DN_HEADS, DN_DV)
    return o.reshape(b, t, DN_WIDTH).astype(qkv.dtype), s_new.astype(s0.dtype), new_buf


def _compress(rows, w1, pe, w2):
    b, lp = rows.shape[:2]
    r = CMP_LEN // CMP_STRIDE
    n_chunks = lp // CMP_STRIDE
    n_cmp = n_chunks - r + 1
    chunks = rows.reshape(b, n_chunks, CMP_STRIDE, NSA_KV_HEADS, HEAD_DIM)
    w1r = w1.reshape(r, CMP_STRIDE, HEAD_DIM, HEAD_DIM)
    hid = jnp.einsum('jd,jde->e', pe, w1)
    for i in range(r):
        hid = hid + jnp.einsum('bnjkd,jde->bnke', chunks[:, i:i + n_cmp], w1r[i])
    return jnp.einsum('bnke,ef->bnkf', jax.nn.silu(hid), w2)


def _nsa_compressed_selected(qg, q_pos, ck_rows, cv_rows, sk_rows, sv_rows, p):
    b, tq = qg.shape[:2]
    length = ck_rows.shape[1]
    pad = (-length) % SEL_BLOCK
    padw = ((0, 0), (0, pad), (0, 0), (0, 0))
    ck_rows, cv_rows, sk_rows, sv_rows = [jnp.pad(r, padw) for r in (ck_rows, cv_rows, sk_rows, sv_rows)]
    lp = length + pad
    slopes = _alibi_slopes()
    scale = HEAD_DIM ** -0.5
    ck = _compress(ck_rows, p['cmp_k_w1'], p['cmp_k_pe'], p['cmp_k_w2'])
    cv = _compress(cv_rows, p['cmp_v_w1'], p['cmp_v_pe'], p['cmp_v_w2'])
    n_cmp = ck.shape[1]
    c_start = jnp.arange(n_cmp, dtype=jnp.int32) * CMP_STRIDE
    c_end = c_start + (CMP_LEN - 1)
    c_valid = (c_end[None, :] <= q_pos[:, None])[:, None, None, :]
    c_dist = (q_pos[:, None] - c_end[None, :]).astype(jnp.float32)[:, None, None, :]
    s = jnp.einsum('btkgd,bnkd->btkgn', qg, ck).astype(jnp.float32) * scale - slopes[:, :, None] * c_dist
    p_cmp = jnp.where(c_valid, jax.nn.softmax(jnp.where(c_valid, s, NEG_INF), axis=-1), 0.0)
    o_cmp = jnp.einsum('btkgn,bnkd->btkgd', p_cmp.astype(cv.dtype), cv)
    n_sel = lp // SEL_BLOCK
    s_start = jnp.arange(n_sel, dtype=jnp.int32) * SEL_BLOCK
    overlap = jnp.maximum(jnp.minimum(c_start[:, None] + CMP_LEN, s_start[None, :] + SEL_BLOCK) - jnp.maximum(c_start[:, None], s_start[None, :]), 0).astype(jnp.float32) / CMP_LEN
    imp = jnp.einsum('btkn,nj->btkj', jnp.sum(p_cmp, axis=3), overlap)
    q_blk = (q_pos // SEL_BLOCK)[:, None]
    j = jnp.arange(n_sel, dtype=jnp.int32)[None, :]
    forced = ((j == 0) | (j == q_blk) | (j == q_blk - 1))[:, None, :]
    valid = (j <= q_blk)[:, None, :]
    score = jnp.where(forced, FORCE_SCORE, jnp.where(valid, imp, -1.0))
    k_top = min(N_SEL, n_sel)
    _, idx = lax.top_k(score, k_top)
    kb = jnp.moveaxis(sk_rows.reshape(b, n_sel, SEL_BLOCK, NSA_KV_HEADS, HEAD_DIM), 3, 1)
    vb = jnp.moveaxis(sv_rows.reshape(b, n_sel, SEL_BLOCK, NSA_KV_HEADS, HEAD_DIM), 3, 1)
    gather = jax.vmap(jax.vmap(lambda blocks, sel: blocks[sel]))

    def attend(args):
        qb, tb, ib = args
        ib = jnp.swapaxes(ib, 1, 2)
        kg = gather(kb, ib)
        vg = gather(vb, ib)
        pos = ib[..., None] * SEL_BLOCK + jnp.arange(SEL_BLOCK, dtype=jnp.int32)
        dist = jnp.swapaxes(tb[None, None, :, None, None] - pos, 1, 2)[:, :, :, None]
        sc = jnp.einsum('btkgd,bktjnd->btkgjn', qb, kg).astype(jnp.float32) * scale
        sc = sc - slopes[:, :, None, None] * dist.astype(jnp.float32)
        sc = jnp.where(dist >= 0, sc, NEG_INF)
        sc = sc.reshape(sc.shape[:4] + (-1,))
        pr = jax.nn.softmax(sc, axis=-1).astype(vg.dtype)
        return jnp.einsum('btkgm,bktmd->btkgd', pr, vg.reshape(vg.shape[:3] + (-1, HEAD_DIM)))

    tb_ = math.gcd(Q_BLOCK, tq)
    nb = tq // tb_
    qs = jnp.swapaxes(qg.reshape(b, nb, tb_, NSA_KV_HEADS, NSA_GROUP, HEAD_DIM), 0, 1)
    ts = q_pos.reshape(nb, tb_)
    iss = jnp.swapaxes(idx.reshape(b, nb, tb_, NSA_KV_HEADS, k_top), 0, 1)
    o_slc = lax.map(attend, (qs, ts, iss))
    o_slc = jnp.swapaxes(o_slc, 0, 1).reshape(b, tq, NSA_KV_HEADS, NSA_GROUP, HEAD_DIM)
    return o_cmp, o_slc


def _window_attend(qb, q_pos, kw, vw, k_pos):
    slopes = _alibi_slopes()
    sc = jnp.einsum('bntkgd,bnskd->bntkgs', qb, kw).astype(jnp.float32) * (HEAD_DIM ** -0.5)
    dist = q_pos[:, :, None] - k_pos[:, None, :]
    valid = ((dist >= 0) & (dist < WINDOW) & (k_pos[:, None, :] >= 0))[:, :, None, None, :]
    sc = sc - slopes[:, :, None] * dist.astype(jnp.float32)[:, :, None, None, :]
    pr = jax.nn.softmax(jnp.where(valid, sc, NEG_INF), axis=-1).astype(vw.dtype)
    return jnp.einsum('bntkgs,bnskd->bntkgd', pr, vw)


def _banded(rows, tb, n_prev):
    b, t = rows.shape[:2]
    nb = t // tb
    rp = jnp.pad(rows, ((0, 0), (n_prev * tb, 0), (0, 0), (0, 0))).reshape(b, nb + n_prev, tb, NSA_KV_HEADS, HEAD_DIM)
    band = jnp.stack([rp[:, i:i + nb] for i in range(n_prev + 1)], axis=2)
    return band.reshape(b, nb, (n_prev + 1) * tb, NSA_KV_HEADS, HEAD_DIM)


def _nsa_window(qg, q_pos, k_new, v_new, past, prompt):
    b, t = qg.shape[:2]
    if prompt:
        tb = math.gcd(Q_BLOCK, t)
        nb = t // tb
        n_prev = -(-(WINDOW - 1) // tb)
        k_pos = (jnp.arange(nb, dtype=jnp.int32)[:, None] - n_prev) * tb + jnp.arange((n_prev + 1) * tb, dtype=jnp.int32)[None, :]
        o = _window_attend(qg.reshape((b, nb, tb) + qg.shape[2:]), q_pos.reshape(nb, tb), _banded(k_new, tb, n_prev), _banded(v_new, tb, n_prev), k_pos)
        keep = min(WINDOW, t)
        return o.reshape(qg.shape), k_new[:, t - keep:], v_new[:, t - keep:]
    buf_len = past['win_k'].shape[1]
    k_all = jnp.concatenate([past['win_k'], k_new], axis=1)
    v_all = jnp.concatenate([past['win_v'], v_new], axis=1)
    k_pos = jnp.arange(buf_len + t, dtype=jnp.int32)[None, :] + (PAST_LEN - buf_len)
    o = _window_attend(qg[:, None], q_pos[None], k_all[:, None], v_all[:, None], k_pos)[:, 0]
    return o, k_all[:, t:], v_all[:, t:]


def _layer(x, mod, p, past, prompt):
    b, t, _ = x.shape
    shift1, scale1, gate1, shift2, scale2, gate2 = jnp.split(mod, 6, axis=-1)
    q_pos = jnp.arange(t, dtype=jnp.int32) + (0 if prompt else PAST_LEN)
    h = _layernorm(x) * (1.0 + scale1) + shift1
    (dn_qkv, dn_beta, dn_a, dn_gate, nsa_q, ck, cv, sk, sv, wk, wv, nsa_g, merge_g) = _split(h @ p['w_in'], IN_SPLITS)
    y_a, dn_state, dn_conv = _gdn_branch(dn_qkv, dn_beta, dn_a, dn_gate, past['dn_conv'], past['dn_state'], p)
    ck, cv, sk, sv, wk, wv = [r.reshape(b, t, NSA_KV_HEADS, HEAD_DIM) for r in (ck, cv, sk, sv, wk, wv)]
    if prompt:
        rows = (ck, cv, sk, sv)
    else:
        rows = tuple(jnp.concatenate([past[n], r], axis=1) for n, r in zip(('cmp_k', 'cmp_v', 'slc_k', 'slc_v'), (ck, cv, sk, sv)))
    qg = nsa_q.reshape(b, t, NSA_KV_HEADS, NSA_GROUP, HEAD_DIM)
    o_cmp, o_slc = _nsa_compressed_selected(qg, q_pos, *rows, p)
    o_win, wk_state, wv_state = _nsa_window(qg, q_pos, wk, wv, past, prompt)
    gts = jax.nn.sigmoid(nsa_g).reshape(b, t, 3, NSA_KV_HEADS, NSA_GROUP, 1)
    y_b = (gts[:, :, 0] * o_cmp + gts[:, :, 1] * o_slc + gts[:, :, 2] * o_win).reshape(b, t, NSA_WIDTH)
    g_a, g_b = jnp.split(jax.nn.sigmoid(merge_g), 2, axis=-1)
    mix = (g_a * (y_a @ p['w_branch_a']) + g_b * (y_b @ p['w_branch_b'])) @ p['w_out']
    x = _layernorm(DEEPNORM_ALPHA * x + gate1 * mix) * p['ln1_g'] + p['ln1_b']
    h = _layernorm(x) * (1.0 + scale2) + shift2
    u, ffn_conv = _causal_dwconv(h @ p['ffn_w_up'], past['ffn_conv'], p['ffn_conv_w'])
    u_a, u_b = jnp.split(u, 2, axis=-1)
    f = (jax.nn.silu(u_a) * u_b) @ p['ffn_w_down']
    x = _layernorm(DEEPNORM_ALPHA * x + gate2 * f) * p['ln2_g'] + p['ln2_b']
    state = {'cmp_k': ck, 'cmp_v': cv, 'slc_k': sk, 'slc_v': sv, 'win_k': wk_state, 'win_v': wv_state,
             'dn_state': dn_state, 'dn_conv': dn_conv, 'ffn_conv': ffn_conv}
    return x, state


def _run(x, c, weights, pasts, prompt):
    new = []
    for l in range(DEPTH):
        p = {name: w[l] for name, w in weights.items()}
        mod = (jax.nn.silu(c) @ p['w_ada'] + p['b_ada'])[:, None, :]
        x, st = _layer(x, mod, p, pasts[l], prompt)
        new.append(st)
    return x, {name: jnp.stack([st[name] for st in new]) for name in new[0]}


def _gather_pages(pool, page_table):
    g = pool[page_table]
    return g.reshape((g.shape[0], -1) + pool.shape[2:])


def setup_inputs(seed: int = 0) -> dict:
    key = jax.random.key(seed)
    keys = iter(jax.random.split(key, 64))

    def nrm(shape, s=1.0):
        return jax.random.normal(next(keys), shape, jnp.float32) * s

    n_pages = PAST_LEN // PAGE_SIZE
    used = DEC_BATCH * n_pages
    n_phys = used + max(1, used // 4)
    page_table = jax.random.permutation(next(keys), n_phys)[:used].reshape(DEC_BATCH, n_pages).astype(jnp.int32)
    win_buf = min(WINDOW, PAST_LEN)
    pool = (DEPTH, n_phys, PAGE_SIZE, NSA_KV_HEADS, HEAD_DIM)
    L = DEPTH
    a_log = jnp.log(jax.random.uniform(next(keys), (L, DN_HEADS), jnp.float32, 1.0, 16.0))
    dt = jnp.exp(jax.random.uniform(next(keys), (L, DN_HEADS), jnp.float32, math.log(1e-3), math.log(1e-1)))
    dt_bias = dt + jnp.log(-jnp.expm1(-dt))
    return {
        'x_prompt': nrm((BATCH, SEQ, D_MODEL)),
        'x_sample': nrm((DEC_BATCH, DEC_SEQ, D_MODEL)),
        'cache_cmp_k': nrm(pool),
        'cache_cmp_v': nrm(pool),
        'cache_slc_k': nrm(pool),
        'cache_slc_v': nrm(pool),
        'cache_win_k': nrm((DEPTH, DEC_BATCH, win_buf, NSA_KV_HEADS, HEAD_DIM)),
        'cache_win_v': nrm((DEPTH, DEC_BATCH, win_buf, NSA_KV_HEADS, HEAD_DIM)),
        'state_dn': nrm((DEPTH, DEC_BATCH, DN_HEADS, DN_DK, DN_DV), 0.1),
        'state_dn_conv': nrm((DEPTH, DEC_BATCH, DN_CONV - 1, 3 * DN_WIDTH)),
        'state_ffn_conv': nrm((DEPTH, DEC_BATCH, FFN_CONV - 1, 2 * D_FF)),
        'page_table': page_table,
        'c_prompt': nrm((BATCH, D_MODEL)),
        'c_sample': nrm((DEC_BATCH, D_MODEL)),
        'w_ada': nrm((L, D_MODEL, 6 * D_MODEL), 0.5 * D_MODEL ** -0.5),
        'b_ada': nrm((L, 6 * D_MODEL), 0.01),
        'w_in': nrm((L, D_MODEL, IN_WIDTH), D_MODEL ** -0.5),
        'dn_conv_w': nrm((L, DN_CONV, 3 * DN_WIDTH), DN_CONV ** -0.5),
        'dn_a_log': a_log,
        'dn_dt_bias': dt_bias,
        'dn_norm_w': 1.0 + nrm((L, DN_DV), 0.02),
        'cmp_k_w1': nrm((L, CMP_LEN, HEAD_DIM, HEAD_DIM), (CMP_LEN * HEAD_DIM) ** -0.5),
        'cmp_k_pe': nrm((L, CMP_LEN, HEAD_DIM), 0.1),
        'cmp_k_w2': nrm((L, HEAD_DIM, HEAD_DIM), HEAD_DIM ** -0.5),
        'cmp_v_w1': nrm((L, CMP_LEN, HEAD_DIM, HEAD_DIM), (CMP_LEN * HEAD_DIM) ** -0.5),
        'cmp_v_pe': nrm((L, CMP_LEN, HEAD_DIM), 0.1),
        'cmp_v_w2': nrm((L, HEAD_DIM, HEAD_DIM), HEAD_DIM ** -0.5),
        'w_branch_a': nrm((L, DN_WIDTH, D_MODEL), DEEPNORM_BETA * DN_WIDTH ** -0.5),
        'w_branch_b': nrm((L, NSA_WIDTH, D_MODEL), DEEPNORM_BETA * NSA_WIDTH ** -0.5),
        'w_out': nrm((L, D_MODEL, D_MODEL), DEEPNORM_BETA * D_MODEL ** -0.5),
        'ln1_g': 1.0 + nrm((L, D_MODEL), 0.02),
        'ln1_b': nrm((L, D_MODEL), 0.02),
        'ffn_w_up': nrm((L, D_MODEL, 2 * D_FF), D_MODEL ** -0.5),
        'ffn_conv_w': nrm((L, FFN_CONV, 2 * D_FF), FFN_CONV ** -0.5),
        'ffn_w_down': nrm((L, D_FF, D_MODEL), DEEPNORM_BETA * D_FF ** -0.5),
        'ln2_g': 1.0 + nrm((L, D_MODEL), 0.02),
        'ln2_b': nrm((L, D_MODEL), 0.02),
    }


def reference(x_prompt, x_sample, cache_cmp_k, cache_cmp_v, cache_slc_k, cache_slc_v, cache_win_k, cache_win_v,
              state_dn, state_dn_conv, state_ffn_conv, page_table, c_prompt, c_sample,
              w_ada, b_ada, w_in, dn_conv_w, dn_a_log, dn_dt_bias, dn_norm_w,
              cmp_k_w1, cmp_k_pe, cmp_k_w2, cmp_v_w1, cmp_v_pe, cmp_v_w2,
              w_branch_a, w_branch_b, w_out, ln1_g, ln1_b, ffn_w_up, ffn_conv_w, ffn_w_down, ln2_g, ln2_b):
    weights = {'w_ada': w_ada, 'b_ada': b_ada, 'w_in': w_in, 'dn_conv_w': dn_conv_w, 'dn_a_log': dn_a_log,
               'dn_dt_bias': dn_dt_bias, 'dn_norm_w': dn_norm_w, 'cmp_k_w1': cmp_k_w1, 'cmp_k_pe': cmp_k_pe,
               'cmp_k_w2': cmp_k_w2, 'cmp_v_w1': cmp_v_w1, 'cmp_v_pe': cmp_v_pe, 'cmp_v_w2': cmp_v_w2,
               'w_branch_a': w_branch_a, 'w_branch_b': w_branch_b, 'w_out': w_out, 'ln1_g': ln1_g, 'ln1_b': ln1_b,
               'ffn_w_up': ffn_w_up, 'ffn_conv_w': ffn_conv_w, 'ffn_w_down': ffn_w_down, 'ln2_g': ln2_g, 'ln2_b': ln2_b}
    bp, dt_ = x_prompt.shape[0], x_prompt.dtype
    zero_past = {'dn_state': jnp.zeros((bp, DN_HEADS, DN_DK, DN_DV), dt_),
                 'dn_conv': jnp.zeros((bp, DN_CONV - 1, 3 * DN_WIDTH), dt_),
                 'ffn_conv': jnp.zeros((bp, FFN_CONV - 1, 2 * D_FF), dt_)}
    y_prompt, sp = _run(x_prompt, c_prompt, weights, [zero_past] * DEPTH, True)
    pasts = [{'dn_state': state_dn[l], 'dn_conv': state_dn_conv[l], 'ffn_conv': state_ffn_conv[l],
              'cmp_k': _gather_pages(cache_cmp_k[l], page_table), 'cmp_v': _gather_pages(cache_cmp_v[l], page_table),
              'slc_k': _gather_pages(cache_slc_k[l], page_table), 'slc_v': _gather_pages(cache_slc_v[l], page_table),
              'win_k': cache_win_k[l], 'win_v': cache_win_v[l]} for l in range(DEPTH)]
    y_sample, ss = _run(x_sample, c_sample, weights, pasts, False)
    return (y_prompt, y_sample,
            sp['cmp_k'], sp['cmp_v'], sp['slc_k'], sp['slc_v'], sp['win_k'], sp['win_v'],
            sp['dn_state'], sp['dn_conv'], sp['ffn_conv'],
            ss['cmp_k'], ss['cmp_v'], ss['slc_k'], ss['slc_v'], ss['win_k'], ss['win_v'],
            ss['dn_state'], ss['dn_conv'], ss['ffn_conv'])
```

```python
import functools
import math

import jax
import jax.numpy as jnp
from jax import lax
from jax.experimental import pallas as pl
from jax.experimental.pallas import tpu as pltpu

F32 = jnp.float32
BF16 = jnp.bfloat16
HIGHEST = lax.Precision.HIGHEST

DN_HEADS = 8
DN_DK = 128
DN_DV = 128
DN_WIDTH = DN_HEADS * DN_DV
DN_CONV = 4
DN_CHUNK = 64
NSA_HEADS = 8
NSA_KV_HEADS = 2
NSA_GROUP = NSA_HEADS // NSA_KV_HEADS
HEAD_DIM = 128
NSA_WIDTH = NSA_HEADS * HEAD_DIM
KV_WIDTH = NSA_KV_HEADS * HEAD_DIM
CMP_LEN = 32
CMP_STRIDE = 16
SEL_BLOCK = 64
N_SEL = 16
WINDOW = 512
FFN_CONV = 3
LN_EPS = 1e-5
RMS_EPS = 1e-6
NEG_INF = -1e30
FORCE_SCORE = 1e4

LANES = 128
SUBLANES = 8
VMEM_LIMIT_BYTES = 56 * 1024 * 1024

OFF_QKV = 0
OFF_DNGATE = 3 * DN_WIDTH
OFF_MERGE = OFF_DNGATE + DN_WIDTH
OFF_NQ = OFF_MERGE + 2 * 2048
OFF_ROWS = OFF_NQ + NSA_WIDTH
OFF_TAIL = OFF_ROWS + 6 * KV_WIDTH
TAIL_BETA = 0
TAIL_A = DN_HEADS
TAIL_NSAG = 2 * DN_HEADS
IN_PAD_WIDTH = 11264


def _cparams(sem):
    return pltpu.CompilerParams(dimension_semantics=sem, vmem_limit_bytes=VMEM_LIMIT_BYTES)


def _dot(a, b, precision=None):
    return jnp.dot(a, b, preferred_element_type=F32, precision=precision)


def _dot_nt(a, b, precision=None):
    return lax.dot_general(a, b, (((1,), (1,)), ((), ())), preferred_element_type=F32, precision=precision)


def _dot_tn(a, b, precision=None):
    return lax.dot_general(a, b, (((0,), (0,)), ((), ())), preferred_element_type=F32, precision=precision)


def _sigmoid(x):
    return 1.0 / (1.0 + jnp.exp(-x))


def _silu(x):
    return x * _sigmoid(x)


def _ln(x):
    mu = jnp.mean(x, axis=-1, keepdims=True)
    xc = x - mu
    var = jnp.mean(xc * xc, axis=-1, keepdims=True)
    return xc * lax.rsqrt(var + LN_EPS)


def _rowparam_spec(p, tm, rows_per_seq):
    if p.ndim == 2:
        return pl.BlockSpec((tm, p.shape[1]), lambda i, *_: (i, 0))
    blocks_per_seq = rows_per_seq // tm
    return pl.BlockSpec((None, 1, p.shape[2]), lambda i, *_: (i // blocks_per_seq, 0, 0))


def _ada_kernel(c_ref, w_ref, b_ref, o_ref):
    h = _silu(c_ref[...]).astype(BF16)
    o_ref[...] = _dot(h, w_ref[...]) + b_ref[...]


def _ada_call(c, w_bf16, b):
    m, k = c.shape
    n = w_bf16.shape[1]
    tn = 1024
    return pl.pallas_call(
        _ada_kernel,
        grid=(n // tn,),
        in_specs=[pl.BlockSpec((m, k), lambda j: (0, 0)),
                  pl.BlockSpec((k, tn), lambda j: (0, j)),
                  pl.BlockSpec((1, tn), lambda j: (0, j))],
        out_specs=pl.BlockSpec((m, tn), lambda j: (0, j)),
        out_shape=jax.ShapeDtypeStruct((m, n), F32),
        compiler_params=_cparams(("parallel",)),
        name="ada_mod",
    )(c, w_bf16, b.reshape(1, n))


def _ln_mm_kernel(x_ref, sh_ref, sc_ref, w_ref, o_ref, h_scr):
    @pl.when(pl.program_id(1) == 0)
    def _():
        h = _ln(x_ref[...]) * (1.0 + sc_ref[...]) + sh_ref[...]
        h_scr[...] = h.astype(BF16)

    o_ref[...] = _dot(h_scr[...], w_ref[...])


def _ln_mm_call(x, shift, scale, w_bf16, rows_per_seq, name):
    m, k = x.shape
    n = w_bf16.shape[1]
    tm = min(512, m)
    tn = 512
    return pl.pallas_call(
        _ln_mm_kernel,
        grid=(m // tm, n // tn),
        in_specs=[pl.BlockSpec((tm, k), lambda i, j: (i, 0)),
                  _rowparam_spec(shift, tm, rows_per_seq),
                  _rowparam_spec(scale, tm, rows_per_seq),
                  pl.BlockSpec((k, tn), lambda i, j: (0, j))],
        out_specs=pl.BlockSpec((tm, tn), lambda i, j: (i, j)),
        out_shape=jax.ShapeDtypeStruct((m, n), F32),
        scratch_shapes=[pltpu.VMEM((tm, k), BF16)],
        compiler_params=_cparams(("parallel", "arbitrary")),
        name=name,
    )(x, shift, scale, w_bf16)


def _softplus(x):
    return jnp.maximum(x, 0.0) + jnp.log1p(jnp.exp(-jnp.abs(x)))


def _gdn_gates_kernel(tail_ref, alog_ref, dtb_ref, beta_ref, gc_ref, gl_ref, *, group, n_pad):
    r = tail_ref.shape[0]
    tail = tail_ref[...]
    braw = tail[:, TAIL_BETA:TAIL_BETA + DN_HEADS]
    araw = tail[:, TAIL_A:TAIL_A + DN_HEADS]
    row = lax.broadcasted_iota(jnp.int32, (r, DN_HEADS), 0)
    real = (row % group) >= n_pad
    g = -jnp.exp(alog_ref[...]) * _softplus(araw + dtb_ref[...])
    g = jnp.where(real, g, 0.0)
    beta = jnp.where(real, _sigmoid(braw), 0.0)
    ri = lax.broadcasted_iota(jnp.int32, (r, r), 0)
    ci = lax.broadcasted_iota(jnp.int32, (r, r), 1)
    same = (ri // group) == (ci // group)
    tri = jnp.where(same & (ri >= ci), 1.0, 0.0).astype(F32)
    ones = jnp.where(same, 1.0, 0.0).astype(F32)
    beta_ref[...] = beta
    gc_ref[...] = _dot(tri, g, HIGHEST)
    gl_ref[...] = _dot(ones, g, HIGHEST)


def _gdn_gates_call(z, tail_off, a_log, dt_bias, group, n_pad):
    m = z.shape[0]
    r = min(512, m)
    tail_blk = tail_off // LANES
    out = jax.ShapeDtypeStruct((m, DN_HEADS), F32)
    ospec = pl.BlockSpec((r, DN_HEADS), lambda i: (i, 0))
    return pl.pallas_call(
        functools.partial(_gdn_gates_kernel, group=group, n_pad=n_pad),
        grid=(m // r,),
        in_specs=[pl.BlockSpec((r, LANES), lambda i: (i, tail_blk)),
                  pl.BlockSpec((1, DN_HEADS), lambda i: (0, 0)),
                  pl.BlockSpec((1, DN_HEADS), lambda i: (0, 0))],
        out_specs=[ospec, ospec, ospec],
        out_shape=[out, out, out],
        compiler_params=_cparams(("parallel",)),
        name="gdn_gates",
    )(z, a_log.reshape(1, DN_HEADS), dt_bias.reshape(1, DN_HEADS))


def _unit_lower_inverse(lower, n_sq):
    c = lower.shape[0]
    eye = (lax.broadcasted_iota(jnp.int32, (c, c), 0) == lax.broadcasted_iota(jnp.int32, (c, c), 1)).astype(F32)
    p = -lower
    t = eye + p
    for _ in range(n_sq):
        p = _dot(p, p, HIGHEST)
        t = t + _dot(t, p, HIGHEST)
    return t


def _gdn_kernel(xq_ref, xk_ref, xv_ref, wq_ref, wk_ref, wv_ref, gate_ref, nw_ref, beta_ref, gc_ref, gl_ref,
                gcrow_ref, s_in_ref, o_ref, s_out_ref, xs_scr, yq_scr, yk_scr, yv_scr,
                *, n_sub, per_chunk_state, n_pad):
    c = DN_CHUNK
    r = xq_ref.shape[0]
    nchunk = r // c
    sub = c // n_sub
    head = pl.program_id(1)
    first = pl.program_id(2) == 0

    @pl.when(first)
    def _():
        s_out_ref[...] = s_in_ref[...]
        xs_scr[:, 0:SUBLANES, :] = jnp.zeros((3, SUBLANES, DN_DK), F32)

    row = lax.broadcasted_iota(jnp.int32, (r, 1), 0)
    real = (row % sub) >= n_pad
    for idx, (x_ref, w_ref, y_scr) in enumerate(((xq_ref, wq_ref, yq_scr), (xk_ref, wk_ref, yk_scr),
                                                 (xv_ref, wv_ref, yv_scr))):
        xs_scr[idx, SUBLANES:SUBLANES + r, :] = x_ref[...]
        w = w_ref[...]
        y = jnp.zeros((r, DN_DK), F32)
        for j in range(DN_CONV):
            off = SUBLANES - (DN_CONV - 1) + j
            y = y + xs_scr[idx, off:off + r, :] * w[j:j + 1, :]
        xs_scr[idx, 0:SUBLANES, :] = xs_scr[idx, r:r + SUBLANES, :]
        y_scr[...] = jnp.where(real, _silu(y), 0.0)

    lane = lax.broadcasted_iota(jnp.int32, (r, DN_HEADS), 1)

    def col(ref):
        return jnp.sum(jnp.where(lane == head, ref[...], 0.0), axis=1, keepdims=True)

    beta_all = col(beta_ref)
    gc_all = col(gc_ref)
    gl_all = col(gl_ref)

    ri = lax.broadcasted_iota(jnp.int32, (c, c), 0)
    ci = lax.broadcasted_iota(jnp.int32, (c, c), 1)
    same = (ri // sub) == (ci // sub)
    causal = same & (ri >= ci)
    strict = same & (ri > ci)
    n_sq = int(math.log2(sub)) - 1
    nw = nw_ref[...]

    for ch in range(nchunk):
        r0 = ch * c
        q = yq_scr[r0:r0 + c, :]
        k = yk_scr[r0:r0 + c, :]
        v = yv_scr[r0:r0 + c, :]
        q = q * lax.rsqrt(jnp.sum(q * q, axis=-1, keepdims=True) + RMS_EPS) * (DN_DK ** -0.5)
        k = k * lax.rsqrt(jnp.sum(k * k, axis=-1, keepdims=True) + RMS_EPS)
        beta = beta_all[r0:r0 + c, :]
        gc = gc_all[r0:r0 + c, :]
        gl = gl_all[r0:r0 + c, :]
        gcrow = gcrow_ref[:, r0:r0 + c]
        diff = gc - gcrow
        decay = jnp.where(causal, jnp.exp(jnp.where(causal, diff, 0.0)), 0.0)
        kb = k * beta
        kbf = k.astype(BF16)
        lower = jnp.where(strict, _dot_nt(kb.astype(BF16), kbf) * decay, 0.0)
        tmat = _unit_lower_inverse(lower, n_sq)
        eg = jnp.exp(gc)
        rhs = jnp.concatenate([v * beta, kb * eg], axis=1)
        uw = _dot(tmat.astype(BF16), rhs.astype(BF16))
        u = uw[:, :DN_DV]
        w = uw[:, DN_DV:]
        intra = jnp.where(causal, _dot_nt(q.astype(BF16), kbf) * decay, 0.0)
        qe = (q * eg).astype(BF16)
        kd = (k * jnp.exp(gl - gc)).astype(BF16)
        wb = w.astype(BF16)
        v_new_parts = []
        o_parts = []
        for s in range(n_sub):
            a0 = s * sub
            sidx = (ch * n_sub + s) if per_chunk_state else 0
            state = s_out_ref[sidx]
            sb = state.astype(BF16)
            v_new = u[a0:a0 + sub, :] - _dot(wb[a0:a0 + sub, :], sb)
            o_parts.append(_dot(qe[a0:a0 + sub, :], sb))
            s_out_ref[sidx] = state * jnp.exp(gl[a0:a0 + 1, :]) + _dot_tn(kd[a0:a0 + sub, :], v_new.astype(BF16))
            v_new_parts.append(v_new)
        v_new = v_new_parts[0] if n_sub == 1 else jnp.concatenate(v_new_parts, axis=0)
        o_state = o_parts[0] if n_sub == 1 else jnp.concatenate(o_parts, axis=0)
        o = o_state + _dot(intra.astype(BF16), v_new.astype(BF16))
        o = o * lax.rsqrt(jnp.mean(o * o, axis=-1, keepdims=True) + RMS_EPS) * nw
        o_ref[r0:r0 + c, :] = o * _silu(gate_ref[r0:r0 + c, :])


def _gdn_call(x, x_col0, conv_w, gate, gate_col0, norm_w, beta, gc, gl, gcrow, s_in, *, rows_per_seq, n_sub,
              per_chunk_state, n_pad):
    m = x.shape[0]
    r = min(512, rows_per_seq) if not per_chunk_state else min(512, m)
    nchunk = r // DN_CHUNK
    n_seq = m // rows_per_seq if not per_chunk_state else m // r
    nt = rows_per_seq // r if not per_chunk_state else 1
    n_state_blk = nchunk * n_sub if per_chunk_state else 1
    xb = x_col0 // DN_DK
    gb = gate_col0 // DN_DV
    hw = DN_WIDTH // DN_DK

    def rowblk(b, h, t):
        return b * nt + t

    xspec = lambda off: pl.BlockSpec((r, DN_DK), lambda b, h, t: (rowblk(b, h, t), xb + off + h))
    wspec = lambda off: pl.BlockSpec((DN_CONV, DN_DK), lambda b, h, t: (0, off + h))
    colspec = pl.BlockSpec((r, DN_HEADS), lambda b, h, t: (rowblk(b, h, t), 0))
    sspec = pl.BlockSpec((n_state_blk, None, DN_DK, DN_DV), lambda b, h, t: (b, h, 0, 0))
    return pl.pallas_call(
        functools.partial(_gdn_kernel, n_sub=n_sub, per_chunk_state=per_chunk_state, n_pad=n_pad),
        grid=(n_seq, DN_HEADS, nt),
        in_specs=[xspec(0), xspec(hw), xspec(2 * hw), wspec(0), wspec(hw), wspec(2 * hw),
                  pl.BlockSpec((r, DN_DV), lambda b, h, t: (rowblk(b, h, t), gb + h)),
                  pl.BlockSpec((1, DN_DV), lambda b, h, t: (0, 0)),
                  colspec, colspec, colspec,
                  pl.BlockSpec((None, 1, r), lambda b, h, t: (h, 0, rowblk(b, h, t))),
                  sspec],
        out_specs=[pl.BlockSpec((r, DN_DV), lambda b, h, t: (rowblk(b, h, t), h)), sspec],
        out_shape=[jax.ShapeDtypeStruct((m, DN_WIDTH), F32), jax.ShapeDtypeStruct(s_in.shape, F32)],
        scratch_shapes=[pltpu.VMEM((3, r + SUBLANES, DN_DK), F32), pltpu.VMEM((r, DN_DK), F32),
                        pltpu.VMEM((r, DN_DK), F32), pltpu.VMEM((r, DN_DV), F32)],
        compiler_params=_cparams(("parallel", "parallel", "arbitrary")),
        name="gdn_delta",
    )(x, x, x, conv_w, conv_w, conv_w, gate, norm_w.reshape(1, DN_DV), beta, gc, gl, gcrow, s_in)


def _pq_kernel(x_ref, w_ref, o_ref):
    for kv in range(NSA_KV_HEADS):
        acc = jnp.zeros((x_ref.shape[0], 2 * HEAD_DIM), F32)
        for j in range(CMP_STRIDE):
            c0 = j * KV_WIDTH + kv * HEAD_DIM
            acc = acc + _dot(x_ref[:, c0:c0 + HEAD_DIM].astype(BF16), w_ref[j])
        o_ref[:, kv * 2 * HEAD_DIM:(kv + 1) * 2 * HEAD_DIM] = acc


def _pq_weights(w1):
    return jnp.concatenate([w1[:CMP_STRIDE], w1[CMP_STRIDE:]], axis=-1).astype(BF16)


def _pq_call(chunks, wpq):
    nc, width = chunks.shape
    tc = 512 if nc % 512 == 0 else nc
    return pl.pallas_call(
        _pq_kernel,
        grid=(nc // tc,),
        in_specs=[pl.BlockSpec((tc, width), lambda i: (i, 0)),
                  pl.BlockSpec(wpq.shape, lambda i: (0, 0, 0))],
        out_specs=pl.BlockSpec((tc, 4 * HEAD_DIM), lambda i: (i, 0)),
        out_shape=jax.ShapeDtypeStruct((nc, 4 * HEAD_DIM), F32),
        compiler_params=_cparams(("parallel",)),
        name="nsa_compress_pq",
    )(chunks, wpq)


def _pe_term(pe, wpq):
    z = jnp.zeros((CMP_STRIDE, HEAD_DIM), F32)
    rows = jnp.stack([jnp.concatenate([pe[:CMP_STRIDE], z], axis=1).reshape(-1),
                      jnp.concatenate([pe[CMP_STRIDE:], z], axis=1).reshape(-1)])
    rows = jnp.concatenate([rows, jnp.zeros((SUBLANES - 2, rows.shape[1]), F32)], axis=0)
    pq = _pq_call(rows, wpq)
    return (pq[0, :HEAD_DIM] + pq[1, HEAD_DIM:2 * HEAD_DIM]).reshape(1, HEAD_DIM)


def _masked_softmax(s, valid):
    s = jnp.where(valid, s, NEG_INF)
    m = jnp.max(s, axis=-1, keepdims=True)
    e = jnp.exp(s - m)
    p = e / jnp.sum(e, axis=-1, keepdims=True)
    return jnp.where(valid, p, 0.0)


def _overlap_matrix(n_cmp, n_blk):
    n = lax.broadcasted_iota(jnp.int32, (n_cmp, n_blk), 0) * CMP_STRIDE
    j = lax.broadcasted_iota(jnp.int32, (n_cmp, n_blk), 1) * SEL_BLOCK
    ov = jnp.maximum(jnp.minimum(n + CMP_LEN, j + SEL_BLOCK) - jnp.maximum(n, j), 0)
    return ov.astype(F32) / CMP_LEN


def _select_bias(imp, tpos):
    t, n_blk = imp.shape
    j = lax.broadcasted_iota(jnp.int32, (t, n_blk), 1)
    q_blk = tpos // SEL_BLOCK
    forced = (j == 0) | (j == q_blk) | (j == q_blk - 1)
    score = jnp.where(forced, FORCE_SCORE, jnp.where(j <= q_blk, imp, -1.0))
    rank = jnp.zeros((t, n_blk), F32)
    for jj in range(n_blk):
        sj = score[:, jj:jj + 1]
        ahead = (sj > score) | ((sj == score) & (j > jj))
        rank = rank + jnp.where(ahead, 1.0, 0.0)
    return jnp.where(rank < N_SEL, 0.0, NEG_INF)


def _block_onehot(n_blk, k0, tk):
    j = lax.broadcasted_iota(jnp.int32, (n_blk, tk), 0)
    s = lax.broadcasted_iota(jnp.int32, (n_blk, tk), 1) + k0
    return jnp.where(s // SEL_BLOCK == j, 1.0, 0.0).astype(BF16)


def _compressed_tokens(hid, w2_ref):
    return _dot(_silu(hid).astype(BF16), w2_ref[...])


def _cmp_branch(q4b, cmpk, cmpv, tpos4, slope4, tq):
    n_cmp = cmpk.shape[0]
    scale = HEAD_DIM ** -0.5
    s = _dot_nt(q4b, cmpk.astype(BF16)) * scale
    c_end = lax.broadcasted_iota(jnp.int32, (1, n_cmp), 1) * CMP_STRIDE + (CMP_LEN - 1)
    dist = tpos4 - c_end
    valid = dist >= 0
    s = s - slope4 * dist.astype(F32)
    p = _masked_softmax(s, valid)
    o = _dot(p.astype(BF16), cmpv.astype(BF16))
    psum = p[0:tq]
    for g in range(1, NSA_GROUP):
        psum = psum + p[g * tq:(g + 1) * tq]
    return o, psum


def _slopes(kv, tq):
    g = lax.broadcasted_iota(jnp.int32, (NSA_GROUP * tq, 1), 0) // tq
    head = (g + kv * NSA_GROUP + 1).astype(F32)
    return jnp.exp(head * (-(8.0 / NSA_HEADS) * math.log(2.0)))


def _tail_col(tail, idx):
    lane = lax.broadcasted_iota(jnp.int32, tail.shape, 1)
    return jnp.sum(jnp.where(lane == idx, tail, 0.0), axis=1, keepdims=True)


def _combine_branches(tail, kv, o_cmp, o_slc, o_win, tq):
    outs = []
    for g in range(NSA_GROUP):
        head = kv * NSA_GROUP + g
        acc = jnp.zeros((tq, HEAD_DIM), F32)
        for br, o in enumerate((o_cmp, o_slc, o_win)):
            gate = _sigmoid(_tail_col(tail, TAIL_NSAG + br * NSA_HEADS + head))
            acc = acc + gate * o[g * tq:(g + 1) * tq]
        outs.append(acc)
    return outs


def _nsa_prompt_kernel(q_ref, pqk_ref, pqv_ref, pek_ref, pev_ref, w2k_ref, w2v_ref, sk_ref, sv_ref, wk_ref, wv_ref,
                       tail_ref, o_ref, cmpk_scr, cmpv_scr, *, tk):
    tq = q_ref.shape[0]
    t_len = sk_ref.shape[0]
    n_cmp = pqk_ref.shape[0]
    n_blk = t_len // SEL_BLOCK
    kv = pl.program_id(1)
    i = pl.program_id(2)
    t0 = i * tq
    scale = HEAD_DIM ** -0.5

    @pl.when(i == 0)
    def _():
        for pq_ref, pe_ref, w2_ref, scr in ((pqk_ref, pek_ref, w2k_ref, cmpk_scr), (pqv_ref, pev_ref, w2v_ref, cmpv_scr)):
            p = pq_ref[:, 0:HEAD_DIM]
            qn = pltpu.roll(pq_ref[:, HEAD_DIM:2 * HEAD_DIM], n_cmp - 1, 0)
            scr[...] = _compressed_tokens(p + qn + pe_ref[...], w2_ref)

    qb = q_ref[...]
    q4b = jnp.concatenate([qb[:, g * HEAD_DIM:(g + 1) * HEAD_DIM] for g in range(NSA_GROUP)], axis=0).astype(BF16)
    rows4 = NSA_GROUP * tq
    tpos4 = t0 + lax.broadcasted_iota(jnp.int32, (rows4, 1), 0) % tq
    tpos = t0 + lax.broadcasted_iota(jnp.int32, (tq, 1), 0)
    slope4 = _slopes(kv, tq)

    o_cmp, psum = _cmp_branch(q4b, cmpk_scr[...], cmpv_scr[...], tpos4, slope4, tq)
    imp = _dot(psum, _overlap_matrix(n_cmp, n_blk), HIGHEST)
    selb = _select_bias(imp, tpos).astype(BF16)
    selb4 = jnp.concatenate([selb] * NSA_GROUP, axis=0)

    def slc_step(jt, carry):
        m, l, acc = carry
        k0 = pl.multiple_of(jt * tk, tk)
        ks = sk_ref[pl.ds(k0, tk), :].astype(BF16)
        vs = sv_ref[pl.ds(k0, tk), :].astype(BF16)
        s = _dot_nt(q4b, ks) * scale + _dot(selb4, _block_onehot(n_blk, k0, tk))
        dist = tpos4 - (k0 + lax.broadcasted_iota(jnp.int32, (1, tk), 1))
        s = jnp.where(dist >= 0, s - slope4 * dist.astype(F32), NEG_INF)
        m_new = jnp.maximum(m, jnp.max(s, axis=-1, keepdims=True))
        alpha = jnp.exp(m - m_new)
        e = jnp.exp(s - m_new)
        l = alpha * l + jnp.sum(e, axis=-1, keepdims=True)
        acc = alpha * acc + _dot(e.astype(BF16), vs)
        return m_new, l, acc

    n_tiles = (t0 + tq - 1) // tk + 1
    init = (jnp.full((rows4, 1), NEG_INF, F32), jnp.zeros((rows4, 1), F32), jnp.zeros((rows4, HEAD_DIM), F32))
    _, l, acc = lax.fori_loop(0, n_tiles, slc_step, init)
    o_slc = acc / l

    band = min(WINDOW + tq, t_len)
    b0 = pl.multiple_of(jnp.maximum(t0 + tq - band, 0), SUBLANES)
    kw = wk_ref[pl.ds(b0, band), :].astype(BF16)
    vw = wv_ref[pl.ds(b0, band), :].astype(BF16)
    s = _dot_nt(q4b, kw) * scale
    dist = tpos4 - (b0 + lax.broadcasted_iota(jnp.int32, (1, band), 1))
    valid = (dist >= 0) & (dist < WINDOW)
    p = _masked_softmax(s - slope4 * dist.astype(F32), valid)
    o_win = _dot(p.astype(BF16), vw)

    outs = _combine_branches(tail_ref[...], kv, o_cmp, o_slc, o_win, tq)
    for g in range(NSA_GROUP):
        o_ref[:, g * HEAD_DIM:(g + 1) * HEAD_DIM] = outs[g]


def _nsa_prompt_call(z, pqk, pqv, pek, pev, w2k, w2v, n_batch, t_len):
    tq = min(128, t_len)
    tk = min(512, t_len)
    nq = t_len // tq
    n_cmp = t_len // CMP_STRIDE
    qblk0 = OFF_NQ // (NSA_GROUP * HEAD_DIM)
    rowspec = lambda off: pl.BlockSpec((t_len, HEAD_DIM), lambda b, kv, i: (b, off // HEAD_DIM + kv))
    pqspec = pl.BlockSpec((n_cmp, 2 * HEAD_DIM), lambda b, kv, i: (b, kv))
    cspec = lambda shape: pl.BlockSpec(shape, lambda b, kv, i: (0, 0))
    return pl.pallas_call(
        functools.partial(_nsa_prompt_kernel, tk=tk),
        grid=(n_batch, NSA_KV_HEADS, nq),
        in_specs=[pl.BlockSpec((tq, NSA_GROUP * HEAD_DIM), lambda b, kv, i: (b * nq + i, qblk0 + kv)),
                  pqspec, pqspec, cspec((1, HEAD_DIM)), cspec((1, HEAD_DIM)),
                  cspec((HEAD_DIM, HEAD_DIM)), cspec((HEAD_DIM, HEAD_DIM)),
                  rowspec(OFF_ROWS + 2 * KV_WIDTH), rowspec(OFF_ROWS + 3 * KV_WIDTH),
                  rowspec(OFF_ROWS + 4 * KV_WIDTH), rowspec(OFF_ROWS + 5 * KV_WIDTH),
                  pl.BlockSpec((tq, LANES), lambda b, kv, i: (b * nq + i, OFF_TAIL // LANES))],
        out_specs=pl.BlockSpec((tq, NSA_GROUP * HEAD_DIM), lambda b, kv, i: (b * nq + i, kv)),
        out_shape=jax.ShapeDtypeStruct((n_batch * t_len, NSA_WIDTH), F32),
        scratch_shapes=[pltpu.VMEM((n_cmp, HEAD_DIM), F32), pltpu.VMEM((n_cmp, HEAD_DIM), F32)],
        compiler_params=_cparams(("parallel", "parallel", "arbitrary")),
        name="nsa_prompt",
    )(z, pqk, pqv, pek, pev, w2k, w2v, z, z, z, z, z)


def _nsa_sample_kernel(pt_ref, q_ref, new_ref, pqnk_ref, pqnv_ref, pek_ref, pev_ref, w2k_ref, w2v_ref, wink_ref,
                       winv_ref, tail_ref, pqk_hbm, pqv_hbm, sk_hbm, sv_hbm, o_ref,
                       pqk_buf, pqv_buf, sk_buf, sv_buf, wkv_scr, sem, *, n_pages, past_len):
    b = pl.program_id(0)
    nb = pl.num_programs(0)
    t_new = q_ref.shape[0]
    page = PAGE_ROWS
    cpp = page // CMP_STRIDE
    n_keys = sk_buf.shape[1]
    n_cmp = pqk_buf.shape[1] * 2
    n_blk = SEL_BLOCK
    scale = HEAD_DIM ** -0.5

    def copies(seq, slot):
        out = []
        for p in range(n_pages):
            pg = pt_ref[seq, p]
            out.append(pltpu.make_async_copy(pqk_hbm.at[pl.ds(pg * cpp, cpp)], pqk_buf.at[slot, pl.ds(p * cpp, cpp)], sem.at[slot, 0]))
            out.append(pltpu.make_async_copy(pqv_hbm.at[pl.ds(pg * cpp, cpp)], pqv_buf.at[slot, pl.ds(p * cpp, cpp)], sem.at[slot, 1]))
            out.append(pltpu.make_async_copy(sk_hbm.at[pg], sk_buf.at[slot, pl.ds(p * page, page)], sem.at[slot, 2]))
            out.append(pltpu.make_async_copy(sv_hbm.at[pg], sv_buf.at[slot, pl.ds(p * page, page)], sem.at[slot, 3]))
        return out

    slot = b % 2

    @pl.when(b == 0)
    def _():
        for cp in copies(0, 0):
            cp.start()

    @pl.when(b + 1 < nb)
    def _():
        for cp in copies(b + 1, 1 - slot):
            cp.start()

    new = new_ref[...]
    qb = q_ref[...]
    tail = tail_ref[...]
    tq = t_new
    rows4 = NSA_GROUP * tq
    tpos4 = past_len + lax.broadcasted_iota(jnp.int32, (rows4, 1), 0) % tq
    tpos = past_len + lax.broadcasted_iota(jnp.int32, (tq, 1), 0)

    win_len = wink_ref.shape[0]
    band = wkv_scr.shape[1]
    pad_rows = band - win_len - t_new
    for idx, (cache_ref, c0) in enumerate(((wink_ref, 4 * KV_WIDTH), (winv_ref, 5 * KV_WIDTH))):
        wkv_scr[idx, 0:win_len, :] = cache_ref[...]
        wkv_scr[idx, win_len:band, :] = jnp.concatenate(
            [new[:, c0:c0 + KV_WIDTH], jnp.zeros((pad_rows, KV_WIDTH), F32)], axis=0)

    for cp in copies(b, slot):
        cp.wait()

    tail_rows = n_keys - past_len
    for buf, c0 in ((sk_buf, 2 * KV_WIDTH), (sv_buf, 3 * KV_WIDTH)):
        buf[slot, past_len:n_keys, :] = jnp.concatenate(
            [new[:, c0:c0 + KV_WIDTH], jnp.zeros((tail_rows - t_new, KV_WIDTH), F32)], axis=0)

    n_pool = n_pages * cpp
    rown = lax.broadcasted_iota(jnp.int32, (n_pool, 1), 0)
    for kv in range(NSA_KV_HEADS):
        q4b = jnp.concatenate([qb[:, (kv * NSA_GROUP + g) * HEAD_DIM:(kv * NSA_GROUP + g + 1) * HEAD_DIM]
                               for g in range(NSA_GROUP)], axis=0).astype(BF16)
        slope4 = _slopes(kv, tq)
        c0 = kv * 2 * HEAD_DIM
        cmp_tokens = []
        for pq_buf, pqn_ref, pe_ref, w2_ref in ((pqk_buf, pqnk_ref, pek_ref, w2k_ref), (pqv_buf, pqnv_ref, pev_ref, w2v_ref)):
            p = pq_buf[slot, :, c0:c0 + HEAD_DIM]
            qn = pltpu.roll(pq_buf[slot, :, c0 + HEAD_DIM:c0 + 2 * HEAD_DIM], n_pool - 1, 0)
            pn = pqn_ref[:, c0:c0 + HEAD_DIM]
            qnn = pqn_ref[:, c0 + HEAD_DIM:c0 + 2 * HEAD_DIM]
            qn = jnp.where(rown == n_pool - 1, qnn, qn)
            hid_pool = p + qn
            rowt = lax.broadcasted_iota(jnp.int32, (n_cmp - n_pool, 1), 0)
            hid_tail = jnp.where(rowt == 0, pn, 0.0)
            hid = jnp.concatenate([hid_pool, hid_tail], axis=0) + pe_ref[...]
            cmp_tokens.append(_compressed_tokens(hid, w2_ref))
        o_cmp, psum = _cmp_branch(q4b, cmp_tokens[0], cmp_tokens[1], tpos4, slope4, tq)
        imp = _dot(psum, _overlap_matrix(n_cmp, n_blk), HIGHEST)
        selb = _select_bias(imp, tpos).astype(BF16)
        selb4 = jnp.concatenate([selb] * NSA_GROUP, axis=0)

        ks = sk_buf[slot, :, kv * HEAD_DIM:(kv + 1) * HEAD_DIM].astype(BF16)
        vs = sv_buf[slot, :, kv * HEAD_DIM:(kv + 1) * HEAD_DIM].astype(BF16)
        s = _dot_nt(q4b, ks) * scale + _dot(selb4, _block_onehot(n_blk, 0, n_keys))
        dist = tpos4 - lax.broadcasted_iota(jnp.int32, (1, n_keys), 1)
        p = _masked_softmax(s - slope4 * dist.astype(F32), dist >= 0)
        o_slc = _dot(p.astype(BF16), vs)

        kw = wkv_scr[0, :, kv * HEAD_DIM:(kv + 1) * HEAD_DIM].astype(BF16)
        vw = wkv_scr[1, :, kv * HEAD_DIM:(kv + 1) * HEAD_DIM].astype(BF16)
        s = _dot_nt(q4b, kw) * scale
        k_pos = (past_len - win_len) + lax.broadcasted_iota(jnp.int32, (1, band), 1)
        dist = tpos4 - k_pos
        p = _masked_softmax(s - slope4 * dist.astype(F32), (dist >= 0) & (dist < WINDOW))
        o_win = _dot(p.astype(BF16), vw)

        outs = _combine_branches(tail, kv, o_cmp, o_slc, o_win, tq)
        for g in range(NSA_GROUP):
            h0 = (kv * NSA_GROUP + g) * HEAD_DIM
            o_ref[:, h0:h0 + HEAD_DIM] = outs[g]


PAGE_ROWS = 128


def _nsa_sample_call(page_table, z3, pq_new_k, pq_new_v, pek, pev, w2k, w2v, win_k, win_v, pq_pool_k, pq_pool_v,
                     pool_sk, pool_sv):
    bd, t_new, _ = z3.shape
    n_pages = page_table.shape[1]
    past_len = n_pages * PAGE_ROWS
    win_len = win_k.shape[1]
    n_keys = -(-(past_len + t_new) // LANES) * LANES
    n_cmp = 2 * n_pages * (PAGE_ROWS // CMP_STRIDE)
    band = -(-(win_len + t_new) // LANES) * LANES
    assert n_keys // SEL_BLOCK <= SEL_BLOCK and past_len // CMP_STRIDE + 8 <= n_cmp

    def zspec(width, col0):
        return pl.BlockSpec((None, t_new, width), lambda b, pt: (b, 0, col0 // width))

    cspec = lambda shape: pl.BlockSpec(shape, lambda b, pt: (0,) * len(shape))
    pqnspec = pl.BlockSpec((None, 1, 4 * HEAD_DIM), lambda b, pt: (b, 0, 0))
    winspec = pl.BlockSpec((None, win_len, KV_WIDTH), lambda b, pt: (b, 0, 0))
    anyspec = pl.BlockSpec(memory_space=pl.ANY)
    grid_spec = pltpu.PrefetchScalarGridSpec(
        num_scalar_prefetch=1,
        grid=(bd,),
        in_specs=[zspec(NSA_WIDTH, OFF_NQ), zspec(6 * KV_WIDTH, OFF_ROWS), pqnspec, pqnspec,
                  cspec((1, HEAD_DIM)), cspec((1, HEAD_DIM)), cspec((HEAD_DIM, HEAD_DIM)), cspec((HEAD_DIM, HEAD_DIM)),
                  winspec, winspec, zspec(LANES, OFF_TAIL), anyspec, anyspec, anyspec, anyspec],
        out_specs=pl.BlockSpec((None, t_new, NSA_WIDTH), lambda b, pt: (b, 0, 0)),
        scratch_shapes=[pltpu.VMEM((2, n_pages * (PAGE_ROWS // CMP_STRIDE), 4 * HEAD_DIM), F32),
                        pltpu.VMEM((2, n_pages * (PAGE_ROWS // CMP_STRIDE), 4 * HEAD_DIM), F32),
                        pltpu.VMEM((2, n_keys, KV_WIDTH), F32),
                        pltpu.VMEM((2, n_keys, KV_WIDTH), F32),
                        pltpu.VMEM((2, band, KV_WIDTH), F32),
                        pltpu.SemaphoreType.DMA((2, 4))],
    )
    return pl.pallas_call(
        functools.partial(_nsa_sample_kernel, n_pages=n_pages, past_len=past_len),
        grid_spec=grid_spec,
        out_shape=jax.ShapeDtypeStruct((bd, t_new, NSA_WIDTH), F32),
        compiler_params=_cparams(("arbitrary",)),
        name="nsa_sample",
    )(page_table, z3, z3, pq_new_k, pq_new_v, pek, pev, w2k, w2v, win_k, win_v, z3, pq_pool_k, pq_pool_v,
      pool_sk, pool_sv)


def _merge_kernel(ya_ref, yb_ref, ga_ref, gb_ref, wa_ref, wb_ref, wo_ref, x_ref, gate_ref, g_ref, b_ref, o_ref,
                  *, alpha):
    a = _dot(ya_ref[...].astype(BF16), wa_ref[...])
    b = _dot(yb_ref[...].astype(BF16), wb_ref[...])
    mix = _sigmoid(ga_ref[...]) * a + _sigmoid(gb_ref[...]) * b
    y = _dot(mix.astype(BF16), wo_ref[...])
    o_ref[...] = _ln(alpha * x_ref[...] + gate_ref[...] * y) * g_ref[...] + b_ref[...]


def _merge_call(ya, yb, z, wa, wb, wo, x, gate, ln_g, ln_b, rows_per_seq, alpha):
    m, d = x.shape
    tm = min(256, m)
    rspec = lambda w: pl.BlockSpec((tm, w), lambda i: (i, 0))
    cspec = lambda a: pl.BlockSpec(a.shape, lambda i: (0, 0))
    return pl.pallas_call(
        functools.partial(_merge_kernel, alpha=alpha),
        grid=(m // tm,),
        in_specs=[rspec(DN_WIDTH), rspec(NSA_WIDTH),
                  pl.BlockSpec((tm, d), lambda i: (i, OFF_MERGE // d)),
                  pl.BlockSpec((tm, d), lambda i: (i, OFF_MERGE // d + 1)),
                  cspec(wa), cspec(wb), cspec(wo), rspec(d), _rowparam_spec(gate, tm, rows_per_seq),
                  pl.BlockSpec((1, d), lambda i: (0, 0)), pl.BlockSpec((1, d), lambda i: (0, 0))],
        out_specs=rspec(d),
        out_shape=jax.ShapeDtypeStruct((m, d), F32),
        compiler_params=_cparams(("parallel",)),
        name="merge_out_ln",
    )(ya, yb, z, z, wa, wb, wo, x, gate, ln_g.reshape(1, d), ln_b.reshape(1, d))


def _ffn_down_kernel(ua_ref, ub_ref, pa_ref, pb_ref, cwa_ref, cwb_ref, wd_ref, x_ref, gate_ref, g_ref, b_ref, o_ref,
                     acc_scr, *, alpha, rows_per_seq):
    i = pl.program_id(0)
    k = pl.program_id(1)
    tm = ua_ref.shape[0]
    seq_start = (i * tm) % rows_per_seq == 0
    row8 = lax.broadcasted_iota(jnp.int32, (SUBLANES, 1), 0)

    def conv(u_ref, p_ref, w_ref):
        u = u_ref[...]
        w = w_ref[...]
        prev = jnp.where(seq_start, 0.0, p_ref[...])
        y = u * w[FFN_CONV - 1:FFN_CONV, :]
        for sh in range(1, FFN_CONV):
            shifted = pltpu.roll(u, sh, 0)
            head = jnp.where(row8 < sh, pltpu.roll(prev, sh, 0), shifted[:SUBLANES])
            shifted = jnp.concatenate([head, shifted[SUBLANES:]], axis=0)
            y = y + shifted * w[FFN_CONV - 1 - sh:FFN_CONV - sh, :]
        return y

    act = _silu(conv(ua_ref, pa_ref, cwa_ref)) * conv(ub_ref, pb_ref, cwb_ref)
    part = _dot(act.astype(BF16), wd_ref[...])

    @pl.when(k == 0)
    def _():
        acc_scr[...] = part

    @pl.when(k > 0)
    def _():
        acc_scr[...] += part

    @pl.when(k == pl.num_programs(1) - 1)
    def _():
        o_ref[...] = _ln(alpha * x_ref[...] + gate_ref[...] * acc_scr[...]) * g_ref[...] + b_ref[...]


def _ffn_down_call(u, conv_w, wd, x, gate, ln_g, ln_b, rows_per_seq, alpha):
    m, d = x.shape
    d_ff = wd.shape[0]
    tm = min(512, m)
    tk = 512
    nk = d_ff // tk
    sub_per_tile = tm // SUBLANES
    prev_idx = lambda i: jnp.maximum(i * sub_per_tile - 1, 0)
    return pl.pallas_call(
        functools.partial(_ffn_down_kernel, alpha=alpha, rows_per_seq=rows_per_seq),
        grid=(m // tm, nk),
        in_specs=[pl.BlockSpec((tm, tk), lambda i, k: (i, k)),
                  pl.BlockSpec((tm, tk), lambda i, k: (i, nk + k)),
                  pl.BlockSpec((SUBLANES, tk), lambda i, k: (prev_idx(i), k)),
                  pl.BlockSpec((SUBLANES, tk), lambda i, k: (prev_idx(i), nk + k)),
                  pl.BlockSpec((FFN_CONV, tk), lambda i, k: (0, k)),
                  pl.BlockSpec((FFN_CONV, tk), lambda i, k: (0, nk + k)),
                  pl.BlockSpec((tk, d), lambda i, k: (k, 0)),
                  pl.BlockSpec((tm, d), lambda i, k: (i, 0)),
                  _rowparam_spec(gate, tm, rows_per_seq),
                  pl.BlockSpec((1, d), lambda i, k: (0, 0)), pl.BlockSpec((1, d), lambda i, k: (0, 0))],
        out_specs=pl.BlockSpec((tm, d), lambda i, k: (i, 0)),
        out_shape=jax.ShapeDtypeStruct((m, d), F32),
        scratch_shapes=[pltpu.VMEM((tm, d), F32)],
        compiler_params=_cparams(("parallel", "arbitrary")),
        name="ffn_down_ln",
    )(u, u, u, u, conv_w, conv_w, wd, x, gate, ln_g.reshape(1, d), ln_b.reshape(1, d))


def _permute_w_in(w_in):
    d = w_in.shape[0]
    sizes = (3 * DN_WIDTH, DN_HEADS, DN_HEADS, DN_WIDTH, NSA_WIDTH) + (KV_WIDTH,) * 6 + (3 * NSA_HEADS, 2 * d)
    offs = [0]
    for s in sizes:
        offs.append(offs[-1] + s)
    part = lambda i: w_in[:, offs[i]:offs[i + 1]]
    cols = [part(0), part(3), part(12), part(4)] + [part(i) for i in range(5, 11)] + [part(1), part(2), part(11)]
    used = sum(c.shape[1] for c in cols)
    cols.append(jnp.zeros((d, IN_PAD_WIDTH - used), w_in.dtype))
    return jnp.concatenate(cols, axis=1).astype(BF16)


def _pad_groups(x, n_pad):
    b, t, c = x.shape
    return jnp.concatenate([jnp.zeros((b, n_pad, c), x.dtype), x], axis=1).reshape(b * (n_pad + t), c)


def kernel(x_prompt, x_sample, cache_cmp_k, cache_cmp_v, cache_slc_k, cache_slc_v, cache_win_k, cache_win_v, state_dn, state_dn_conv, state_ffn_conv, page_table, c_prompt, c_sample, w_ada, b_ada, w_in, dn_conv_w, dn_a_log, dn_dt_bias, dn_norm_w, cmp_k_w1, cmp_k_pe, cmp_k_w2, cmp_v_w1, cmp_v_pe, cmp_v_w2, w_branch_a, w_branch_b, w_out, ln1_g, ln1_b, ffn_w_up, ffn_conv_w, ffn_w_down, ln2_g, ln2_b):
    depth = w_in.shape[0]
    bp, t_len, d = x_prompt.shape
    bd, t_new, _ = x_sample.shape
    n_phys = cache_cmp_k.shape[1]
    win_len = cache_win_k.shape[2]
    d_ff = ffn_w_down.shape[1]
    alpha = (2 * depth) ** 0.25
    grp = SUBLANES
    assert t_new <= grp - (DN_CONV - 1) and t_new >= DN_CONV - 1 and cache_cmp_k.shape[2] == PAGE_ROWS

    xp = x_prompt.reshape(bp * t_len, d)
    xs = x_sample.reshape(bd * t_new, d)
    c_all = jnp.concatenate([c_prompt, c_sample], axis=0)
    c_rows = -(-c_all.shape[0] // SUBLANES) * SUBLANES
    c_all = jnp.pad(c_all, ((0, c_rows - c_all.shape[0]), (0, 0)))

    outs_p = {k: [] for k in ('cmp_k', 'cmp_v', 'slc_k', 'slc_v', 'win_k', 'win_v', 'dn_state', 'dn_conv', 'ffn_conv')}
    outs_s = {k: [] for k in outs_p}

    for l in range(depth):
        mod = _ada_call(c_all, w_ada[l].astype(BF16), b_ada[l])
        mod_p = mod[:bp].reshape(bp, 6, 1, d)
        mod_s = mod[bp:bp + bd].reshape(bd, 6, d)
        mp = [mod_p[:, i] for i in range(6)]
        ms = [jnp.repeat(mod_s[:, i], t_new, axis=0) for i in range(6)]

        w_in_l = _permute_w_in(w_in[l])
        wpq_k = _pq_weights(cmp_k_w1[l])
        wpq_v = _pq_weights(cmp_v_w1[l])
        pek = _pe_term(cmp_k_pe[l], wpq_k)
        pev = _pe_term(cmp_v_pe[l], wpq_v)
        w2k = cmp_k_w2[l].astype(BF16)
        w2v = cmp_v_w2[l].astype(BF16)
        wa = w_branch_a[l].astype(BF16)
        wb = w_branch_b[l].astype(BF16)
        wo = w_out[l].astype(BF16)
        w_up = ffn_w_up[l].astype(BF16)
        w_dn = ffn_w_down[l].astype(BF16)

        zp = _ln_mm_call(xp, mp[0], mp[1], w_in_l, t_len, "in_proj")
        rows_p = zp[:, OFF_ROWS:OFF_ROWS + 6 * KV_WIDTH].reshape(bp, t_len, 6, NSA_KV_HEADS, HEAD_DIM)
        beta, gc, gl = _gdn_gates_call(zp, OFF_TAIL, dn_a_log[l], dn_dt_bias[l], DN_CHUNK, 0)
        gcrow = gc.T.reshape(DN_HEADS, 1, bp * t_len)
        ya_p, dn_state_p = _gdn_call(zp, OFF_QKV, dn_conv_w[l], zp, OFF_DNGATE, dn_norm_w[l], beta, gc, gl, gcrow,
                                     jnp.zeros((bp, DN_HEADS, DN_DK, DN_DV), F32), rows_per_seq=t_len, n_sub=1,
                                     per_chunk_state=False, n_pad=0)
        chunk_w = CMP_STRIDE * KV_WIDTH
        pqk = _pq_call(zp[:, OFF_ROWS:OFF_ROWS + KV_WIDTH].reshape(-1, chunk_w), wpq_k)
        pqv = _pq_call(zp[:, OFF_ROWS + KV_WIDTH:OFF_ROWS + 2 * KV_WIDTH].reshape(-1, chunk_w), wpq_v)
        yb_p = _nsa_prompt_call(zp, pqk, pqv, pek, pev, w2k, w2v, bp, t_len)
        x1p = _merge_call(ya_p, yb_p, zp, wa, wb, wo, xp, mp[2], ln1_g[l], ln1_b[l], t_len, alpha)
        up = _ln_mm_call(x1p, mp[3], mp[4], w_up, t_len, "ffn_up")
        xp = _ffn_down_call(up, ffn_conv_w[l], w_dn, x1p, mp[5], ln2_g[l], ln2_b[l], t_len, alpha)

        keep = min(WINDOW, t_len)
        for i, name in enumerate(('cmp_k', 'cmp_v', 'slc_k', 'slc_v')):
            outs_p[name].append(rows_p[:, :, i])
        outs_p['win_k'].append(rows_p[:, t_len - keep:, 4])
        outs_p['win_v'].append(rows_p[:, t_len - keep:, 5])
        outs_p['dn_state'].append(dn_state_p)
        outs_p['dn_conv'].append(zp[:, OFF_QKV:OFF_QKV + 3 * DN_WIDTH].reshape(bp, t_len, -1)[:, t_len - (DN_CONV - 1):])
        outs_p['ffn_conv'].append(up[:, :2 * d_ff].reshape(bp, t_len, -1)[:, t_len - (FFN_CONV - 1):])

        zs = _ln_mm_call(xs, ms[0], ms[1], w_in_l, t_new, "in_proj")
        zs3 = zs.reshape(bd, t_new, -1)
        rows_s = zs3[:, :, OFF_ROWS:OFF_ROWS + 6 * KV_WIDTH].reshape(bd, t_new, 6, NSA_KV_HEADS, HEAD_DIM)
        n_pad = grp - t_new
        qkv_g = jnp.concatenate([jnp.zeros((bd, n_pad - (DN_CONV - 1), 3 * DN_WIDTH), F32), state_dn_conv[l],
                                 zs3[:, :, OFF_QKV:OFF_QKV + 3 * DN_WIDTH]], axis=1).reshape(bd * grp, -1)
        zs_g = _pad_groups(jnp.concatenate([zs3[:, :, OFF_DNGATE:OFF_DNGATE + DN_WIDTH],
                                            zs3[:, :, OFF_TAIL:OFF_TAIL + LANES]], axis=-1), n_pad)
        beta, gc, gl = _gdn_gates_call(zs_g, DN_WIDTH, dn_a_log[l], dn_dt_bias[l], grp, n_pad)
        gcrow = gc.T.reshape(DN_HEADS, 1, bd * grp)
        ya_g, dn_state_s = _gdn_call(qkv_g, 0, dn_conv_w[l], zs_g, 0, dn_norm_w[l], beta, gc, gl, gcrow, state_dn[l],
                                     rows_per_seq=grp, n_sub=DN_CHUNK // grp, per_chunk_state=True, n_pad=n_pad)
        ya_s = ya_g.reshape(bd, grp, DN_WIDTH)[:, n_pad:].reshape(bd * t_new, DN_WIDTH)
        pq_pool_k = _pq_call(cache_cmp_k[l].reshape(-1, chunk_w), wpq_k)
        pq_pool_v = _pq_call(cache_cmp_v[l].reshape(-1, chunk_w), wpq_v)
        new_chunk = lambda i: jnp.pad(zs3[:, :, OFF_ROWS + i * KV_WIDTH:OFF_ROWS + (i + 1) * KV_WIDTH],
                                      ((0, 0), (0, CMP_STRIDE - t_new), (0, 0))).reshape(bd, chunk_w)
        pq_new_k = _pq_call(new_chunk(0), wpq_k).reshape(bd, 1, -1)
        pq_new_v = _pq_call(new_chunk(1), wpq_v).reshape(bd, 1, -1)
        win_k = cache_win_k[l].reshape(bd, win_len, KV_WIDTH)
        win_v = cache_win_v[l].reshape(bd, win_len, KV_WIDTH)
        yb_s = _nsa_sample_call(page_table, zs3, pq_new_k, pq_new_v, pek, pev, w2k, w2v, win_k, win_v,
                                pq_pool_k, pq_pool_v,
                                cache_slc_k[l].reshape(n_phys, PAGE_ROWS, KV_WIDTH),
                                cache_slc_v[l].reshape(n_phys, PAGE_ROWS, KV_WIDTH)).reshape(bd * t_new, NSA_WIDTH)
        x1s = _merge_call(ya_s, yb_s, zs, wa, wb, wo, xs, ms[2], ln1_g[l], ln1_b[l], t_new, alpha)
        us = _ln_mm_call(x1s, ms[3], ms[4], w_up, t_new, "ffn_up")
        us3 = us.reshape(bd, t_new, -1)
        u_g = jnp.concatenate([jnp.zeros((bd, n_pad - (FFN_CONV - 1), 2 * d_ff), F32), state_ffn_conv[l], us3],
                              axis=1).reshape(bd * grp, -1)
        x1_g = _pad_groups(x1s.reshape(bd, t_new, d), n_pad)
        gate2_g = jnp.repeat(mod_s[:, 5], grp, axis=0)
        xs_g = _ffn_down_call(u_g, ffn_conv_w[l], w_dn, x1_g, gate2_g, ln2_g[l], ln2_b[l], grp, alpha)
        xs = xs_g.reshape(bd, grp, d)[:, n_pad:].reshape(bd * t_new, d)

        for i, name in enumerate(('cmp_k', 'cmp_v', 'slc_k', 'slc_v')):
            outs_s[name].append(rows_s[:, :, i])
        outs_s['win_k'].append(jnp.concatenate([cache_win_k[l], rows_s[:, :, 4]], axis=1)[:, t_new:])
        outs_s['win_v'].append(jnp.concatenate([cache_win_v[l], rows_s[:, :, 5]], axis=1)[:, t_new:])
        outs_s['dn_state'].append(dn_state_s)
        outs_s['dn_conv'].append(jnp.concatenate([state_dn_conv[l], zs3[:, :, OFF_QKV:OFF_QKV + 3 * DN_WIDTH]], axis=1)[:, t_new:])
        outs_s['ffn_conv'].append(jnp.concatenate([state_ffn_conv[l], us3], axis=1)[:, t_new:])

    order = ('cmp_k', 'cmp_v', 'slc_k', 'slc_v', 'win_k', 'win_v', 'dn_state', 'dn_conv', 'ffn_conv')
    return ((xp.reshape(bp, t_len, d), xs.reshape(bd, t_new, d))
            + tuple(jnp.stack(outs_p[k]) for k in order) + tuple(jnp.stack(outs_s[k]) for k in order))
```

```python
import functools
import math

import jax
import jax.numpy as jnp
from jax import lax
from jax.experimental import pallas as pl
from jax.experimental.pallas import tpu as pltpu

F32 = jnp.float32
BF16 = jnp.bfloat16
HIGHEST = lax.Precision.HIGHEST

DN_HEADS = 8
DN_DK = 128
DN_DV = 128
DN_WIDTH = DN_HEADS * DN_DV
DN_CONV = 4
DN_CHUNK = 64
NSA_HEADS = 8
NSA_KV_HEADS = 2
NSA_GROUP = NSA_HEADS // NSA_KV_HEADS
HEAD_DIM = 128
NSA_WIDTH = NSA_HEADS * HEAD_DIM
KV_WIDTH = NSA_KV_HEADS * HEAD_DIM
CMP_LEN = 32
CMP_STRIDE = 16
SEL_BLOCK = 64
N_SEL = 16
WINDOW = 512
FFN_CONV = 3
LN_EPS = 1e-5
RMS_EPS = 1e-6
NEG_INF = -1e30
FORCE_SCORE = 1e4

LANES = 128
SUBLANES = 8
VMEM_LIMIT_BYTES = 56 * 1024 * 1024
MXU_DIM = 256
GDN_MATMUL_ROWS = MXU_DIM

OFF_QKV = 0
OFF_DNGATE = 3 * DN_WIDTH
OFF_MERGE = OFF_DNGATE + DN_WIDTH
OFF_NQ = OFF_MERGE + 2 * 2048
OFF_ROWS = OFF_NQ + NSA_WIDTH
OFF_TAIL = OFF_ROWS + 6 * KV_WIDTH
TAIL_BETA = 0
TAIL_A = DN_HEADS
TAIL_NSAG = 2 * DN_HEADS
IN_PAD_WIDTH = 11264


def _cparams(sem):
    return pltpu.CompilerParams(dimension_semantics=sem, vmem_limit_bytes=VMEM_LIMIT_BYTES)


def _dot(a, b, precision=None):
    return jnp.dot(a, b, preferred_element_type=F32, precision=precision)


def _dot_nt(a, b, precision=None):
    return lax.dot_general(a, b, (((1,), (1,)), ((), ())), preferred_element_type=F32, precision=precision)


def _dot_tn(a, b, precision=None):
    return lax.dot_general(a, b, (((0,), (0,)), ((), ())), preferred_element_type=F32, precision=precision)


def _sigmoid(x):
    return 1.0 / (1.0 + jnp.exp(-x))


def _silu(x):
    return x * _sigmoid(x)


def _ln(x):
    mu = jnp.mean(x, axis=-1, keepdims=True)
    xc = x - mu
    var = jnp.mean(xc * xc, axis=-1, keepdims=True)
    return xc * lax.rsqrt(var + LN_EPS)


def _rowparam_spec(p, tm, rows_per_seq):
    if p.ndim == 2:
        return pl.BlockSpec((tm, p.shape[1]), lambda i, *_: (i, 0))
    blocks_per_seq = rows_per_seq // tm
    return pl.BlockSpec((None, 1, p.shape[2]), lambda i, *_: (i // blocks_per_seq, 0, 0))


def _ada_kernel(c_ref, w_ref, b_ref, o_ref):
    h = _silu(c_ref[...]).astype(BF16)
    o_ref[...] = _dot(h, w_ref[...]) + b_ref[...]


def _ada_call(c, w_bf16, b):
    m, k = c.shape
    n = w_bf16.shape[1]
    tn = 1024
    return pl.pallas_call(
        _ada_kernel,
        grid=(n // tn,),
        in_specs=[pl.BlockSpec((m, k), lambda j: (0, 0)),
                  pl.BlockSpec((k, tn), lambda j: (0, j)),
                  pl.BlockSpec((1, tn), lambda j: (0, j))],
        out_specs=pl.BlockSpec((m, tn), lambda j: (0, j)),
        out_shape=jax.ShapeDtypeStruct((m, n), F32),
        compiler_params=_cparams(("parallel",)),
        name="ada_mod",
    )(c, w_bf16, b.reshape(1, n))


def _ln_mm_kernel(x_ref, sh_ref, sc_ref, w_ref, o_ref, h_scr):
    @pl.when(pl.program_id(1) == 0)
    def _():
        h = _ln(x_ref[...]) * (1.0 + sc_ref[...]) + sh_ref[...]
        h_scr[...] = h.astype(BF16)

    o_ref[...] = _dot(h_scr[...], w_ref[...])


def _ln_mm_call(x, shift, scale, w_bf16, rows_per_seq, name):
    m, k = x.shape
    n = w_bf16.shape[1]
    tm = min(1024, m)
    tn = 512
    return pl.pallas_call(
        _ln_mm_kernel,
        grid=(m // tm, n // tn),
        in_specs=[pl.BlockSpec((tm, k), lambda i, j: (i, 0)),
                  _rowparam_spec(shift, tm, rows_per_seq),
                  _rowparam_spec(scale, tm, rows_per_seq),
                  pl.BlockSpec((k, tn), lambda i, j: (0, j))],
        out_specs=pl.BlockSpec((tm, tn), lambda i, j: (i, j)),
        out_shape=jax.ShapeDtypeStruct((m, n), F32),
        scratch_shapes=[pltpu.VMEM((tm, k), BF16)],
        compiler_params=_cparams(("parallel", "arbitrary")),
        name=name,
    )(x, shift, scale, w_bf16)


def _softplus(x):
    return jnp.maximum(x, 0.0) + jnp.log1p(jnp.exp(-jnp.abs(x)))


def _gdn_gates_kernel(tail_ref, alog_ref, dtb_ref, beta_ref, gc_ref, gl_ref, *, group, n_pad):
    r = tail_ref.shape[0]
    tail = tail_ref[...]
    braw = tail[:, TAIL_BETA:TAIL_BETA + DN_HEADS]
    araw = tail[:, TAIL_A:TAIL_A + DN_HEADS]
    row = lax.broadcasted_iota(jnp.int32, (r, DN_HEADS), 0)
    real = (row % group) >= n_pad
    g = -jnp.exp(alog_ref[...]) * _softplus(araw + dtb_ref[...])
    g = jnp.where(real, g, 0.0)
    beta = jnp.where(real, _sigmoid(braw), 0.0)
    ri = lax.broadcasted_iota(jnp.int32, (r, r), 0)
    ci = lax.broadcasted_iota(jnp.int32, (r, r), 1)
    same = (ri // group) == (ci // group)
    tri = jnp.where(same & (ri >= ci), 1.0, 0.0).astype(F32)
    ones = jnp.where(same, 1.0, 0.0).astype(F32)
    beta_ref[...] = beta
    gc_ref[...] = _dot(tri, g, HIGHEST)
    gl_ref[...] = _dot(ones, g, HIGHEST)


def _gdn_gates_call(z, tail_off, a_log, dt_bias, group, n_pad):
    m = z.shape[0]
    r = min(512, m)
    tail_blk = tail_off // LANES
    out = jax.ShapeDtypeStruct((m, DN_HEADS), F32)
    ospec = pl.BlockSpec((r, DN_HEADS), lambda i: (i, 0))
    return pl.pallas_call(
        functools.partial(_gdn_gates_kernel, group=group, n_pad=n_pad),
        grid=(m // r,),
        in_specs=[pl.BlockSpec((r, LANES), lambda i: (i, tail_blk)),
                  pl.BlockSpec((1, DN_HEADS), lambda i: (0, 0)),
                  pl.BlockSpec((1, DN_HEADS), lambda i: (0, 0))],
        out_specs=[ospec, ospec, ospec],
        out_shape=[out, out, out],
        compiler_params=_cparams(("parallel",)),
        name="gdn_gates",
    )(z, a_log.reshape(1, DN_HEADS), dt_bias.reshape(1, DN_HEADS))


def _unit_lower_inverse(lower, n_sq):
    c = lower.shape[0]
    eye = (lax.broadcasted_iota(jnp.int32, (c, c), 0) == lax.broadcasted_iota(jnp.int32, (c, c), 1)).astype(F32)
    p = -lower
    t = eye + p
    for _ in range(n_sq):
        p = _dot(p, p, HIGHEST)
        t = t + _dot(t, p, HIGHEST)
    return t


def _gdn_kernel(xq_ref, xk_ref, xv_ref, wq_ref, wk_ref, wv_ref, gate_ref, nw_ref, beta_ref, gc_ref, gl_ref,
                gcrow_ref, s_in_ref, o_ref, s_out_ref, xs_scr, yq_scr, yk_scr, yv_scr,
                *, chunk, sub, per_chunk_state, n_pad):
    c = chunk
    r = xq_ref.shape[0]
    nchunk = r // c
    n_sub = c // sub
    head = pl.program_id(1)
    first = pl.program_id(2) == 0

    @pl.when(first)
    def _():
        s_out_ref[...] = s_in_ref[...]
        xs_scr[:, 0:SUBLANES, :] = jnp.zeros((3, SUBLANES, DN_DK), F32)

    row = lax.broadcasted_iota(jnp.int32, (r, 1), 0)
    real = (row % sub) >= n_pad
    for idx, (x_ref, w_ref, y_scr) in enumerate(((xq_ref, wq_ref, yq_scr), (xk_ref, wk_ref, yk_scr),
                                                 (xv_ref, wv_ref, yv_scr))):
        xs_scr[idx, SUBLANES:SUBLANES + r, :] = x_ref[...]
        w = w_ref[...]
        y = jnp.zeros((r, DN_DK), F32)
        for j in range(DN_CONV):
            off = SUBLANES - (DN_CONV - 1) + j
            y = y + xs_scr[idx, off:off + r, :] * w[j:j + 1, :]
        xs_scr[idx, 0:SUBLANES, :] = xs_scr[idx, r:r + SUBLANES, :]
        y_scr[...] = jnp.where(real, _silu(y), 0.0)

    lane = lax.broadcasted_iota(jnp.int32, (r, DN_HEADS), 1)

    def col(ref):
        return jnp.sum(jnp.where(lane == head, ref[...], 0.0), axis=1, keepdims=True)

    beta_all = col(beta_ref)
    gc_all = col(gc_ref)
    gl_all = col(gl_ref)

    ri = lax.broadcasted_iota(jnp.int32, (c, c), 0)
    ci = lax.broadcasted_iota(jnp.int32, (c, c), 1)
    same = (ri // sub) == (ci // sub)
    causal = same & (ri >= ci)
    strict = same & (ri > ci)
    n_sq = int(math.log2(sub)) - 1
    nw = nw_ref[...]

    for ch in range(nchunk):
        r0 = ch * c
        q = yq_scr[r0:r0 + c, :]
        k = yk_scr[r0:r0 + c, :]
        v = yv_scr[r0:r0 + c, :]
        q = q * lax.rsqrt(jnp.sum(q * q, axis=-1, keepdims=True) + RMS_EPS) * (DN_DK ** -0.5)
        k = k * lax.rsqrt(jnp.sum(k * k, axis=-1, keepdims=True) + RMS_EPS)
        beta = beta_all[r0:r0 + c, :]
        gc = gc_all[r0:r0 + c, :]
        gl = gl_all[r0:r0 + c, :]
        gcrow = gcrow_ref[:, r0:r0 + c]
        diff = gc - gcrow
        decay = jnp.where(causal, jnp.exp(jnp.where(causal, diff, 0.0)), 0.0)
        kb = k * beta
        kbf = k.astype(BF16)
        lower = jnp.where(strict, _dot_nt(kb.astype(BF16), kbf) * decay, 0.0)
        tmat = _unit_lower_inverse(lower, n_sq)
        eg = jnp.exp(gc)
        rhs = jnp.concatenate([v * beta, kb * eg], axis=1)
        uw = _dot(tmat.astype(BF16), rhs.astype(BF16))
        u = uw[:, :DN_DV]
        w = uw[:, DN_DV:]
        intra = jnp.where(causal, _dot_nt(q.astype(BF16), kbf) * decay, 0.0)
        qe = (q * eg).astype(BF16)
        kd = (k * jnp.exp(gl - gc)).astype(BF16)
        wb = w.astype(BF16)
        v_new_parts = []
        o_parts = []
        for s in range(n_sub):
            a0 = s * sub
            sidx = (ch * n_sub + s) if per_chunk_state else 0
            state = s_out_ref[sidx]
            sb = state.astype(BF16)
            v_new = u[a0:a0 + sub, :] - _dot(wb[a0:a0 + sub, :], sb)
            o_parts.append(_dot(qe[a0:a0 + sub, :], sb))
            s_out_ref[sidx] = state * jnp.exp(gl[a0:a0 + 1, :]) + _dot_tn(kd[a0:a0 + sub, :], v_new.astype(BF16))
            v_new_parts.append(v_new)
        v_new = v_new_parts[0] if n_sub == 1 else jnp.concatenate(v_new_parts, axis=0)
        o_state = o_parts[0] if n_sub == 1 else jnp.concatenate(o_parts, axis=0)
        o = o_state + _dot(intra.astype(BF16), v_new.astype(BF16))
        o = o * lax.rsqrt(jnp.mean(o * o, axis=-1, keepdims=True) + RMS_EPS) * nw
        o_ref[r0:r0 + c, :] = o * _silu(gate_ref[r0:r0 + c, :])


def _gdn_call(x, x_col0, conv_w, gate, gate_col0, norm_w, beta, gc, gl, gcrow, s_in, *, rows_per_seq, chunk, sub,
              per_chunk_state, n_pad):
    m = x.shape[0]
    r = min(512, rows_per_seq) if not per_chunk_state else min(512, m)
    chunk = min(chunk, r)
    n_seq = m // rows_per_seq if not per_chunk_state else m // r
    nt = rows_per_seq // r if not per_chunk_state else 1
    n_state_blk = r // sub if per_chunk_state else 1
    xb = x_col0 // DN_DK
    gb = gate_col0 // DN_DV
    hw = DN_WIDTH // DN_DK

    def rowblk(b, h, t):
        return b * nt + t

    xspec = lambda off: pl.BlockSpec((r, DN_DK), lambda b, h, t: (rowblk(b, h, t), xb + off + h))
    wspec = lambda off: pl.BlockSpec((DN_CONV, DN_DK), lambda b, h, t: (0, off + h))
    colspec = pl.BlockSpec((r, DN_HEADS), lambda b, h, t: (rowblk(b, h, t), 0))
    sspec = pl.BlockSpec((n_state_blk, None, DN_DK, DN_DV), lambda b, h, t: (b, h, 0, 0))
    return pl.pallas_call(
        functools.partial(_gdn_kernel, chunk=chunk, sub=sub, per_chunk_state=per_chunk_state, n_pad=n_pad),
        grid=(n_seq, DN_HEADS, nt),
        in_specs=[xspec(0), xspec(hw), xspec(2 * hw), wspec(0), wspec(hw), wspec(2 * hw),
                  pl.BlockSpec((r, DN_DV), lambda b, h, t: (rowblk(b, h, t), gb + h)),
                  pl.BlockSpec((1, DN_DV), lambda b, h, t: (0, 0)),
                  colspec, colspec, colspec,
                  pl.BlockSpec((None, 1, r), lambda b, h, t: (h, 0, rowblk(b, h, t))),
                  sspec],
        out_specs=[pl.BlockSpec((r, DN_DV), lambda b, h, t: (rowblk(b, h, t), h)), sspec],
        out_shape=[jax.ShapeDtypeStruct((m, DN_WIDTH), F32), jax.ShapeDtypeStruct(s_in.shape, F32)],
        scratch_shapes=[pltpu.VMEM((3, r + SUBLANES, DN_DK), F32), pltpu.VMEM((r, DN_DK), F32),
                        pltpu.VMEM((r, DN_DK), F32), pltpu.VMEM((r, DN_DV), F32)],
        compiler_params=_cparams(("parallel", "parallel", "arbitrary")),
        name="gdn_delta",
    )(x, x, x, conv_w, conv_w, conv_w, gate, norm_w.reshape(1, DN_DV), beta, gc, gl, gcrow, s_in)


def _pq_kernel(x_ref, w_ref, o_ref):
    for kv in range(NSA_KV_HEADS):
        acc = jnp.zeros((x_ref.shape[0], 2 * HEAD_DIM), F32)
        for j in range(CMP_STRIDE):
            c0 = j * KV_WIDTH + kv * HEAD_DIM
            acc = acc + _dot(x_ref[:, c0:c0 + HEAD_DIM].astype(BF16), w_ref[j])
        o_ref[:, kv * 2 * HEAD_DIM:(kv + 1) * 2 * HEAD_DIM] = acc


def _pq_weights(w1):
    return jnp.concatenate([w1[:CMP_STRIDE], w1[CMP_STRIDE:]], axis=-1).astype(BF16)


def _pq_call(chunks, wpq):
    nc, width = chunks.shape
    tc = 512 if nc % 512 == 0 else nc
    return pl.pallas_call(
        _pq_kernel,
        grid=(nc // tc,),
        in_specs=[pl.BlockSpec((tc, width), lambda i: (i, 0)),
                  pl.BlockSpec(wpq.shape, lambda i: (0, 0, 0))],
        out_specs=pl.BlockSpec((tc, 4 * HEAD_DIM), lambda i: (i, 0)),
        out_shape=jax.ShapeDtypeStruct((nc, 4 * HEAD_DIM), F32),
        compiler_params=_cparams(("parallel",)),
        name="nsa_compress_pq",
    )(chunks, wpq)


def _pq_pool_kernel(*refs):
    x_refs, w_ref, o_ref = refs[:CMP_STRIDE], refs[CMP_STRIDE], refs[CMP_STRIDE + 1]
    rows = o_ref.shape[1]
    acc = jnp.zeros((rows, 2 * HEAD_DIM), F32)
    for j in range(CMP_STRIDE):
        acc = acc + _dot(x_refs[j][...].reshape(rows, HEAD_DIM).astype(BF16), w_ref[j])
    o_ref[0] = acc[:, :HEAD_DIM]
    o_ref[1] = acc[:, HEAD_DIM:]


def _pq_pool_call(pool, layer, wpq):
    depth, n_phys, page, n_kv, hd = pool.shape
    nc = n_phys * (page // CMP_STRIDE)
    chunks = pool.reshape(depth, nc, CMP_STRIDE, n_kv, hd)
    tc = 512 if nc % 512 == 0 else nc
    xspec = lambda j: pl.BlockSpec((None, tc, None, n_kv, hd), lambda i: (layer, i, j, 0, 0))
    return pl.pallas_call(
        _pq_pool_kernel,
        grid=(nc // tc,),
        in_specs=[xspec(j) for j in range(CMP_STRIDE)] + [pl.BlockSpec(wpq.shape, lambda i: (0, 0, 0))],
        out_specs=pl.BlockSpec((2, tc * n_kv, HEAD_DIM), lambda i: (0, i, 0)),
        out_shape=jax.ShapeDtypeStruct((2, nc * n_kv, HEAD_DIM), F32),
        compiler_params=_cparams(("parallel",)),
        name="nsa_compress_pool",
    )(*([chunks] * CMP_STRIDE), wpq)


def _pe_term(pe, wpq):
    z = jnp.zeros((CMP_STRIDE, HEAD_DIM), F32)
    rows = jnp.stack([jnp.concatenate([pe[:CMP_STRIDE], z], axis=1).reshape(-1),
                      jnp.concatenate([pe[CMP_STRIDE:], z], axis=1).reshape(-1)])
    rows = jnp.concatenate([rows, jnp.zeros((SUBLANES - 2, rows.shape[1]), F32)], axis=0)
    pq = _pq_call(rows, wpq)
    return (pq[0, :HEAD_DIM] + pq[1, HEAD_DIM:2 * HEAD_DIM]).reshape(1, HEAD_DIM)


def _masked_softmax(s, valid):
    s = jnp.where(valid, s, NEG_INF)
    m = jnp.max(s, axis=-1, keepdims=True)
    e = jnp.exp(s - m)
    p = e / jnp.sum(e, axis=-1, keepdims=True)
    return jnp.where(valid, p, 0.0)


def _overlap_matrix(n_cmp, n_blk):
    n = lax.broadcasted_iota(jnp.int32, (n_cmp, n_blk), 0) * CMP_STRIDE
    j = lax.broadcasted_iota(jnp.int32, (n_cmp, n_blk), 1) * SEL_BLOCK
    ov = jnp.maximum(jnp.minimum(n + CMP_LEN, j + SEL_BLOCK) - jnp.maximum(n, j), 0)
    return ov.astype(F32) / CMP_LEN


def _select_bias(imp, tpos):
    t, n_blk = imp.shape
    j = lax.broadcasted_iota(jnp.int32, (t, n_blk), 1)
    q_blk = tpos // SEL_BLOCK
    forced = (j == 0) | (j == q_blk) | (j == q_blk - 1)
    score = jnp.where(forced, FORCE_SCORE, jnp.where(j <= q_blk, imp, -1.0))
    rank = jnp.zeros((t, n_blk), F32)
    for jj in range(n_blk):
        sj = score[:, jj:jj + 1]
        ahead = (sj > score) | ((sj == score) & (j > jj))
        rank = rank + jnp.where(ahead, 1.0, 0.0)
    return jnp.where(rank < N_SEL, 0.0, NEG_INF)


def _block_onehot(n_blk, k0, tk):
    j = lax.broadcasted_iota(jnp.int32, (n_blk, tk), 0)
    s = lax.broadcasted_iota(jnp.int32, (n_blk, tk), 1) + k0
    return jnp.where(s // SEL_BLOCK == j, 1.0, 0.0).astype(BF16)


def _compressed_tokens(hid, w2_ref):
    return _dot(_silu(hid).astype(BF16), w2_ref[...])


def _cmp_branch(q4b, cmpk, cmpv, tpos4, slope4, tq):
    n_cmp = cmpk.shape[0]
    scale = HEAD_DIM ** -0.5
    s = _dot_nt(q4b, cmpk.astype(BF16)) * scale
    c_end = lax.broadcasted_iota(jnp.int32, (1, n_cmp), 1) * CMP_STRIDE + (CMP_LEN - 1)
    dist = tpos4 - c_end
    valid = dist >= 0
    s = s - slope4 * dist.astype(F32)
    p = _masked_softmax(s, valid)
    o = _dot(p.astype(BF16), cmpv.astype(BF16))
    psum = p[0:tq]
    for g in range(1, NSA_GROUP):
        psum = psum + p[g * tq:(g + 1) * tq]
    return o, psum


def _slopes(kv, tq):
    g = lax.broadcasted_iota(jnp.int32, (NSA_GROUP * tq, 1), 0) // tq
    head = (g + kv * NSA_GROUP + 1).astype(F32)
    return jnp.exp(head * (-(8.0 / NSA_HEADS) * math.log(2.0)))


def _tail_col(tail, idx):
    lane = lax.broadcasted_iota(jnp.int32, tail.shape, 1)
    return jnp.sum(jnp.where(lane == idx, tail, 0.0), axis=1, keepdims=True)


def _combine_branches(tail, kv, o_cmp, o_slc, o_win, tq):
    outs = []
    for g in range(NSA_GROUP):
        head = kv * NSA_GROUP + g
        acc = jnp.zeros((tq, HEAD_DIM), F32)
        for br, o in enumerate((o_cmp, o_slc, o_win)):
            gate = _sigmoid(_tail_col(tail, TAIL_NSAG + br * NSA_HEADS + head))
            acc = acc + gate * o[g * tq:(g + 1) * tq]
        outs.append(acc)
    return outs


def _nsa_prompt_kernel(q_ref, pqk_ref, pqv_ref, pek_ref, pev_ref, w2k_ref, w2v_ref, sk_ref, sv_ref, wk_ref, wv_ref,
                       tail_ref, o_ref, cmpk_scr, cmpv_scr, *, tk):
    tq = q_ref.shape[0]
    t_len = sk_ref.shape[0]
    n_cmp = pqk_ref.shape[0]
    n_blk = t_len // SEL_BLOCK
    kv = pl.program_id(1)
    i = pl.program_id(2)
    t0 = i * tq
    scale = HEAD_DIM ** -0.5

    @pl.when(i == 0)
    def _():
        for pq_ref, pe_ref, w2_ref, scr in ((pqk_ref, pek_ref, w2k_ref, cmpk_scr), (pqv_ref, pev_ref, w2v_ref, cmpv_scr)):
            p = pq_ref[:, 0:HEAD_DIM]
            qn = pltpu.roll(pq_ref[:, HEAD_DIM:2 * HEAD_DIM], n_cmp - 1, 0)
            scr[...] = _compressed_tokens(p + qn + pe_ref[...], w2_ref)

    qb = q_ref[...]
    q4b = jnp.concatenate([qb[:, g * HEAD_DIM:(g + 1) * HEAD_DIM] for g in range(NSA_GROUP)], axis=0).astype(BF16)
    rows4 = NSA_GROUP * tq
    tpos4 = t0 + lax.broadcasted_iota(jnp.int32, (rows4, 1), 0) % tq
    tpos = t0 + lax.broadcasted_iota(jnp.int32, (tq, 1), 0)
    slope4 = _slopes(kv, tq)

    o_cmp, psum = _cmp_branch(q4b, cmpk_scr[...], cmpv_scr[...], tpos4, slope4, tq)
    imp = _dot(psum, _overlap_matrix(n_cmp, n_blk), HIGHEST)
    selb = _select_bias(imp, tpos).astype(BF16)
    selb4 = jnp.concatenate([selb] * NSA_GROUP, axis=0)

    def slc_step(jt, carry):
        m, l, acc = carry
        k0 = pl.multiple_of(jt * tk, tk)
        ks = sk_ref[pl.ds(k0, tk), :].astype(BF16)
        vs = sv_ref[pl.ds(k0, tk), :].astype(BF16)
        s = _dot_nt(q4b, ks) * scale + _dot(selb4, _block_onehot(n_blk, k0, tk))
        dist = tpos4 - (k0 + lax.broadcasted_iota(jnp.int32, (1, tk), 1))
        s = jnp.where(dist >= 0, s - slope4 * dist.astype(F32), NEG_INF)
        m_new = jnp.maximum(m, jnp.max(s, axis=-1, keepdims=True))
        alpha = jnp.exp(m - m_new)
        e = jnp.exp(s - m_new)
        l = alpha * l + jnp.sum(e, axis=-1, keepdims=True)
        acc = alpha * acc + _dot(e.astype(BF16), vs)
        return m_new, l, acc

    n_tiles = (t0 + tq - 1) // tk + 1
    init = (jnp.full((rows4, 1), NEG_INF, F32), jnp.zeros((rows4, 1), F32), jnp.zeros((rows4, HEAD_DIM), F32))
    _, l, acc = lax.fori_loop(0, n_tiles, slc_step, init)
    o_slc = acc / l

    band = min(WINDOW + tq, t_len)
    b0 = pl.multiple_of(jnp.maximum(t0 + tq - band, 0), SUBLANES)
    kw = wk_ref[pl.ds(b0, band), :].astype(BF16)
    vw = wv_ref[pl.ds(b0, band), :].astype(BF16)
    s = _dot_nt(q4b, kw) * scale
    dist = tpos4 - (b0 + lax.broadcasted_iota(jnp.int32, (1, band), 1))
    valid = (dist >= 0) & (dist < WINDOW)
    p = _masked_softmax(s - slope4 * dist.astype(F32), valid)
    o_win = _dot(p.astype(BF16), vw)

    outs = _combine_branches(tail_ref[...], kv, o_cmp, o_slc, o_win, tq)
    for g in range(NSA_GROUP):
        o_ref[:, g * HEAD_DIM:(g + 1) * HEAD_DIM] = outs[g]


def _nsa_prompt_call(z, pqk, pqv, pek, pev, w2k, w2v, n_batch, t_len):
    tq = min(128, t_len)
    tk = min(512, t_len)
    nq = t_len // tq
    n_cmp = t_len // CMP_STRIDE
    qblk0 = OFF_NQ // (NSA_GROUP * HEAD_DIM)
    rowspec = lambda off: pl.BlockSpec((t_len, HEAD_DIM), lambda b, kv, i: (b, off // HEAD_DIM + kv))
    pqspec = pl.BlockSpec((n_cmp, 2 * HEAD_DIM), lambda b, kv, i: (b, kv))
    cspec = lambda shape: pl.BlockSpec(shape, lambda b, kv, i: (0, 0))
    return pl.pallas_call(
        functools.partial(_nsa_prompt_kernel, tk=tk),
        grid=(n_batch, NSA_KV_HEADS, nq),
        in_specs=[pl.BlockSpec((tq, NSA_GROUP * HEAD_DIM), lambda b, kv, i: (b * nq + i, qblk0 + kv)),
                  pqspec, pqspec, cspec((1, HEAD_DIM)), cspec((1, HEAD_DIM)),
                  cspec((HEAD_DIM, HEAD_DIM)), cspec((HEAD_DIM, HEAD_DIM)),
                  rowspec(OFF_ROWS + 2 * KV_WIDTH), rowspec(OFF_ROWS + 3 * KV_WIDTH),
                  rowspec(OFF_ROWS + 4 * KV_WIDTH), rowspec(OFF_ROWS + 5 * KV_WIDTH),
                  pl.BlockSpec((tq, LANES), lambda b, kv, i: (b * nq + i, OFF_TAIL // LANES))],
        out_specs=pl.BlockSpec((tq, NSA_GROUP * HEAD_DIM), lambda b, kv, i: (b * nq + i, kv)),
        out_shape=jax.ShapeDtypeStruct((n_batch * t_len, NSA_WIDTH), F32),
        scratch_shapes=[pltpu.VMEM((n_cmp, HEAD_DIM), F32), pltpu.VMEM((n_cmp, HEAD_DIM), F32)],
        compiler_params=_cparams(("parallel", "parallel", "arbitrary")),
        name="nsa_prompt",
    )(z, pqk, pqv, pek, pev, w2k, w2v, z, z, z, z, z)


def _nsa_sample_kernel(pt_ref, q_ref, new_ref, pqnk_ref, pqnv_ref, pek_ref, pev_ref, w2k_ref, w2v_ref, wink_ref,
                       winv_ref, tail_ref, pqk_hbm, pqv_hbm, sk_hbm, sv_hbm, o_ref,
                       pqk_buf, pqv_buf, sk_buf, sv_buf, sem, *, n_pages, past_len, layer):
    b = pl.program_id(0)
    nb = pl.num_programs(0)
    t_new = q_ref.shape[0]
    page = PAGE_ROWS
    cpp = page // CMP_STRIDE
    new_rows = LANES
    n_keys = past_len + new_rows
    n_cmp = pqk_buf.shape[2]
    n_blk = SEL_BLOCK
    scale = HEAD_DIM ** -0.5
    ppr = cpp * NSA_KV_HEADS
    kpr = page * NSA_KV_HEADS

    def copies(seq, slot):
        out = []
        for p in range(n_pages):
            pg = pt_ref[seq, p]
            out.append(pltpu.make_async_copy(pqk_hbm.at[:, pl.ds(pg * ppr, ppr)], pqk_buf.at[slot, :, pl.ds(p * ppr, ppr)], sem.at[slot, 0]))
            out.append(pltpu.make_async_copy(pqv_hbm.at[:, pl.ds(pg * ppr, ppr)], pqv_buf.at[slot, :, pl.ds(p * ppr, ppr)], sem.at[slot, 1]))
            out.append(pltpu.make_async_copy(sk_hbm.at[layer, pg], sk_buf.at[slot, pl.ds(p * kpr, kpr)], sem.at[slot, 2]))
            out.append(pltpu.make_async_copy(sv_hbm.at[layer, pg], sv_buf.at[slot, pl.ds(p * kpr, kpr)], sem.at[slot, 3]))
        return out

    slot = b % 2

    @pl.when(b == 0)
    def _():
        for cp in copies(0, 0):
            cp.start()

    @pl.when(b + 1 < nb)
    def _():
        for cp in copies(b + 1, 1 - slot):
            cp.start()

    new = new_ref[...]
    qb = q_ref[...]
    tail = tail_ref[...]
    tq = t_new
    rows4 = NSA_GROUP * tq
    tpos4 = past_len + lax.broadcasted_iota(jnp.int32, (rows4, 1), 0) % tq
    tpos = past_len + lax.broadcasted_iota(jnp.int32, (tq, 1), 0)

    win_len = wink_ref.shape[0] // NSA_KV_HEADS
    band = win_len + new_rows

    def new_tile(which, kv):
        c0 = which * KV_WIDTH + kv * HEAD_DIM
        return jnp.concatenate([new[:, c0:c0 + HEAD_DIM], jnp.zeros((new_rows - t_new, HEAD_DIM), F32)],
                               axis=0).astype(BF16)

    for cp in copies(b, slot):
        cp.wait()

    n_pool = n_pages * cpp
    rown = lax.broadcasted_iota(jnp.int32, (n_pool, 1), 0)
    onehot = _block_onehot(n_blk, 0, n_keys)
    for kv in range(NSA_KV_HEADS):
        q4b = jnp.concatenate([qb[:, (kv * NSA_GROUP + g) * HEAD_DIM:(kv * NSA_GROUP + g + 1) * HEAD_DIM]
                               for g in range(NSA_GROUP)], axis=0).astype(BF16)
        slope4 = _slopes(kv, tq)
        c0 = kv * 2 * HEAD_DIM
        cmp_tokens = []
        for pq_buf, pqn_ref, pe_ref, w2_ref in ((pqk_buf, pqnk_ref, pek_ref, w2k_ref), (pqv_buf, pqnv_ref, pev_ref, w2v_ref)):
            kv_rows = pl.ds(kv, n_pool, stride=NSA_KV_HEADS)
            p = pq_buf[slot, 0, kv_rows, :]
            qn = pltpu.roll(pq_buf[slot, 1, kv_rows, :], n_pool - 1, 0)
            pn = pqn_ref[:, c0:c0 + HEAD_DIM]
            qnn = pqn_ref[:, c0 + HEAD_DIM:c0 + 2 * HEAD_DIM]
            qn = jnp.where(rown == n_pool - 1, qnn, qn)
            hid_pool = p + qn
            rowt = lax.broadcasted_iota(jnp.int32, (n_cmp - n_pool, 1), 0)
            hid_tail = jnp.where(rowt == 0, pn, 0.0)
            hid = jnp.concatenate([hid_pool, hid_tail], axis=0) + pe_ref[...]
            cmp_tokens.append(_compressed_tokens(hid, w2_ref))
        o_cmp, psum = _cmp_branch(q4b, cmp_tokens[0], cmp_tokens[1], tpos4, slope4, tq)
        imp = _dot(psum, _overlap_matrix(n_cmp, n_blk), HIGHEST)
        selb = _select_bias(imp, tpos).astype(BF16)
        selb4 = jnp.concatenate([selb] * NSA_GROUP, axis=0)

        ks = sk_buf[slot, pl.ds(kv, past_len, stride=NSA_KV_HEADS), :].astype(BF16)
        vs = sv_buf[slot, pl.ds(kv, past_len, stride=NSA_KV_HEADS), :].astype(BF16)
        s = jnp.concatenate([_dot_nt(q4b, ks), _dot_nt(q4b, new_tile(2, kv))], axis=1) * scale + _dot(selb4, onehot)
        dist = tpos4 - lax.broadcasted_iota(jnp.int32, (1, n_keys), 1)
        p = _masked_softmax(s - slope4 * dist.astype(F32), dist >= 0).astype(BF16)
        o_slc = _dot(p[:, :past_len], vs) + _dot(p[:, past_len:], new_tile(3, kv))

        kw = wink_ref[pl.ds(kv, win_len, stride=NSA_KV_HEADS), :].astype(BF16)
        vw = winv_ref[pl.ds(kv, win_len, stride=NSA_KV_HEADS), :].astype(BF16)
        s = jnp.concatenate([_dot_nt(q4b, kw), _dot_nt(q4b, new_tile(4, kv))], axis=1) * scale
        k_pos = (past_len - win_len) + lax.broadcasted_iota(jnp.int32, (1, band), 1)
        dist = tpos4 - k_pos
        p = _masked_softmax(s - slope4 * dist.astype(F32), (dist >= 0) & (dist < WINDOW)).astype(BF16)
        o_win = _dot(p[:, :win_len], vw) + _dot(p[:, win_len:], new_tile(5, kv))

        outs = _combine_branches(tail, kv, o_cmp, o_slc, o_win, tq)
        for g in range(NSA_GROUP):
            h0 = (kv * NSA_GROUP + g) * HEAD_DIM
            o_ref[:, h0:h0 + HEAD_DIM] = outs[g]


PAGE_ROWS = 128


def _kv_rows(x):
    return x.reshape(x.shape[:-3] + (x.shape[-3] * x.shape[-2], x.shape[-1]))


def _nsa_sample_call(page_table, z3, pq_new_k, pq_new_v, pek, pev, w2k, w2v, win_k, win_v, pq_pool_k, pq_pool_v,
                     pool_sk, pool_sv, layer):
    bd, t_new, _ = z3.shape
    n_pages = page_table.shape[1]
    past_len = n_pages * PAGE_ROWS
    win_len = win_k.shape[2]
    n_cmp = 2 * n_pages * (PAGE_ROWS // CMP_STRIDE)
    assert (past_len + LANES) // SEL_BLOCK <= SEL_BLOCK and past_len // CMP_STRIDE + 8 <= n_cmp and t_new <= LANES

    def zspec(width, col0):
        return pl.BlockSpec((None, t_new, width), lambda b, pt: (b, 0, col0 // width))

    cspec = lambda shape: pl.BlockSpec(shape, lambda b, pt: (0,) * len(shape))
    pqnspec = pl.BlockSpec((None, 1, 4 * HEAD_DIM), lambda b, pt: (b, 0, 0))
    winspec = pl.BlockSpec((None, None, win_len * NSA_KV_HEADS, HEAD_DIM), lambda b, pt: (layer, b, 0, 0))
    anyspec = pl.BlockSpec(memory_space=pl.ANY)
    grid_spec = pltpu.PrefetchScalarGridSpec(
        num_scalar_prefetch=1,
        grid=(bd,),
        in_specs=[zspec(NSA_WIDTH, OFF_NQ), zspec(6 * KV_WIDTH, OFF_ROWS), pqnspec, pqnspec,
                  cspec((1, HEAD_DIM)), cspec((1, HEAD_DIM)), cspec((HEAD_DIM, HEAD_DIM)), cspec((HEAD_DIM, HEAD_DIM)),
                  winspec, winspec, zspec(LANES, OFF_TAIL), anyspec, anyspec, anyspec, anyspec],
        out_specs=pl.BlockSpec((None, t_new, NSA_WIDTH), lambda b, pt: (b, 0, 0)),
        scratch_shapes=[pltpu.VMEM((2, 2, n_cmp, HEAD_DIM), F32),
                        pltpu.VMEM((2, 2, n_cmp, HEAD_DIM), F32),
                        pltpu.VMEM((2, past_len * NSA_KV_HEADS, HEAD_DIM), F32),
                        pltpu.VMEM((2, past_len * NSA_KV_HEADS, HEAD_DIM), F32),
                        pltpu.SemaphoreType.DMA((2, 4))],
    )
    return pl.pallas_call(
        functools.partial(_nsa_sample_kernel, n_pages=n_pages, past_len=past_len, layer=layer),
        grid_spec=grid_spec,
        out_shape=jax.ShapeDtypeStruct((bd, t_new, NSA_WIDTH), F32),
        compiler_params=_cparams(("arbitrary",)),
        name="nsa_sample",
    )(page_table, z3, z3, pq_new_k, pq_new_v, pek, pev, w2k, w2v, _kv_rows(win_k), _kv_rows(win_v), z3,
      pq_pool_k, pq_pool_v, _kv_rows(pool_sk), _kv_rows(pool_sv))


def _merge_kernel(ya_ref, yb_ref, ga_ref, gb_ref, wa_ref, wb_ref, wo_ref, x_ref, gate_ref, g_ref, b_ref, o_ref,
                  *, alpha):
    a = _dot(ya_ref[...].astype(BF16), wa_ref[...])
    b = _dot(yb_ref[...].astype(BF16), wb_ref[...])
    mix = _sigmoid(ga_ref[...]) * a + _sigmoid(gb_ref[...]) * b
    y = _dot(mix.astype(BF16), wo_ref[...])
    o_ref[...] = _ln(alpha * x_ref[...] + gate_ref[...] * y) * g_ref[...] + b_ref[...]


def _merge_call(ya, yb, z, wa, wb, wo, x, gate, ln_g, ln_b, rows_per_seq, alpha):
    m, d = x.shape
    tm = min(256, m)
    rspec = lambda w: pl.BlockSpec((tm, w), lambda i: (i, 0))
    cspec = lambda a: pl.BlockSpec(a.shape, lambda i: (0, 0))
    return pl.pallas_call(
        functools.partial(_merge_kernel, alpha=alpha),
        grid=(m // tm,),
        in_specs=[rspec(DN_WIDTH), rspec(NSA_WIDTH),
                  pl.BlockSpec((tm, d), lambda i: (i, OFF_MERGE // d)),
                  pl.BlockSpec((tm, d), lambda i: (i, OFF_MERGE // d + 1)),
                  cspec(wa), cspec(wb), cspec(wo), rspec(d), _rowparam_spec(gate, tm, rows_per_seq),
                  pl.BlockSpec((1, d), lambda i: (0, 0)), pl.BlockSpec((1, d), lambda i: (0, 0))],
        out_specs=rspec(d),
        out_shape=jax.ShapeDtypeStruct((m, d), F32),
        compiler_params=_cparams(("parallel",)),
        name="merge_out_ln",
    )(ya, yb, z, z, wa, wb, wo, x, gate, ln_g.reshape(1, d), ln_b.reshape(1, d))


def _ffn_down_kernel(ua_ref, ub_ref, pa_ref, pb_ref, cwa_ref, cwb_ref, wd_ref, x_ref, gate_ref, g_ref, b_ref, o_ref,
                     acc_scr, *, alpha, rows_per_seq):
    i = pl.program_id(0)
    k = pl.program_id(1)
    tm = ua_ref.shape[0]
    seq_start = (i * tm) % rows_per_seq == 0
    row8 = lax.broadcasted_iota(jnp.int32, (SUBLANES, 1), 0)

    def conv(u_ref, p_ref, w_ref):
        u = u_ref[...]
        w = w_ref[...]
        prev = jnp.where(seq_start, 0.0, p_ref[...])
        y = u * w[FFN_CONV - 1:FFN_CONV, :]
        for sh in range(1, FFN_CONV):
            shifted = pltpu.roll(u, sh, 0)
            head = jnp.where(row8 < sh, pltpu.roll(prev, sh, 0), shifted[:SUBLANES])
            shifted = jnp.concatenate([head, shifted[SUBLANES:]], axis=0)
            y = y + shifted * w[FFN_CONV - 1 - sh:FFN_CONV - sh, :]
        return y

    act = _silu(conv(ua_ref, pa_ref, cwa_ref)) * conv(ub_ref, pb_ref, cwb_ref)
    part = _dot(act.astype(BF16), wd_ref[...])

    @pl.when(k == 0)
    def _():
        acc_scr[...] = part

    @pl.when(k > 0)
    def _():
        acc_scr[...] += part

    @pl.when(k == pl.num_programs(1) - 1)
    def _():
        o_ref[...] = _ln(alpha * x_ref[...] + gate_ref[...] * acc_scr[...]) * g_ref[...] + b_ref[...]


def _ffn_down_call(u, conv_w, wd, x, gate, ln_g, ln_b, rows_per_seq, alpha):
    m, d = x.shape
    d_ff = wd.shape[0]
    tm = min(512, m)
    tk = 512
    nk = d_ff // tk
    sub_per_tile = tm // SUBLANES
    prev_idx = lambda i: jnp.maximum(i * sub_per_tile - 1, 0)
    return pl.pallas_call(
        functools.partial(_ffn_down_kernel, alpha=alpha, rows_per_seq=rows_per_seq),
        grid=(m // tm, nk),
        in_specs=[pl.BlockSpec((tm, tk), lambda i, k: (i, k)),
                  pl.BlockSpec((tm, tk), lambda i, k: (i, nk + k)),
                  pl.BlockSpec((SUBLANES, tk), lambda i, k: (prev_idx(i), k)),
                  pl.BlockSpec((SUBLANES, tk), lambda i, k: (prev_idx(i), nk + k)),
                  pl.BlockSpec((FFN_CONV, tk), lambda i, k: (0, k)),
                  pl.BlockSpec((FFN_CONV, tk), lambda i, k: (0, nk + k)),
                  pl.BlockSpec((tk, d), lambda i, k: (k, 0)),
                  pl.BlockSpec((tm, d), lambda i, k: (i, 0)),
                  _rowparam_spec(gate, tm, rows_per_seq),
                  pl.BlockSpec((1, d), lambda i, k: (0, 0)), pl.BlockSpec((1, d), lambda i, k: (0, 0))],
        out_specs=pl.BlockSpec((tm, d), lambda i, k: (i, 0)),
        out_shape=jax.ShapeDtypeStruct((m, d), F32),
        scratch_shapes=[pltpu.VMEM((tm, d), F32)],
        compiler_params=_cparams(("parallel", "arbitrary")),
        name="ffn_down_ln",
    )(u, u, u, u, conv_w, conv_w, wd, x, gate, ln_g.reshape(1, d), ln_b.reshape(1, d))


def _permute_w_in(w_in):
    d = w_in.shape[0]
    sizes = (3 * DN_WIDTH, DN_HEADS, DN_HEADS, DN_WIDTH, NSA_WIDTH) + (KV_WIDTH,) * 6 + (3 * NSA_HEADS, 2 * d)
    offs = [0]
    for s in sizes:
        offs.append(offs[-1] + s)
    part = lambda i: w_in[:, offs[i]:offs[i + 1]]
    cols = [part(0), part(3), part(12), part(4)] + [part(i) for i in range(5, 11)] + [part(1), part(2), part(11)]
    used = sum(c.shape[1] for c in cols)
    cols.append(jnp.zeros((d, IN_PAD_WIDTH - used), w_in.dtype))
    return jnp.concatenate(cols, axis=1).astype(BF16)


def _pad_groups(x, n_pad):
    b, t, c = x.shape
    return jnp.concatenate([jnp.zeros((b, n_pad, c), x.dtype), x], axis=1).reshape(b * (n_pad + t), c)


def kernel(x_prompt, x_sample, cache_cmp_k, cache_cmp_v, cache_slc_k, cache_slc_v, cache_win_k, cache_win_v, state_dn, state_dn_conv, state_ffn_conv, page_table, c_prompt, c_sample, w_ada, b_ada, w_in, dn_conv_w, dn_a_log, dn_dt_bias, dn_norm_w, cmp_k_w1, cmp_k_pe, cmp_k_w2, cmp_v_w1, cmp_v_pe, cmp_v_w2, w_branch_a, w_branch_b, w_out, ln1_g, ln1_b, ffn_w_up, ffn_conv_w, ffn_w_down, ln2_g, ln2_b):
    depth = w_in.shape[0]
    bp, t_len, d = x_prompt.shape
    bd, t_new, _ = x_sample.shape
    n_phys = cache_cmp_k.shape[1]
    win_len = cache_win_k.shape[2]
    d_ff = ffn_w_down.shape[1]
    alpha = (2 * depth) ** 0.25
    grp = SUBLANES
    assert t_new <= grp - (DN_CONV - 1) and t_new >= DN_CONV - 1 and cache_cmp_k.shape[2] == PAGE_ROWS

    xp = x_prompt.reshape(bp * t_len, d)
    xs = x_sample.reshape(bd * t_new, d)
    c_all = jnp.concatenate([c_prompt, c_sample], axis=0)
    c_rows = -(-c_all.shape[0] // SUBLANES) * SUBLANES
    c_all = jnp.pad(c_all, ((0, c_rows - c_all.shape[0]), (0, 0)))

    outs_p = {k: [] for k in ('cmp_k', 'cmp_v', 'slc_k', 'slc_v', 'win_k', 'win_v', 'dn_state', 'dn_conv', 'ffn_conv')}
    outs_s = {k: [] for k in outs_p}

    for l in range(depth):
        mod = _ada_call(c_all, w_ada[l].astype(BF16), b_ada[l])
        mod_p = mod[:bp].reshape(bp, 6, 1, d)
        mod_s = mod[bp:bp + bd].reshape(bd, 6, d)
        mp = [mod_p[:, i] for i in range(6)]
        ms = [jnp.repeat(mod_s[:, i], t_new, axis=0) for i in range(6)]

        w_in_l = _permute_w_in(w_in[l])
        wpq_k = _pq_weights(cmp_k_w1[l])
        wpq_v = _pq_weights(cmp_v_w1[l])
        pek = _pe_term(cmp_k_pe[l], wpq_k)
        pev = _pe_term(cmp_v_pe[l], wpq_v)
        w2k = cmp_k_w2[l].astype(BF16)
        w2v = cmp_v_w2[l].astype(BF16)
        wa = w_branch_a[l].astype(BF16)
        wb = w_branch_b[l].astype(BF16)
        wo = w_out[l].astype(BF16)
        w_up = ffn_w_up[l].astype(BF16)
        w_dn = ffn_w_down[l].astype(BF16)

        zp = _ln_mm_call(xp, mp[0], mp[1], w_in_l, t_len, "in_proj")
        rows_p = zp[:, OFF_ROWS:OFF_ROWS + 6 * KV_WIDTH].reshape(bp, t_len, 6, NSA_KV_HEADS, HEAD_DIM)
        beta, gc, gl = _gdn_gates_call(zp, OFF_TAIL, dn_a_log[l], dn_dt_bias[l], DN_CHUNK, 0)
        gcrow = gc.T.reshape(DN_HEADS, 1, bp * t_len)
        ya_p, dn_state_p = _gdn_call(zp, OFF_QKV, dn_conv_w[l], zp, OFF_DNGATE, dn_norm_w[l], beta, gc, gl, gcrow,
                                     jnp.zeros((bp, DN_HEADS, DN_DK, DN_DV), F32), rows_per_seq=t_len, chunk=GDN_MATMUL_ROWS,
                                     sub=math.gcd(DN_CHUNK, t_len),
                                     per_chunk_state=False, n_pad=0)
        chunk_w = CMP_STRIDE * KV_WIDTH
        pqk = _pq_call(zp[:, OFF_ROWS:OFF_ROWS + KV_WIDTH].reshape(-1, chunk_w), wpq_k)
        pqv = _pq_call(zp[:, OFF_ROWS + KV_WIDTH:OFF_ROWS + 2 * KV_WIDTH].reshape(-1, chunk_w), wpq_v)
        yb_p = _nsa_prompt_call(zp, pqk, pqv, pek, pev, w2k, w2v, bp, t_len)
        x1p = _merge_call(ya_p, yb_p, zp, wa, wb, wo, xp, mp[2], ln1_g[l], ln1_b[l], t_len, alpha)
        up = _ln_mm_call(x1p, mp[3], mp[4], w_up, t_len, "ffn_up")
        xp = _ffn_down_call(up, ffn_conv_w[l], w_dn, x1p, mp[5], ln2_g[l], ln2_b[l], t_len, alpha)

        keep = min(WINDOW, t_len)
        for i, name in enumerate(('cmp_k', 'cmp_v', 'slc_k', 'slc_v')):
            outs_p[name].append(rows_p[:, :, i])
        outs_p['win_k'].append(rows_p[:, t_len - keep:, 4])
        outs_p['win_v'].append(rows_p[:, t_len - keep:, 5])
        outs_p['dn_state'].append(dn_state_p)
        outs_p['dn_conv'].append(zp[:, OFF_QKV:OFF_QKV + 3 * DN_WIDTH].reshape(bp, t_len, -1)[:, t_len - (DN_CONV - 1):])
        outs_p['ffn_conv'].append(up[:, :2 * d_ff].reshape(bp, t_len, -1)[:, t_len - (FFN_CONV - 1):])

        zs = _ln_mm_call(xs, ms[0], ms[1], w_in_l, t_new, "in_proj")
        zs3 = zs.reshape(bd, t_new, -1)
        rows_s = zs3[:, :, OFF_ROWS:OFF_ROWS + 6 * KV_WIDTH].reshape(bd, t_new, 6, NSA_KV_HEADS, HEAD_DIM)
        n_pad = grp - t_new
        qkv_g = jnp.concatenate([jnp.zeros((bd, n_pad - (DN_CONV - 1), 3 * DN_WIDTH), F32), state_dn_conv[l],
                                 zs3[:, :, OFF_QKV:OFF_QKV + 3 * DN_WIDTH]], axis=1).reshape(bd * grp, -1)
        zs_g = _pad_groups(jnp.concatenate([zs3[:, :, OFF_DNGATE:OFF_DNGATE + DN_WIDTH],
                                            zs3[:, :, OFF_TAIL:OFF_TAIL + LANES]], axis=-1), n_pad)
        beta, gc, gl = _gdn_gates_call(zs_g, DN_WIDTH, dn_a_log[l], dn_dt_bias[l], grp, n_pad)
        gcrow = gc.T.reshape(DN_HEADS, 1, bd * grp)
        ya_g, dn_state_s = _gdn_call(qkv_g, 0, dn_conv_w[l], zs_g, 0, dn_norm_w[l], beta, gc, gl, gcrow, state_dn[l],
                                     rows_per_seq=grp, chunk=GDN_MATMUL_ROWS, sub=grp, per_chunk_state=True, n_pad=n_pad)
        ya_s = ya_g.reshape(bd, grp, DN_WIDTH)[:, n_pad:].reshape(bd * t_new, DN_WIDTH)
        pq_pool_k = _pq_pool_call(cache_cmp_k, l, wpq_k)
        pq_pool_v = _pq_pool_call(cache_cmp_v, l, wpq_v)
        new_chunk = lambda i: jnp.pad(zs3[:, :, OFF_ROWS + i * KV_WIDTH:OFF_ROWS + (i + 1) * KV_WIDTH],
                                      ((0, 0), (0, CMP_STRIDE - t_new), (0, 0))).reshape(bd, chunk_w)
        pq_new_k = _pq_call(new_chunk(0), wpq_k).reshape(bd, 1, -1)
        pq_new_v = _pq_call(new_chunk(1), wpq_v).reshape(bd, 1, -1)
        yb_s = _nsa_sample_call(page_table, zs3, pq_new_k, pq_new_v, pek, pev, w2k, w2v, cache_win_k, cache_win_v,
                                pq_pool_k, pq_pool_v, cache_slc_k, cache_slc_v, l).reshape(bd * t_new, NSA_WIDTH)
        x1s = _merge_call(ya_s, yb_s, zs, wa, wb, wo, xs, ms[2], ln1_g[l], ln1_b[l], t_new, alpha)
        us = _ln_mm_call(x1s, ms[3], ms[4], w_up, t_new, "ffn_up")
        us3 = us.reshape(bd, t_new, -1)
        u_g = jnp.concatenate([jnp.zeros((bd, n_pad - (FFN_CONV - 1), 2 * d_ff), F32), state_ffn_conv[l], us3],
                              axis=1).reshape(bd * grp, -1)
        x1_g = _pad_groups(x1s.reshape(bd, t_new, d), n_pad)
        gate2_g = jnp.repeat(mod_s[:, 5], grp, axis=0)
        xs_g = _ffn_down_call(u_g, ffn_conv_w[l], w_dn, x1_g, gate2_g, ln2_g[l], ln2_b[l], grp, alpha)
        xs = xs_g.reshape(bd, grp, d)[:, n_pad:].reshape(bd * t_new, d)

        for i, name in enumerate(('cmp_k', 'cmp_v', 'slc_k', 'slc_v')):
            outs_s[name].append(rows_s[:, :, i])
        outs_s['win_k'].append(jnp.concatenate([cache_win_k[l], rows_s[:, :, 4]], axis=1)[:, t_new:])
        outs_s['win_v'].append(jnp.concatenate([cache_win_v[l], rows_s[:, :, 5]], axis=1)[:, t_new:])
        outs_s['dn_state'].append(dn_state_s)
        outs_s['dn_conv'].append(jnp.concatenate([state_dn_conv[l], zs3[:, :, OFF_QKV:OFF_QKV + 3 * DN_WIDTH]], axis=1)[:, t_new:])
        outs_s['ffn_conv'].append(jnp.concatenate([state_ffn_conv[l], us3], axis=1)[:, t_new:])

    order = ('cmp_k', 'cmp_v', 'slc_k', 'slc_v', 'win_k', 'win_v', 'dn_state', 'dn_conv', 'ffn_conv')
    return ((xp.reshape(bp, t_len, d), xs.reshape(bd, t_new, d))
            + tuple(jnp.stack(outs_p[k]) for k in order) + tuple(jnp.stack(outs_s[k]) for k in order))
```

```python
import functools
import math

import jax
import jax.numpy as jnp
from jax import lax
from jax.experimental import pallas as pl
from jax.experimental.pallas import tpu as pltpu

F32 = jnp.float32
BF16 = jnp.bfloat16
HIGHEST = lax.Precision.HIGHEST

DN_HEADS = 8
DN_DK = 128
DN_DV = 128
DN_WIDTH = DN_HEADS * DN_DV
DN_CONV = 4
DN_CHUNK = 64
NSA_HEADS = 8
NSA_KV_HEADS = 2
NSA_GROUP = NSA_HEADS // NSA_KV_HEADS
HEAD_DIM = 128
NSA_WIDTH = NSA_HEADS * HEAD_DIM
KV_WIDTH = NSA_KV_HEADS * HEAD_DIM
CMP_LEN = 32
CMP_STRIDE = 16
SEL_BLOCK = 64
N_SEL = 16
WINDOW = 512
FFN_CONV = 3
LN_EPS = 1e-5
RMS_EPS = 1e-6
NEG_INF = -1e30
FORCE_SCORE = 1e4

LANES = 128
SUBLANES = 8
BF16_ROWS = 16
VMEM_LIMIT_BYTES = 56 * 1024 * 1024
MXU_DIM = 256
GDN_MATMUL_ROWS = MXU_DIM

OFF_QKV = 0
OFF_DNGATE = 3 * DN_WIDTH
OFF_MERGE = OFF_DNGATE + DN_WIDTH
OFF_NQ = OFF_MERGE + 2 * 2048
OFF_ROWS = OFF_NQ + NSA_WIDTH
OFF_TAIL = OFF_ROWS + 6 * KV_WIDTH
TAIL_BETA = 0
TAIL_A = DN_HEADS
TAIL_NSAG = 2 * DN_HEADS
IN_PAD_WIDTH = 11264


def _cparams(sem):
    return pltpu.CompilerParams(dimension_semantics=sem, vmem_limit_bytes=VMEM_LIMIT_BYTES)


def _dot(a, b, precision=None):
    return jnp.dot(a, b, preferred_element_type=F32, precision=precision)


def _dot_nt(a, b, precision=None):
    return lax.dot_general(a, b, (((1,), (1,)), ((), ())), preferred_element_type=F32, precision=precision)


def _dot_tn(a, b, precision=None):
    return lax.dot_general(a, b, (((0,), (0,)), ((), ())), preferred_element_type=F32, precision=precision)


def _split_bf16(x):
    hi = x.astype(BF16)
    return hi, (x - hi.astype(F32)).astype(BF16)


def _dot_split(a, b):
    (ah, al), (bh, bl) = a, b
    return _dot(ah, bh) + (_dot(al, bh) + _dot(ah, bl))


def _sigmoid(x):
    return 1.0 / (1.0 + jnp.exp(-x))


def _silu(x):
    return x * _sigmoid(x)


def _ln(x):
    mu = jnp.mean(x, axis=-1, keepdims=True)
    xc = x - mu
    var = jnp.mean(xc * xc, axis=-1, keepdims=True)
    return xc * lax.rsqrt(var + LN_EPS)


def _rowparam_spec(p, tm, rows_per_seq):
    if p.ndim == 2:
        return pl.BlockSpec((tm, p.shape[1]), lambda i, *_: (i, 0))
    blocks_per_seq = rows_per_seq // tm
    return pl.BlockSpec((None, 1, p.shape[2]), lambda i, *_: (i // blocks_per_seq, 0, 0))


def _ada_kernel(c_ref, w_ref, b_ref, o_ref):
    h = _silu(c_ref[...]).astype(BF16)
    o_ref[...] = _dot(h, w_ref[...]) + b_ref[...]


def _ada_call(c, w_bf16, layer, b):
    m, k = c.shape
    n = w_bf16.shape[2]
    tn = 1024
    return pl.pallas_call(
        _ada_kernel,
        grid=(n // tn,),
        in_specs=[pl.BlockSpec((m, k), lambda j: (0, 0)),
                  pl.BlockSpec((None, k, tn), lambda j: (layer, 0, j)),
                  pl.BlockSpec((1, tn), lambda j: (0, j))],
        out_specs=pl.BlockSpec((m, tn), lambda j: (0, j)),
        out_shape=jax.ShapeDtypeStruct((m, n), F32),
        compiler_params=_cparams(("parallel",)),
        name="ada_mod",
    )(c, w_bf16, b.reshape(1, n))


def _ln_mm_kernel(x_ref, sh_ref, sc_ref, w_ref, o_ref, h_scr):
    @pl.when(pl.program_id(1) == 0)
    def _():
        h = _ln(x_ref[...]) * (1.0 + sc_ref[...]) + sh_ref[...]
        h_scr[...] = h.astype(BF16)

    o_ref[...] = _dot(h_scr[...], w_ref[...])


def _ln_mm_call(x, shift, scale, w_bf16, layer, rows_per_seq, name):
    m, k = x.shape
    n = w_bf16.shape[2]
    tm = min(1024, m)
    tn = 512
    return pl.pallas_call(
        _ln_mm_kernel,
        grid=(m // tm, n // tn),
        in_specs=[pl.BlockSpec((tm, k), lambda i, j: (i, 0)),
                  _rowparam_spec(shift, tm, rows_per_seq),
                  _rowparam_spec(scale, tm, rows_per_seq),
                  pl.BlockSpec((None, k, tn), lambda i, j: (layer, 0, j))],
        out_specs=pl.BlockSpec((tm, tn), lambda i, j: (i, j)),
        out_shape=jax.ShapeDtypeStruct((m, n), F32),
        scratch_shapes=[pltpu.VMEM((tm, k), BF16)],
        compiler_params=_cparams(("parallel", "arbitrary")),
        name=name,
    )(x, shift, scale, w_bf16)


def _softplus(x):
    return jnp.maximum(x, 0.0) + jnp.log1p(jnp.exp(-jnp.abs(x)))


def _gdn_gates_kernel(tail_ref, alog_ref, dtb_ref, beta_ref, gc_ref, gl_ref, *, group, n_pad):
    r = tail_ref.shape[0]
    tail = tail_ref[...]
    braw = tail[:, TAIL_BETA:TAIL_BETA + DN_HEADS]
    araw = tail[:, TAIL_A:TAIL_A + DN_HEADS]
    row = lax.broadcasted_iota(jnp.int32, (r, DN_HEADS), 0)
    real = (row % group) >= n_pad
    g = -jnp.exp(alog_ref[...]) * _softplus(araw + dtb_ref[...])
    g = jnp.where(real, g, 0.0)
    beta = jnp.where(real, _sigmoid(braw), 0.0)
    ri = lax.broadcasted_iota(jnp.int32, (r, r), 0)
    ci = lax.broadcasted_iota(jnp.int32, (r, r), 1)
    same = (ri // group) == (ci // group)
    tri = jnp.where(same & (ri >= ci), 1.0, 0.0).astype(F32)
    ones = jnp.where(same, 1.0, 0.0).astype(F32)
    beta_ref[...] = beta
    gc_ref[...] = _dot(tri, g, HIGHEST)
    gl_ref[...] = _dot(ones, g, HIGHEST)


def _gdn_gates_call(z, tail_off, a_log, dt_bias, group, n_pad):
    m = z.shape[0]
    r = min(512, m)
    tail_blk = tail_off // LANES
    out = jax.ShapeDtypeStruct((m, DN_HEADS), F32)
    ospec = pl.BlockSpec((r, DN_HEADS), lambda i: (i, 0))
    return pl.pallas_call(
        functools.partial(_gdn_gates_kernel, group=group, n_pad=n_pad),
        grid=(m // r,),
        in_specs=[pl.BlockSpec((r, LANES), lambda i: (i, tail_blk)),
                  pl.BlockSpec((1, DN_HEADS), lambda i: (0, 0)),
                  pl.BlockSpec((1, DN_HEADS), lambda i: (0, 0))],
        out_specs=[ospec, ospec, ospec],
        out_shape=[out, out, out],
        compiler_params=_cparams(("parallel",)),
        name="gdn_gates",
    )(z, a_log.reshape(1, DN_HEADS), dt_bias.reshape(1, DN_HEADS))


def _unit_lower_inverse(lower, n_sq):
    c = lower.shape[0]
    eye = (lax.broadcasted_iota(jnp.int32, (c, c), 0) == lax.broadcasted_iota(jnp.int32, (c, c), 1)).astype(F32)
    p = -lower
    t = eye + p
    ps = _split_bf16(p)
    for _ in range(n_sq):
        p = _dot_split(ps, ps)
        ps = _split_bf16(p)
        t = t + _dot_split(_split_bf16(t), ps)
    return t


def _gdn_kernel(xq_ref, xk_ref, xv_ref, wq_ref, wk_ref, wv_ref, gate_ref, nw_ref, beta_ref, gc_ref, gl_ref,
                gcrow_ref, s_in_ref, o_ref, s_out_ref, xs_scr, yq_scr, yk_scr, yv_scr,
                *, chunk, sub, per_chunk_state, n_pad):
    c = chunk
    r = xq_ref.shape[0]
    nchunk = r // c
    n_sub = c // sub
    head = pl.program_id(1)
    first = pl.program_id(2) == 0

    @pl.when(first)
    def _():
        s_out_ref[...] = s_in_ref[...]
        xs_scr[:, 0:SUBLANES, :] = jnp.zeros((3, SUBLANES, DN_DK), F32)

    row = lax.broadcasted_iota(jnp.int32, (r, 1), 0)
    real = (row % sub) >= n_pad
    for idx, (x_ref, w_ref, y_scr) in enumerate(((xq_ref, wq_ref, yq_scr), (xk_ref, wk_ref, yk_scr),
                                                 (xv_ref, wv_ref, yv_scr))):
        xs_scr[idx, SUBLANES:SUBLANES + r, :] = x_ref[...]
        w = w_ref[...]
        y = jnp.zeros((r, DN_DK), F32)
        for j in range(DN_CONV):
            off = SUBLANES - (DN_CONV - 1) + j
            y = y + xs_scr[idx, off:off + r, :] * w[j:j + 1, :]
        xs_scr[idx, 0:SUBLANES, :] = xs_scr[idx, r:r + SUBLANES, :]
        y_scr[...] = jnp.where(real, _silu(y), 0.0)

    lane = lax.broadcasted_iota(jnp.int32, (r, DN_HEADS), 1)

    def col(ref):
        return jnp.sum(jnp.where(lane == head, ref[...], 0.0), axis=1, keepdims=True)

    beta_all = col(beta_ref)
    gc_all = col(gc_ref)
    gl_all = col(gl_ref)

    ri = lax.broadcasted_iota(jnp.int32, (c, c), 0)
    ci = lax.broadcasted_iota(jnp.int32, (c, c), 1)
    same = (ri // sub) == (ci // sub)
    causal = same & (ri >= ci)
    strict = same & (ri > ci)
    n_sq = int(math.log2(sub)) - 1
    nw = nw_ref[...]

    for ch in range(nchunk):
        r0 = ch * c
        q = yq_scr[r0:r0 + c, :]
        k = yk_scr[r0:r0 + c, :]
        v = yv_scr[r0:r0 + c, :]
        q = q * lax.rsqrt(jnp.sum(q * q, axis=-1, keepdims=True) + RMS_EPS) * (DN_DK ** -0.5)
        k = k * lax.rsqrt(jnp.sum(k * k, axis=-1, keepdims=True) + RMS_EPS)
        beta = beta_all[r0:r0 + c, :]
        gc = gc_all[r0:r0 + c, :]
        gl = gl_all[r0:r0 + c, :]
        gcrow = gcrow_ref[:, r0:r0 + c]
        diff = gc - gcrow
        decay = jnp.where(causal, jnp.exp(jnp.where(causal, diff, 0.0)), 0.0)
        kb = k * beta
        kbf = k.astype(BF16)
        lower = jnp.where(strict, _dot_nt(kb.astype(BF16), kbf) * decay, 0.0)
        tmat = _unit_lower_inverse(lower, n_sq)
        eg = jnp.exp(gc)
        rhs = jnp.concatenate([v * beta, kb * eg], axis=1)
        uw = _dot(tmat.astype(BF16), rhs.astype(BF16))
        u = uw[:, :DN_DV]
        w = uw[:, DN_DV:]
        intra = jnp.where(causal, _dot_nt(q.astype(BF16), kbf) * decay, 0.0)
        qe = (q * eg).astype(BF16)
        kd = (k * jnp.exp(gl - gc)).astype(BF16)
        wb = w.astype(BF16)
        v_new_parts = []
        o_parts = []
        for s in range(n_sub):
            a0 = s * sub
            sidx = (ch * n_sub + s) if per_chunk_state else 0
            state = s_out_ref[sidx]
            sb = state.astype(BF16)
            v_new = u[a0:a0 + sub, :] - _dot(wb[a0:a0 + sub, :], sb)
            o_parts.append(_dot(qe[a0:a0 + sub, :], sb))
            s_out_ref[sidx] = state * jnp.exp(gl[a0:a0 + 1, :]) + _dot_tn(kd[a0:a0 + sub, :], v_new.astype(BF16))
            v_new_parts.append(v_new)
        v_new = v_new_parts[0] if n_sub == 1 else jnp.concatenate(v_new_parts, axis=0)
        o_state = o_parts[0] if n_sub == 1 else jnp.concatenate(o_parts, axis=0)
        o = o_state + _dot(intra.astype(BF16), v_new.astype(BF16))
        o = o * lax.rsqrt(jnp.mean(o * o, axis=-1, keepdims=True) + RMS_EPS) * nw
        o_ref[r0:r0 + c, :] = o * _silu(gate_ref[r0:r0 + c, :])


def _gdn_call(x, x_col0, conv_w, gate, gate_col0, norm_w, beta, gc, gl, gcrow, s_in, layer, *, rows_per_seq, chunk,
              sub, per_chunk_state, n_pad):
    m = x.shape[0]
    r = min(512, rows_per_seq) if not per_chunk_state else min(512, m)
    chunk = min(chunk, r)
    n_seq = m // rows_per_seq if not per_chunk_state else m // r
    nt = rows_per_seq // r if not per_chunk_state else 1
    n_state_blk = r // sub if per_chunk_state else 1
    xb = x_col0 // DN_DK
    gb = gate_col0 // DN_DV
    hw = DN_WIDTH // DN_DK

    def rowblk(b, h, t):
        return b * nt + t

    xspec = lambda off: pl.BlockSpec((r, DN_DK), lambda b, h, t: (rowblk(b, h, t), xb + off + h))
    wspec = lambda off: pl.BlockSpec((DN_CONV, DN_DK), lambda b, h, t: (0, off + h))
    colspec = pl.BlockSpec((r, DN_HEADS), lambda b, h, t: (rowblk(b, h, t), 0))
    sspec = pl.BlockSpec((n_state_blk, None, DN_DK, DN_DV), lambda b, h, t: (b, h, 0, 0))
    return pl.pallas_call(
        functools.partial(_gdn_kernel, chunk=chunk, sub=sub, per_chunk_state=per_chunk_state, n_pad=n_pad),
        grid=(n_seq, DN_HEADS, nt),
        in_specs=[xspec(0), xspec(hw), xspec(2 * hw), wspec(0), wspec(hw), wspec(2 * hw),
                  pl.BlockSpec((r, DN_DV), lambda b, h, t: (rowblk(b, h, t), gb + h)),
                  pl.BlockSpec((1, DN_DV), lambda b, h, t: (0, 0)),
                  colspec, colspec, colspec,
                  pl.BlockSpec((None, 1, r), lambda b, h, t: (h, 0, rowblk(b, h, t))),
                  pl.BlockSpec((None, n_state_blk, None, DN_DK, DN_DV), lambda b, h, t: (layer, b, h, 0, 0))],
        out_specs=[pl.BlockSpec((r, DN_DV), lambda b, h, t: (rowblk(b, h, t), h)), sspec],
        out_shape=[jax.ShapeDtypeStruct((m, DN_WIDTH), F32), jax.ShapeDtypeStruct(s_in.shape[1:], F32)],
        scratch_shapes=[pltpu.VMEM((3, r + SUBLANES, DN_DK), F32), pltpu.VMEM((r, DN_DK), F32),
                        pltpu.VMEM((r, DN_DK), F32), pltpu.VMEM((r, DN_DV), F32)],
        compiler_params=_cparams(("parallel", "parallel", "arbitrary")),
        name="gdn_delta",
    )(x, x, x, conv_w, conv_w, conv_w, gate, norm_w.reshape(1, DN_DV), beta, gc, gl, gcrow, s_in)


def _pq_kernel(x_ref, w_ref, o_ref):
    for kv in range(NSA_KV_HEADS):
        acc = jnp.zeros((x_ref.shape[0], 2 * HEAD_DIM), F32)
        for j in range(CMP_STRIDE):
            c0 = j * KV_WIDTH + kv * HEAD_DIM
            acc = acc + _dot(x_ref[:, c0:c0 + HEAD_DIM].astype(BF16), w_ref[j])
        o_ref[:, kv * 2 * HEAD_DIM:(kv + 1) * 2 * HEAD_DIM] = acc


def _pq_weights(w1):
    return jnp.concatenate([w1[:CMP_STRIDE], w1[CMP_STRIDE:]], axis=-1).astype(BF16)


def _pq_call(chunks, wpq):
    nc, width = chunks.shape
    tc = 512 if nc % 512 == 0 else nc
    return pl.pallas_call(
        _pq_kernel,
        grid=(nc // tc,),
        in_specs=[pl.BlockSpec((tc, width), lambda i: (i, 0)),
                  pl.BlockSpec(wpq.shape, lambda i: (0, 0, 0))],
        out_specs=pl.BlockSpec((tc, 4 * HEAD_DIM), lambda i: (i, 0)),
        out_shape=jax.ShapeDtypeStruct((nc, 4 * HEAD_DIM), F32),
        compiler_params=_cparams(("parallel",)),
        name="nsa_compress_pq",
    )(chunks, wpq)


def _pq_pool_kernel(*refs):
    x_refs, w_ref, o_ref = refs[:CMP_STRIDE], refs[CMP_STRIDE], refs[CMP_STRIDE + 1]
    rows = o_ref.shape[1]
    acc = jnp.zeros((rows, 2 * HEAD_DIM), F32)
    for j in range(CMP_STRIDE):
        acc = acc + _dot(x_refs[j][...].reshape(rows, HEAD_DIM).astype(BF16), w_ref[j])
    o_ref[0] = acc[:, :HEAD_DIM]
    o_ref[1] = acc[:, HEAD_DIM:]


def _pq_pool_call(pool, layer, wpq):
    depth, n_phys, page, n_kv, hd = pool.shape
    nc = n_phys * (page // CMP_STRIDE)
    chunks = pool.reshape(depth, nc, CMP_STRIDE, n_kv, hd)
    tc = 512 if nc % 512 == 0 else nc
    xspec = lambda j: pl.BlockSpec((None, tc, None, n_kv, hd), lambda i: (layer, i, j, 0, 0))
    return pl.pallas_call(
        _pq_pool_kernel,
        grid=(nc // tc,),
        in_specs=[xspec(j) for j in range(CMP_STRIDE)] + [pl.BlockSpec(wpq.shape, lambda i: (0, 0, 0))],
        out_specs=pl.BlockSpec((2, tc * n_kv, HEAD_DIM), lambda i: (0, i, 0)),
        out_shape=jax.ShapeDtypeStruct((2, nc * n_kv, HEAD_DIM), F32),
        compiler_params=_cparams(("parallel",)),
        name="nsa_compress_pool",
    )(*([chunks] * CMP_STRIDE), wpq)


def _pe_term(pe, wpq):
    z = jnp.zeros((CMP_STRIDE, HEAD_DIM), F32)
    rows = jnp.stack([jnp.concatenate([pe[:CMP_STRIDE], z], axis=1).reshape(-1),
                      jnp.concatenate([pe[CMP_STRIDE:], z], axis=1).reshape(-1)])
    rows = jnp.concatenate([rows, jnp.zeros((SUBLANES - 2, rows.shape[1]), F32)], axis=0)
    pq = _pq_call(rows, wpq)
    return (pq[0, :HEAD_DIM] + pq[1, HEAD_DIM:2 * HEAD_DIM]).reshape(1, HEAD_DIM)


def _masked_softmax(s, valid):
    s = jnp.where(valid, s, NEG_INF)
    m = jnp.max(s, axis=-1, keepdims=True)
    e = jnp.exp(s - m)
    p = e / jnp.sum(e, axis=-1, keepdims=True)
    return jnp.where(valid, p, 0.0)


def _overlap_matrix(n_cmp, n_blk, transposed=False):
    shape, n_dim, j_dim = ((n_blk, n_cmp), 1, 0) if transposed else ((n_cmp, n_blk), 0, 1)
    n = lax.broadcasted_iota(jnp.int32, shape, n_dim) * CMP_STRIDE
    j = lax.broadcasted_iota(jnp.int32, shape, j_dim) * SEL_BLOCK
    ov = jnp.maximum(jnp.minimum(n + CMP_LEN, j + SEL_BLOCK) - jnp.maximum(n, j), 0)
    return ov.astype(F32) / CMP_LEN


AUG_HI = SEL_BLOCK
AUG_LO = SEL_BLOCK + 1


def _key_aug(pos, with_blocks):
    lane = lax.broadcasted_iota(jnp.int32, pos.shape, 1)
    hi = pos // SEL_BLOCK
    aug = jnp.where(lane == AUG_HI, hi, jnp.where(lane == AUG_LO, pos % SEL_BLOCK, 0))
    if with_blocks:
        aug = jnp.where(lane < SEL_BLOCK, jnp.where(hi == lane, 1, 0), aug)
    return aug.astype(F32).astype(BF16)


def _select_bias_t(imp_t, tpos_row):
    n_blk, t = imp_t.shape
    j = lax.broadcasted_iota(jnp.int32, (n_blk, t), 0)
    q_blk = tpos_row // SEL_BLOCK
    forced = (j == 0) | (j == q_blk) | (j == q_blk - 1)
    score = jnp.where(forced, FORCE_SCORE, jnp.where(j <= q_blk, imp_t, -1.0))
    rank = jnp.zeros((n_blk, t), F32)
    for jj in range(n_blk):
        sj = score[jj:jj + 1, :]
        ahead = (sj > score) | ((sj == score) & (j > jj))
        rank = rank + jnp.where(ahead, 1.0, 0.0)
    return jnp.where(rank < N_SEL, 0.0, NEG_INF)


def _select_bias(imp, tpos):
    t, n_blk = imp.shape
    j = lax.broadcasted_iota(jnp.int32, (t, n_blk), 1)
    q_blk = tpos // SEL_BLOCK
    forced = (j == 0) | (j == q_blk) | (j == q_blk - 1)
    score = jnp.where(forced, FORCE_SCORE, jnp.where(j <= q_blk, imp, -1.0))
    rank = jnp.zeros((t, n_blk), F32)
    for jj in range(n_blk):
        sj = score[:, jj:jj + 1]
        ahead = (sj > score) | ((sj == score) & (j > jj))
        rank = rank + jnp.where(ahead, 1.0, 0.0)
    return jnp.where(rank < N_SEL, 0.0, NEG_INF)


def _block_onehot(n_blk, k0, tk):
    j = lax.broadcasted_iota(jnp.int32, (n_blk, tk), 0)
    s = lax.broadcasted_iota(jnp.int32, (n_blk, tk), 1) + k0
    return jnp.where(s // SEL_BLOCK == j, 1.0, 0.0).astype(BF16)


def _compressed_tokens(hid, w2_ref):
    return _dot(_silu(hid).astype(BF16), w2_ref[...])


def _cmp_branch(q4b, cmpk, cmpv, tpos4, slope4, tq):
    n_cmp = cmpk.shape[0]
    scale = HEAD_DIM ** -0.5
    s = _dot_nt(q4b, cmpk.astype(BF16)) * scale
    c_end = lax.broadcasted_iota(jnp.int32, (1, n_cmp), 1) * CMP_STRIDE + (CMP_LEN - 1)
    dist = tpos4 - c_end
    valid = dist >= 0
    s = s - slope4 * dist.astype(F32)
    p = _masked_softmax(s, valid)
    o = _dot(p.astype(BF16), cmpv.astype(BF16))
    psum = p[0:tq]
    for g in range(1, NSA_GROUP):
        psum = psum + p[g * tq:(g + 1) * tq]
    return o, psum


def _slopes(kv, tq):
    g = lax.broadcasted_iota(jnp.int32, (NSA_GROUP * tq, 1), 0) // tq
    head = (g + kv * NSA_GROUP + 1).astype(F32)
    return jnp.exp(head * (-(8.0 / NSA_HEADS) * math.log(2.0)))


def _tail_col(tail, idx):
    lane = lax.broadcasted_iota(jnp.int32, tail.shape, 1)
    return jnp.sum(jnp.where(lane == idx, tail, 0.0), axis=1, keepdims=True)


def _combine_branches(tail, kv, o_cmp, o_slc, o_win, tq):
    outs = []
    for g in range(NSA_GROUP):
        head = kv * NSA_GROUP + g
        acc = jnp.zeros((tq, HEAD_DIM), F32)
        for br, o in enumerate((o_cmp, o_slc, o_win)):
            gate = _sigmoid(_tail_col(tail, TAIL_NSAG + br * NSA_HEADS + head))
            acc = acc + gate * o[g * tq:(g + 1) * tq]
        outs.append(acc)
    return outs


def _nsa_prompt_kernel(q_ref, pqk_ref, pqv_ref, pek_ref, pev_ref, w2k_ref, w2v_ref, sk_ref, sv_ref, wk_ref, wv_ref,
                       tail_ref, o_ref, ccat_scr, cmpv_scr, kcat_scr, sv_scr, wcat_scr, wv_scr, *, tk):
    tq = q_ref.shape[0]
    t_len = sk_ref.shape[0]
    n_cmp = pqk_ref.shape[0]
    n_blk = t_len // SEL_BLOCK
    kv = pl.program_id(1)
    i = pl.program_id(2)
    t0 = i * tq
    scale = HEAD_DIM ** -0.5

    @pl.when(i == 0)
    def _():
        kaug = _key_aug(lax.broadcasted_iota(jnp.int32, (t_len, LANES), 0), True)
        for cat_scr, k_ref, vb_scr, v_ref in ((kcat_scr, sk_ref, sv_scr, sv_ref), (wcat_scr, wk_ref, wv_scr, wv_ref)):
            cat_scr[:, 0:HEAD_DIM] = k_ref[...].astype(BF16)
            cat_scr[:, HEAD_DIM:] = kaug
            vb_scr[...] = v_ref[...].astype(BF16)
        tokens = []
        for pq_ref, pe_ref, w2_ref in ((pqk_ref, pek_ref, w2k_ref), (pqv_ref, pev_ref, w2v_ref)):
            p = pq_ref[:, 0:HEAD_DIM]
            qn = pltpu.roll(pq_ref[:, HEAD_DIM:2 * HEAD_DIM], n_cmp - 1, 0)
            tokens.append(_compressed_tokens(p + qn + pe_ref[...], w2_ref).astype(BF16))
        c_end = lax.broadcasted_iota(jnp.int32, (n_cmp, LANES), 0) * CMP_STRIDE + (CMP_LEN - 1)
        ccat_scr[:, 0:HEAD_DIM] = tokens[0]
        ccat_scr[:, HEAD_DIM:] = _key_aug(c_end, False)
        cmpv_scr[...] = tokens[1]

    qb = q_ref[...] * scale
    q4b = jnp.concatenate([qb[:, g * HEAD_DIM:(g + 1) * HEAD_DIM] for g in range(NSA_GROUP)], axis=0).astype(BF16)
    rows4 = NSA_GROUP * tq
    tpos4 = t0 + lax.broadcasted_iota(jnp.int32, (rows4, 1), 0) % tq
    slope4 = _slopes(kv, tq)
    lane = lax.broadcasted_iota(jnp.int32, (rows4, LANES), 1)
    alibi = jnp.where(lane == AUG_HI, slope4 * SEL_BLOCK, jnp.where(lane == AUG_LO, slope4, 0.0))
    q_plain = jnp.concatenate([q4b, alibi.astype(BF16)], axis=1)

    c_end = lax.broadcasted_iota(jnp.int32, (1, n_cmp), 1) * CMP_STRIDE + (CMP_LEN - 1)
    p = _masked_softmax(_dot_nt(q_plain, ccat_scr[...]), c_end <= tpos4)
    o_cmp = _dot(p.astype(BF16), cmpv_scr[...])
    psum = p[0:tq]
    for g in range(1, NSA_GROUP):
        psum = psum + p[g * tq:(g + 1) * tq]

    imp_t = _dot_nt(_overlap_matrix(n_cmp, n_blk, transposed=True), psum, HIGHEST)
    tpos_row = t0 + lax.broadcasted_iota(jnp.int32, (1, tq), 1)
    selb = _select_bias_t(imp_t, tpos_row).T
    selb = jnp.concatenate([selb, jnp.zeros((tq, LANES - n_blk), F32)], axis=1)
    selb4 = jnp.concatenate([selb] * NSA_GROUP, axis=0)
    q_sel = jnp.concatenate([q4b, jnp.where(lane < SEL_BLOCK, selb4, alibi).astype(BF16)], axis=1)

    def slc_tile(jt, carry, diagonal):
        m, l, acc = carry
        k0 = pl.multiple_of(jt * tk, tk)
        s = _dot_nt(q_sel, kcat_scr[pl.ds(k0, tk), :])
        if diagonal:
            s = jnp.where(k0 + lax.broadcasted_iota(jnp.int32, (1, tk), 1) <= tpos4, s, NEG_INF)
        m_new = jnp.maximum(m, jnp.max(s, axis=-1, keepdims=True))
        alpha = jnp.exp(m - m_new)
        e = jnp.exp(s - m_new)
        l = alpha * l + jnp.sum(e, axis=-1, keepdims=True)
        acc = alpha * acc + _dot(e.astype(BF16), sv_scr[pl.ds(k0, tk), :])
        return m_new, l, acc

    n_full = t0 // tk
    init = (jnp.full((rows4, 1), NEG_INF, F32), jnp.zeros((rows4, 1), F32), jnp.zeros((rows4, HEAD_DIM), F32))
    carry = lax.fori_loop(0, n_full, lambda jt, c: slc_tile(jt, c, False), init)
    _, l, acc = slc_tile(n_full, carry, True)
    o_slc = acc / l

    band = min(WINDOW + tq, t_len)
    b0 = pl.multiple_of(jnp.maximum(t0 + tq - band, 0), BF16_ROWS)
    dist = tpos4 - (b0 + lax.broadcasted_iota(jnp.int32, (1, band), 1))
    p = _masked_softmax(_dot_nt(q_plain, wcat_scr[pl.ds(b0, band), :]), (dist >= 0) & (dist < WINDOW))
    o_win = _dot(p.astype(BF16), wv_scr[pl.ds(b0, band), :])

    outs = _combine_branches(tail_ref[...], kv, o_cmp, o_slc, o_win, tq)
    for g in range(NSA_GROUP):
        o_ref[:, g * HEAD_DIM:(g + 1) * HEAD_DIM] = outs[g]


def _nsa_prompt_call(z, pqk, pqv, pek, pev, w2k, w2v, n_batch, t_len):
    tq = min(128, t_len)
    tk = min(512, t_len)
    nq = t_len // tq
    n_cmp = t_len // CMP_STRIDE
    qblk0 = OFF_NQ // (NSA_GROUP * HEAD_DIM)
    assert t_len <= SEL_BLOCK * SEL_BLOCK and 8 % NSA_HEADS == 0 and tk % tq == 0
    rowspec = lambda off: pl.BlockSpec((t_len, HEAD_DIM), lambda b, kv, i: (b, off // HEAD_DIM + kv))
    pqspec = pl.BlockSpec((n_cmp, 2 * HEAD_DIM), lambda b, kv, i: (b, kv))
    cspec = lambda shape: pl.BlockSpec(shape, lambda b, kv, i: (0, 0))
    return pl.pallas_call(
        functools.partial(_nsa_prompt_kernel, tk=tk),
        grid=(n_batch, NSA_KV_HEADS, nq),
        in_specs=[pl.BlockSpec((tq, NSA_GROUP * HEAD_DIM), lambda b, kv, i: (b * nq + i, qblk0 + kv)),
                  pqspec, pqspec, cspec((1, HEAD_DIM)), cspec((1, HEAD_DIM)),
                  cspec((HEAD_DIM, HEAD_DIM)), cspec((HEAD_DIM, HEAD_DIM)),
                  rowspec(OFF_ROWS + 2 * KV_WIDTH), rowspec(OFF_ROWS + 3 * KV_WIDTH),
                  rowspec(OFF_ROWS + 4 * KV_WIDTH), rowspec(OFF_ROWS + 5 * KV_WIDTH),
                  pl.BlockSpec((tq, LANES), lambda b, kv, i: (b * nq + i, OFF_TAIL // LANES))],
        out_specs=pl.BlockSpec((tq, NSA_GROUP * HEAD_DIM), lambda b, kv, i: (b * nq + i, kv)),
        out_shape=jax.ShapeDtypeStruct((n_batch * t_len, NSA_WIDTH), F32),
        scratch_shapes=[pltpu.VMEM((n_cmp, 2 * HEAD_DIM), BF16), pltpu.VMEM((n_cmp, HEAD_DIM), BF16),
                        pltpu.VMEM((t_len, 2 * HEAD_DIM), BF16), pltpu.VMEM((t_len, HEAD_DIM), BF16),
                        pltpu.VMEM((t_len, 2 * HEAD_DIM), BF16), pltpu.VMEM((t_len, HEAD_DIM), BF16)],
        compiler_params=_cparams(("parallel", "parallel", "arbitrary")),
        name="nsa_prompt",
    )(z, pqk, pqv, pek, pev, w2k, w2v, z, z, z, z, z)


def _nsa_sample_kernel(pt_ref, q_ref, new_ref, pqnk_ref, pqnv_ref, pek_ref, pev_ref, w2k_ref, w2v_ref, wink_ref,
                       winv_ref, tail_ref, pqk_hbm, pqv_hbm, sk_hbm, sv_hbm, o_ref,
                       pqk_buf, pqv_buf, sk_buf, sv_buf, sem, *, n_pages, past_len, layer):
    b = pl.program_id(0)
    nb = pl.num_programs(0)
    t_new = q_ref.shape[0]
    page = PAGE_ROWS
    cpp = page // CMP_STRIDE
    new_rows = LANES
    n_keys = past_len + new_rows
    n_cmp = pqk_buf.shape[2]
    n_blk = SEL_BLOCK
    scale = HEAD_DIM ** -0.5
    ppr = cpp * NSA_KV_HEADS
    kpr = page * NSA_KV_HEADS

    def copies(seq, slot):
        out = []
        for p in range(n_pages):
            pg = pt_ref[seq, p]
            out.append(pltpu.make_async_copy(pqk_hbm.at[:, pl.ds(pg * ppr, ppr)], pqk_buf.at[slot, :, pl.ds(p * ppr, ppr)], sem.at[slot, 0]))
            out.append(pltpu.make_async_copy(pqv_hbm.at[:, pl.ds(pg * ppr, ppr)], pqv_buf.at[slot, :, pl.ds(p * ppr, ppr)], sem.at[slot, 1]))
            out.append(pltpu.make_async_copy(sk_hbm.at[layer, pg], sk_buf.at[slot, pl.ds(p * kpr, kpr)], sem.at[slot, 2]))
            out.append(pltpu.make_async_copy(sv_hbm.at[layer, pg], sv_buf.at[slot, pl.ds(p * kpr, kpr)], sem.at[slot, 3]))
        return out

    slot = b % 2

    @pl.when(b == 0)
    def _():
        for cp in copies(0, 0):
            cp.start()

    @pl.when(b + 1 < nb)
    def _():
        for cp in copies(b + 1, 1 - slot):
            cp.start()

    new = new_ref[...]
    qb = q_ref[...]
    tail = tail_ref[...]
    tq = t_new
    rows4 = NSA_GROUP * tq
    tpos4 = past_len + lax.broadcasted_iota(jnp.int32, (rows4, 1), 0) % tq
    tpos = past_len + lax.broadcasted_iota(jnp.int32, (tq, 1), 0)

    win_len = wink_ref.shape[0] // NSA_KV_HEADS
    band = win_len + new_rows

    def new_tile(which, kv):
        c0 = which * KV_WIDTH + kv * HEAD_DIM
        return jnp.concatenate([new[:, c0:c0 + HEAD_DIM], jnp.zeros((new_rows - t_new, HEAD_DIM), F32)],
                               axis=0).astype(BF16)

    for cp in copies(b, slot):
        cp.wait()

    n_pool = n_pages * cpp
    rown = lax.broadcasted_iota(jnp.int32, (n_pool, 1), 0)
    onehot = _block_onehot(n_blk, 0, n_keys)
    for kv in range(NSA_KV_HEADS):
        q4b = jnp.concatenate([qb[:, (kv * NSA_GROUP + g) * HEAD_DIM:(kv * NSA_GROUP + g + 1) * HEAD_DIM]
                               for g in range(NSA_GROUP)], axis=0).astype(BF16)
        slope4 = _slopes(kv, tq)
        c0 = kv * 2 * HEAD_DIM
        cmp_tokens = []
        for pq_buf, pqn_ref, pe_ref, w2_ref in ((pqk_buf, pqnk_ref, pek_ref, w2k_ref), (pqv_buf, pqnv_ref, pev_ref, w2v_ref)):
            kv_rows = pl.ds(kv, n_pool, stride=NSA_KV_HEADS)
            p = pq_buf[slot, 0, kv_rows, :]
            qn = pltpu.roll(pq_buf[slot, 1, kv_rows, :], n_pool - 1, 0)
            pn = pqn_ref[:, c0:c0 + HEAD_DIM]
            qnn = pqn_ref[:, c0 + HEAD_DIM:c0 + 2 * HEAD_DIM]
            qn = jnp.where(rown == n_pool - 1, qnn, qn)
            hid_pool = p + qn
            rowt = lax.broadcasted_iota(jnp.int32, (n_cmp - n_pool, 1), 0)
            hid_tail = jnp.where(rowt == 0, pn, 0.0)
            hid = jnp.concatenate([hid_pool, hid_tail], axis=0) + pe_ref[...]
            cmp_tokens.append(_compressed_tokens(hid, w2_ref))
        o_cmp, psum = _cmp_branch(q4b, cmp_tokens[0], cmp_tokens[1], tpos4, slope4, tq)
        imp = _dot(psum, _overlap_matrix(n_cmp, n_blk), HIGHEST)
        selb = _select_bias(imp, tpos).astype(BF16)
        selb4 = jnp.concatenate([selb] * NSA_GROUP, axis=0)

        ks = sk_buf[slot, pl.ds(kv, past_len, stride=NSA_KV_HEADS), :].astype(BF16)
        vs = sv_buf[slot, pl.ds(kv, past_len, stride=NSA_KV_HEADS), :].astype(BF16)
        s = jnp.concatenate([_dot_nt(q4b, ks), _dot_nt(q4b, new_tile(2, kv))], axis=1) * scale + _dot(selb4, onehot)
        dist = tpos4 - lax.broadcasted_iota(jnp.int32, (1, n_keys), 1)
        p = _masked_softmax(s - slope4 * dist.astype(F32), dist >= 0).astype(BF16)
        o_slc = _dot(p[:, :past_len], vs) + _dot(p[:, past_len:], new_tile(3, kv))

        kw = wink_ref[pl.ds(kv, win_len, stride=NSA_KV_HEADS), :].astype(BF16)
        vw = winv_ref[pl.ds(kv, win_len, stride=NSA_KV_HEADS), :].astype(BF16)
        s = jnp.concatenate([_dot_nt(q4b, kw), _dot_nt(q4b, new_tile(4, kv))], axis=1) * scale
        k_pos = (past_len - win_len) + lax.broadcasted_iota(jnp.int32, (1, band), 1)
        dist = tpos4 - k_pos
        p = _masked_softmax(s - slope4 * dist.astype(F32), (dist >= 0) & (dist < WINDOW)).astype(BF16)
        o_win = _dot(p[:, :win_len], vw) + _dot(p[:, win_len:], new_tile(5, kv))

        outs = _combine_branches(tail, kv, o_cmp, o_slc, o_win, tq)
        for g in range(NSA_GROUP):
            h0 = (kv * NSA_GROUP + g) * HEAD_DIM
            o_ref[:, h0:h0 + HEAD_DIM] = outs[g]


PAGE_ROWS = 128


def _kv_rows(x):
    return x.reshape(x.shape[:-3] + (x.shape[-3] * x.shape[-2], x.shape[-1]))


def _nsa_sample_call(page_table, z3, pq_new_k, pq_new_v, pek, pev, w2k, w2v, win_k, win_v, pq_pool_k, pq_pool_v,
                     pool_sk, pool_sv, layer):
    bd, t_new, _ = z3.shape
    n_pages = page_table.shape[1]
    past_len = n_pages * PAGE_ROWS
    win_len = win_k.shape[2]
    n_cmp = 2 * n_pages * (PAGE_ROWS // CMP_STRIDE)
    assert (past_len + LANES) // SEL_BLOCK <= SEL_BLOCK and past_len // CMP_STRIDE + 8 <= n_cmp and t_new <= LANES

    def zspec(width, col0):
        return pl.BlockSpec((None, t_new, width), lambda b, pt: (b, 0, col0 // width))

    cspec = lambda shape: pl.BlockSpec(shape, lambda b, pt: (0,) * len(shape))
    pqnspec = pl.BlockSpec((None, 1, 4 * HEAD_DIM), lambda b, pt: (b, 0, 0))
    winspec = pl.BlockSpec((None, None, win_len * NSA_KV_HEADS, HEAD_DIM), lambda b, pt: (layer, b, 0, 0))
    anyspec = pl.BlockSpec(memory_space=pl.ANY)
    grid_spec = pltpu.PrefetchScalarGridSpec(
        num_scalar_prefetch=1,
        grid=(bd,),
        in_specs=[zspec(NSA_WIDTH, OFF_NQ), zspec(6 * KV_WIDTH, OFF_ROWS), pqnspec, pqnspec,
                  cspec((1, HEAD_DIM)), cspec((1, HEAD_DIM)), cspec((HEAD_DIM, HEAD_DIM)), cspec((HEAD_DIM, HEAD_DIM)),
                  winspec, winspec, zspec(LANES, OFF_TAIL), anyspec, anyspec, anyspec, anyspec],
        out_specs=pl.BlockSpec((None, t_new, NSA_WIDTH), lambda b, pt: (b, 0, 0)),
        scratch_shapes=[pltpu.VMEM((2, 2, n_cmp, HEAD_DIM), F32),
                        pltpu.VMEM((2, 2, n_cmp, HEAD_DIM), F32),
                        pltpu.VMEM((2, past_len * NSA_KV_HEADS, HEAD_DIM), F32),
                        pltpu.VMEM((2, past_len * NSA_KV_HEADS, HEAD_DIM), F32),
                        pltpu.SemaphoreType.DMA((2, 4))],
    )
    return pl.pallas_call(
        functools.partial(_nsa_sample_kernel, n_pages=n_pages, past_len=past_len, layer=layer),
        grid_spec=grid_spec,
        out_shape=jax.ShapeDtypeStruct((bd, t_new, NSA_WIDTH), F32),
        compiler_params=_cparams(("arbitrary",)),
        name="nsa_sample",
    )(page_table, z3, z3, pq_new_k, pq_new_v, pek, pev, w2k, w2v, _kv_rows(win_k), _kv_rows(win_v), z3,
      pq_pool_k, pq_pool_v, _kv_rows(pool_sk), _kv_rows(pool_sv))


def _merge_kernel(ya_ref, yb_ref, ga_ref, gb_ref, wa_ref, wb_ref, wo_ref, x_ref, gate_ref, g_ref, b_ref, o_ref,
                  *, alpha):
    a = _dot(ya_ref[...].astype(BF16), wa_ref[...])
    b = _dot(yb_ref[...].astype(BF16), wb_ref[...])
    mix = _sigmoid(ga_ref[...]) * a + _sigmoid(gb_ref[...]) * b
    y = _dot(mix.astype(BF16), wo_ref[...])
    o_ref[...] = _ln(alpha * x_ref[...] + gate_ref[...] * y) * g_ref[...] + b_ref[...]


def _merge_call(ya, yb, z, wa, wb, wo, layer, x, gate, ln_g, ln_b, rows_per_seq, alpha):
    m, d = x.shape
    tm = min(256, m)
    rspec = lambda w: pl.BlockSpec((tm, w), lambda i: (i, 0))
    cspec = lambda a: pl.BlockSpec((None,) + a.shape[1:], lambda i: (layer, 0, 0))
    return pl.pallas_call(
        functools.partial(_merge_kernel, alpha=alpha),
        grid=(m // tm,),
        in_specs=[rspec(DN_WIDTH), rspec(NSA_WIDTH),
                  pl.BlockSpec((tm, d), lambda i: (i, OFF_MERGE // d)),
                  pl.BlockSpec((tm, d), lambda i: (i, OFF_MERGE // d + 1)),
                  cspec(wa), cspec(wb), cspec(wo), rspec(d), _rowparam_spec(gate, tm, rows_per_seq),
                  pl.BlockSpec((1, d), lambda i: (0, 0)), pl.BlockSpec((1, d), lambda i: (0, 0))],
        out_specs=rspec(d),
        out_shape=jax.ShapeDtypeStruct((m, d), F32),
        compiler_params=_cparams(("parallel",)),
        name="merge_out_ln",
    )(ya, yb, z, z, wa, wb, wo, x, gate, ln_g.reshape(1, d), ln_b.reshape(1, d))


def _ffn_down_kernel(ua_ref, ub_ref, pa_ref, pb_ref, cwa_ref, cwb_ref, wd_ref, x_ref, gate_ref, g_ref, b_ref, o_ref,
                     acc_scr, *, alpha, rows_per_seq):
    i = pl.program_id(0)
    k = pl.program_id(1)
    tm = ua_ref.shape[0]
    seq_start = (i * tm) % rows_per_seq == 0
    row8 = lax.broadcasted_iota(jnp.int32, (SUBLANES, 1), 0)

    def conv(u_ref, p_ref, w_ref):
        u = u_ref[...]
        w = w_ref[...]
        prev = jnp.where(seq_start, 0.0, p_ref[...])
        y = u * w[FFN_CONV - 1:FFN_CONV, :]
        for sh in range(1, FFN_CONV):
            shifted = pltpu.roll(u, sh, 0)
            head = jnp.where(row8 < sh, pltpu.roll(prev, sh, 0), shifted[:SUBLANES])
            shifted = jnp.concatenate([head, shifted[SUBLANES:]], axis=0)
            y = y + shifted * w[FFN_CONV - 1 - sh:FFN_CONV - sh, :]
        return y

    act = _silu(conv(ua_ref, pa_ref, cwa_ref)) * conv(ub_ref, pb_ref, cwb_ref)
    part = _dot(act.astype(BF16), wd_ref[...])

    @pl.when(k == 0)
    def _():
        acc_scr[...] = part

    @pl.when(k > 0)
    def _():
        acc_scr[...] += part

    @pl.when(k == pl.num_programs(1) - 1)
    def _():
        o_ref[...] = _ln(alpha * x_ref[...] + gate_ref[...] * acc_scr[...]) * g_ref[...] + b_ref[...]


def _ffn_down_call(u, conv_w, wd, layer, x, gate, ln_g, ln_b, rows_per_seq, alpha):
    m, d = x.shape
    d_ff = wd.shape[1]
    tm = min(512, m)
    tk = 512
    nk = d_ff // tk
    sub_per_tile = tm // SUBLANES
    prev_idx = lambda i: jnp.maximum(i * sub_per_tile - 1, 0)
    return pl.pallas_call(
        functools.partial(_ffn_down_kernel, alpha=alpha, rows_per_seq=rows_per_seq),
        grid=(m // tm, nk),
        in_specs=[pl.BlockSpec((tm, tk), lambda i, k: (i, k)),
                  pl.BlockSpec((tm, tk), lambda i, k: (i, nk + k)),
                  pl.BlockSpec((SUBLANES, tk), lambda i, k: (prev_idx(i), k)),
                  pl.BlockSpec((SUBLANES, tk), lambda i, k: (prev_idx(i), nk + k)),
                  pl.BlockSpec((FFN_CONV, tk), lambda i, k: (0, k)),
                  pl.BlockSpec((FFN_CONV, tk), lambda i, k: (0, nk + k)),
                  pl.BlockSpec((None, tk, d), lambda i, k: (layer, k, 0)),
                  pl.BlockSpec((tm, d), lambda i, k: (i, 0)),
                  _rowparam_spec(gate, tm, rows_per_seq),
                  pl.BlockSpec((1, d), lambda i, k: (0, 0)), pl.BlockSpec((1, d), lambda i, k: (0, 0))],
        out_specs=pl.BlockSpec((tm, d), lambda i, k: (i, 0)),
        out_shape=jax.ShapeDtypeStruct((m, d), F32),
        scratch_shapes=[pltpu.VMEM((tm, d), F32)],
        compiler_params=_cparams(("parallel", "arbitrary")),
        name="ffn_down_ln",
    )(u, u, u, u, conv_w, conv_w, wd, x, gate, ln_g.reshape(1, d), ln_b.reshape(1, d))


def _permute_w_in(w_in):
    d = w_in.shape[-2]
    sizes = (3 * DN_WIDTH, DN_HEADS, DN_HEADS, DN_WIDTH, NSA_WIDTH) + (KV_WIDTH,) * 6 + (3 * NSA_HEADS, 2 * d)
    offs = [0]
    for s in sizes:
        offs.append(offs[-1] + s)
    part = lambda i: w_in[..., offs[i]:offs[i + 1]].astype(BF16)
    cols = [part(0), part(3), part(12), part(4)] + [part(i) for i in range(5, 11)] + [part(1), part(2), part(11)]
    used = sum(c.shape[-1] for c in cols)
    cols.append(jnp.zeros(w_in.shape[:-1] + (IN_PAD_WIDTH - used,), BF16))
    return jnp.concatenate(cols, axis=-1)


def _pad_groups(x, n_pad):
    b, t, c = x.shape
    return jnp.concatenate([jnp.zeros((b, n_pad, c), x.dtype), x], axis=1).reshape(b * (n_pad + t), c)


def kernel(x_prompt, x_sample, cache_cmp_k, cache_cmp_v, cache_slc_k, cache_slc_v, cache_win_k, cache_win_v, state_dn, state_dn_conv, state_ffn_conv, page_table, c_prompt, c_sample, w_ada, b_ada, w_in, dn_conv_w, dn_a_log, dn_dt_bias, dn_norm_w, cmp_k_w1, cmp_k_pe, cmp_k_w2, cmp_v_w1, cmp_v_pe, cmp_v_w2, w_branch_a, w_branch_b, w_out, ln1_g, ln1_b, ffn_w_up, ffn_conv_w, ffn_w_down, ln2_g, ln2_b):
    depth = w_in.shape[0]
    bp, t_len, d = x_prompt.shape
    bd, t_new, _ = x_sample.shape
    n_phys = cache_cmp_k.shape[1]
    win_len = cache_win_k.shape[2]
    d_ff = ffn_w_down.shape[1]
    alpha = (2 * depth) ** 0.25
    grp = SUBLANES
    assert t_new <= grp - (DN_CONV - 1) and t_new >= DN_CONV - 1 and cache_cmp_k.shape[2] == PAGE_ROWS

    xp = x_prompt.reshape(bp * t_len, d)
    xs = x_sample.reshape(bd * t_new, d)
    c_all = jnp.concatenate([c_prompt, c_sample], axis=0)
    c_rows = -(-c_all.shape[0] // SUBLANES) * SUBLANES
    c_all = jnp.pad(c_all, ((0, c_rows - c_all.shape[0]), (0, 0)))

    outs_p = {k: [] for k in ('cmp_k', 'cmp_v', 'slc_k', 'slc_v', 'win_k', 'win_v', 'dn_state', 'dn_conv', 'ffn_conv')}
    outs_s = {k: [] for k in outs_p}

    w_ada_b = w_ada.astype(BF16)
    w_in_b = _permute_w_in(w_in)
    wa, wb, wo = w_branch_a.astype(BF16), w_branch_b.astype(BF16), w_out.astype(BF16)
    w_up, w_dn = ffn_w_up.astype(BF16), ffn_w_down.astype(BF16)
    zero_state = jnp.zeros((1, bp, DN_HEADS, DN_DK, DN_DV), F32)

    for l in range(depth):
        mod = _ada_call(c_all, w_ada_b, l, b_ada[l])
        mod_p = mod[:bp].reshape(bp, 6, 1, d)
        mod_s = mod[bp:bp + bd].reshape(bd, 6, d)
        mp = [mod_p[:, i] for i in range(6)]
        ms = [jnp.repeat(mod_s[:, i], t_new, axis=0) for i in range(6)]

        wpq_k = _pq_weights(cmp_k_w1[l])
        wpq_v = _pq_weights(cmp_v_w1[l])
        pek = _pe_term(cmp_k_pe[l], wpq_k)
        pev = _pe_term(cmp_v_pe[l], wpq_v)
        w2k = cmp_k_w2[l].astype(BF16)
        w2v = cmp_v_w2[l].astype(BF16)

        zp = _ln_mm_call(xp, mp[0], mp[1], w_in_b, l, t_len, "in_proj")
        rows_p = zp[:, OFF_ROWS:OFF_ROWS + 6 * KV_WIDTH].reshape(bp, t_len, 6, NSA_KV_HEADS, HEAD_DIM)
        beta, gc, gl = _gdn_gates_call(zp, OFF_TAIL, dn_a_log[l], dn_dt_bias[l], DN_CHUNK, 0)
        gcrow = gc.T.reshape(DN_HEADS, 1, bp * t_len)
        ya_p, dn_state_p = _gdn_call(zp, OFF_QKV, dn_conv_w[l], zp, OFF_DNGATE, dn_norm_w[l], beta, gc, gl, gcrow,
                                     zero_state, 0, rows_per_seq=t_len, chunk=GDN_MATMUL_ROWS,
                                     sub=math.gcd(DN_CHUNK, t_len), per_chunk_state=False, n_pad=0)
        chunk_w = CMP_STRIDE * KV_WIDTH
        pqk = _pq_call(zp[:, OFF_ROWS:OFF_ROWS + KV_WIDTH].reshape(-1, chunk_w), wpq_k)
        pqv = _pq_call(zp[:, OFF_ROWS + KV_WIDTH:OFF_ROWS + 2 * KV_WIDTH].reshape(-1, chunk_w), wpq_v)
        yb_p = _nsa_prompt_call(zp, pqk, pqv, pek, pev, w2k, w2v, bp, t_len)
        x1p = _merge_call(ya_p, yb_p, zp, wa, wb, wo, l, xp, mp[2], ln1_g[l], ln1_b[l], t_len, alpha)
        up = _ln_mm_call(x1p, mp[3], mp[4], w_up, l, t_len, "ffn_up")
        xp = _ffn_down_call(up, ffn_conv_w[l], w_dn, l, x1p, mp[5], ln2_g[l], ln2_b[l], t_len, alpha)

        keep = min(WINDOW, t_len)
        for i, name in enumerate(('cmp_k', 'cmp_v', 'slc_k', 'slc_v')):
            outs_p[name].append(rows_p[:, :, i])
        outs_p['win_k'].append(rows_p[:, t_len - keep:, 4])
        outs_p['win_v'].append(rows_p[:, t_len - keep:, 5])
        outs_p['dn_state'].append(dn_state_p)
        outs_p['dn_conv'].append(zp.reshape(bp, t_len, -1)[:, t_len - (DN_CONV - 1):, OFF_QKV:OFF_QKV + 3 * DN_WIDTH])
        outs_p['ffn_conv'].append(up.reshape(bp, t_len, -1)[:, t_len - (FFN_CONV - 1):, :2 * d_ff])

        zs = _ln_mm_call(xs, ms[0], ms[1], w_in_b, l, t_new, "in_proj")
        zs3 = zs.reshape(bd, t_new, -1)
        rows_s = zs3[:, :, OFF_ROWS:OFF_ROWS + 6 * KV_WIDTH].reshape(bd, t_new, 6, NSA_KV_HEADS, HEAD_DIM)
        n_pad = grp - t_new
        qkv_g = jnp.concatenate([jnp.zeros((bd, n_pad - (DN_CONV - 1), 3 * DN_WIDTH), F32), state_dn_conv[l],
                                 zs3[:, :, OFF_QKV:OFF_QKV + 3 * DN_WIDTH]], axis=1).reshape(bd * grp, -1)
        zs_g = _pad_groups(jnp.concatenate([zs3[:, :, OFF_DNGATE:OFF_DNGATE + DN_WIDTH],
                                            zs3[:, :, OFF_TAIL:OFF_TAIL + LANES]], axis=-1), n_pad)
        beta, gc, gl = _gdn_gates_call(zs_g, DN_WIDTH, dn_a_log[l], dn_dt_bias[l], grp, n_pad)
        gcrow = gc.T.reshape(DN_HEADS, 1, bd * grp)
        ya_g, dn_state_s = _gdn_call(qkv_g, 0, dn_conv_w[l], zs_g, 0, dn_norm_w[l], beta, gc, gl, gcrow, state_dn, l,
                                     rows_per_seq=grp, chunk=GDN_MATMUL_ROWS, sub=grp, per_chunk_state=True, n_pad=n_pad)
        ya_s = ya_g.reshape(bd, grp, DN_WIDTH)[:, n_pad:].reshape(bd * t_new, DN_WIDTH)
        pq_pool_k = _pq_pool_call(cache_cmp_k, l, wpq_k)
        pq_pool_v = _pq_pool_call(cache_cmp_v, l, wpq_v)
        new_chunk = lambda i: jnp.pad(zs3[:, :, OFF_ROWS + i * KV_WIDTH:OFF_ROWS + (i + 1) * KV_WIDTH],
                                      ((0, 0), (0, CMP_STRIDE - t_new), (0, 0))).reshape(bd, chunk_w)
        pq_new_k = _pq_call(new_chunk(0), wpq_k).reshape(bd, 1, -1)
        pq_new_v = _pq_call(new_chunk(1), wpq_v).reshape(bd, 1, -1)
        yb_s = _nsa_sample_call(page_table, zs3, pq_new_k, pq_new_v, pek, pev, w2k, w2v, cache_win_k, cache_win_v,
                                pq_pool_k, pq_pool_v, cache_slc_k, cache_slc_v, l).reshape(bd * t_new, NSA_WIDTH)
        x1s = _merge_call(ya_s, yb_s, zs, wa, wb, wo, l, xs, ms[2], ln1_g[l], ln1_b[l], t_new, alpha)
        us = _ln_mm_call(x1s, ms[3], ms[4], w_up, l, t_new, "ffn_up")
        us3 = us.reshape(bd, t_new, -1)
        u_g = jnp.concatenate([jnp.zeros((bd, n_pad - (FFN_CONV - 1), 2 * d_ff), F32), state_ffn_conv[l], us3],
                              axis=1).reshape(bd * grp, -1)
        x1_g = _pad_groups(x1s.reshape(bd, t_new, d), n_pad)
        gate2_g = jnp.repeat(mod_s[:, 5], grp, axis=0)
        xs_g = _ffn_down_call(u_g, ffn_conv_w[l], w_dn, l, x1_g, gate2_g, ln2_g[l], ln2_b[l], grp, alpha)
        xs = xs_g.reshape(bd, grp, d)[:, n_pad:].reshape(bd * t_new, d)

        for i, name in enumerate(('cmp_k', 'cmp_v', 'slc_k', 'slc_v')):
            outs_s[name].append(rows_s[:, :, i])
        outs_s['win_k'].append(rows_s[:, :, 4])
        outs_s['win_v'].append(rows_s[:, :, 5])
        outs_s['dn_state'].append(dn_state_s)
        outs_s['dn_conv'].append(jnp.concatenate([state_dn_conv[l], zs3[:, :, OFF_QKV:OFF_QKV + 3 * DN_WIDTH]], axis=1)[:, t_new:])
        outs_s['ffn_conv'].append(jnp.concatenate([state_ffn_conv[l], us3], axis=1)[:, t_new:])

    order = ('cmp_k', 'cmp_v', 'slc_k', 'slc_v', 'win_k', 'win_v', 'dn_state', 'dn_conv', 'ffn_conv')
    stacked_s = {k: jnp.stack(outs_s[k]) for k in order}
    for name, cache in (('win_k', cache_win_k), ('win_v', cache_win_v)):
        stacked_s[name] = jnp.concatenate([cache, stacked_s[name]], axis=2)[:, :, t_new:]
    return ((xp.reshape(bp, t_len, d), xs.reshape(bd, t_new, d))
            + tuple(jnp.stack(outs_p[k]) for k in order) + tuple(stacked_s[k] for k in order))
```

```python
import functools
import math

import jax
import jax.numpy as jnp
from jax import lax
from jax.experimental import pallas as pl
from jax.experimental.pallas import tpu as pltpu

F32 = jnp.float32
BF16 = jnp.bfloat16
HIGHEST = lax.Precision.HIGHEST

DN_HEADS = 8
DN_DK = 128
DN_DV = 128
DN_WIDTH = DN_HEADS * DN_DV
DN_CONV = 4
DN_CHUNK = 64
NSA_HEADS = 8
NSA_KV_HEADS = 2
NSA_GROUP = NSA_HEADS // NSA_KV_HEADS
HEAD_DIM = 128
NSA_WIDTH = NSA_HEADS * HEAD_DIM
KV_WIDTH = NSA_KV_HEADS * HEAD_DIM
CMP_LEN = 32
CMP_STRIDE = 16
SEL_BLOCK = 64
N_SEL = 16
WINDOW = 512
FFN_CONV = 3
LN_EPS = 1e-5
RMS_EPS = 1e-6
NEG_INF = -1e30
FORCE_SCORE = 1e4

LANES = 128
SUBLANES = 8
BF16_ROWS = 16
VMEM_LIMIT_BYTES = 56 * 1024 * 1024
MXU_DIM = 256
GDN_MATMUL_ROWS = MXU_DIM
GDN_HEADS_PER_STEP = 2

OFF_QKV = 0
OFF_DNGATE = 3 * DN_WIDTH
OFF_MERGE = OFF_DNGATE + DN_WIDTH
OFF_NQ = OFF_MERGE + 2 * 2048
OFF_ROWS = OFF_NQ + NSA_WIDTH
OFF_TAIL = OFF_ROWS + 6 * KV_WIDTH
TAIL_BETA = 0
TAIL_A = DN_HEADS
TAIL_NSAG = 2 * DN_HEADS
IN_PAD_WIDTH = 11264


def _cparams(sem):
    return pltpu.CompilerParams(dimension_semantics=sem, vmem_limit_bytes=VMEM_LIMIT_BYTES)


def _dot(a, b, precision=None):
    return jnp.dot(a, b, preferred_element_type=F32, precision=precision)


def _dot_nt(a, b, precision=None):
    return lax.dot_general(a, b, (((1,), (1,)), ((), ())), preferred_element_type=F32, precision=precision)


def _dot_tn(a, b, precision=None):
    return lax.dot_general(a, b, (((0,), (0,)), ((), ())), preferred_element_type=F32, precision=precision)


def _split_bf16(x):
    hi = x.astype(BF16)
    return hi, (x - hi.astype(F32)).astype(BF16)


def _dot_split(a, b):
    (ah, al), (bh, bl) = a, b
    return _dot(ah, bh) + (_dot(al, bh) + _dot(ah, bl))


def _sigmoid(x):
    return 1.0 / (1.0 + jnp.exp(-x))


def _silu(x):
    return x * _sigmoid(x)


def _ln(x):
    mu = jnp.mean(x, axis=-1, keepdims=True)
    xc = x - mu
    var = jnp.mean(xc * xc, axis=-1, keepdims=True)
    return xc * lax.rsqrt(var + LN_EPS)


def _rowparam_spec(p, tm, rows_per_seq):
    if p.ndim == 2:
        return pl.BlockSpec((tm, p.shape[1]), lambda i, *_: (i, 0))
    blocks_per_seq = rows_per_seq // tm
    return pl.BlockSpec((None, 1, p.shape[2]), lambda i, *_: (i // blocks_per_seq, 0, 0))


def _ada_kernel(c_ref, w_ref, b_ref, o_ref):
    h = _silu(c_ref[...]).astype(BF16)
    o_ref[...] = _dot(h, w_ref[...]) + b_ref[...]


def _ada_call(c, w_bf16, layer, b):
    m, k = c.shape
    n = w_bf16.shape[2]
    tn = 1024
    return pl.pallas_call(
        _ada_kernel,
        grid=(n // tn,),
        in_specs=[pl.BlockSpec((m, k), lambda j: (0, 0)),
                  pl.BlockSpec((None, k, tn), lambda j: (layer, 0, j)),
                  pl.BlockSpec((1, tn), lambda j: (0, j))],
        out_specs=pl.BlockSpec((m, tn), lambda j: (0, j)),
        out_shape=jax.ShapeDtypeStruct((m, n), F32),
        compiler_params=_cparams(("parallel",)),
        name="ada_mod",
    )(c, w_bf16, b.reshape(1, n))


def _ln_mm_kernel(x_ref, sh_ref, sc_ref, w_ref, o_ref, h_scr):
    @pl.when(pl.program_id(1) == 0)
    def _():
        h = _ln(x_ref[...]) * (1.0 + sc_ref[...]) + sh_ref[...]
        h_scr[...] = h.astype(BF16)

    o_ref[...] = _dot(h_scr[...], w_ref[...])


def _ln_mm_call(x, shift, scale, w_bf16, layer, rows_per_seq, name):
    m, k = x.shape
    n = w_bf16.shape[2]
    tm = min(1024, m)
    tn = 512
    return pl.pallas_call(
        _ln_mm_kernel,
        grid=(m // tm, n // tn),
        in_specs=[pl.BlockSpec((tm, k), lambda i, j: (i, 0)),
                  _rowparam_spec(shift, tm, rows_per_seq),
                  _rowparam_spec(scale, tm, rows_per_seq),
                  pl.BlockSpec((None, k, tn), lambda i, j: (layer, 0, j))],
        out_specs=pl.BlockSpec((tm, tn), lambda i, j: (i, j)),
        out_shape=jax.ShapeDtypeStruct((m, n), F32),
        scratch_shapes=[pltpu.VMEM((tm, k), BF16)],
        compiler_params=_cparams(("parallel", "arbitrary")),
        name=name,
    )(x, shift, scale, w_bf16)


def _softplus(x):
    return jnp.maximum(x, 0.0) + jnp.log1p(jnp.exp(-jnp.abs(x)))


def _gdn_gates_kernel(tail_ref, alog_ref, dtb_ref, beta_ref, gc_ref, gl_ref, *, group, n_pad):
    r = tail_ref.shape[0]
    tail = tail_ref[...]
    braw = tail[:, TAIL_BETA:TAIL_BETA + DN_HEADS]
    araw = tail[:, TAIL_A:TAIL_A + DN_HEADS]
    row = lax.broadcasted_iota(jnp.int32, (r, DN_HEADS), 0)
    real = (row % group) >= n_pad
    g = -jnp.exp(alog_ref[...]) * _softplus(araw + dtb_ref[...])
    g = jnp.where(real, g, 0.0)
    beta = jnp.where(real, _sigmoid(braw), 0.0)
    ri = lax.broadcasted_iota(jnp.int32, (r, r), 0)
    ci = lax.broadcasted_iota(jnp.int32, (r, r), 1)
    same = (ri // group) == (ci // group)
    tri = jnp.where(same & (ri >= ci), 1.0, 0.0).astype(F32)
    ones = jnp.where(same, 1.0, 0.0).astype(F32)
    beta_ref[...] = beta
    gc_ref[...] = _dot(tri, g, HIGHEST)
    gl_ref[...] = _dot(ones, g, HIGHEST)


def _gdn_gates_call(z, tail_off, a_log, dt_bias, group, n_pad):
    m = z.shape[0]
    r = min(512, m)
    tail_blk = tail_off // LANES
    out = jax.ShapeDtypeStruct((m, DN_HEADS), F32)
    ospec = pl.BlockSpec((r, DN_HEADS), lambda i: (i, 0))
    return pl.pallas_call(
        functools.partial(_gdn_gates_kernel, group=group, n_pad=n_pad),
        grid=(m // r,),
        in_specs=[pl.BlockSpec((r, LANES), lambda i: (i, tail_blk)),
                  pl.BlockSpec((1, DN_HEADS), lambda i: (0, 0)),
                  pl.BlockSpec((1, DN_HEADS), lambda i: (0, 0))],
        out_specs=[ospec, ospec, ospec],
        out_shape=[out, out, out],
        compiler_params=_cparams(("parallel",)),
        name="gdn_gates",
    )(z, a_log.reshape(1, DN_HEADS), dt_bias.reshape(1, DN_HEADS))


def _unit_lower_inverses(lowers, n_sq):
    c = lowers[0].shape[0]
    eye = (lax.broadcasted_iota(jnp.int32, (c, c), 0) == lax.broadcasted_iota(jnp.int32, (c, c), 1)).astype(F32)
    p = [-x for x in lowers]
    t = [eye + x for x in p]
    ps = [_split_bf16(x) for x in p]
    for _ in range(n_sq):
        p = [_dot_split(x, x) for x in ps]
        ps = [_split_bf16(x) for x in p]
        t = [a + _dot_split(_split_bf16(a), b) for a, b in zip(t, ps)]
    return t


def _gdn_kernel(xq_ref, xk_ref, xv_ref, wq_ref, wk_ref, wv_ref, gate_ref, nw_ref, beta_ref, gc_ref, gl_ref,
                gcrow_ref, s_in_ref, o_ref, s_out_ref, xs_scr, yq_scr, yk_scr, yv_scr,
                *, chunk, sub, per_chunk_state, n_pad):
    c = chunk
    r = xq_ref.shape[0]
    nchunk = r // c
    n_sub = c // sub
    n_heads = xq_ref.shape[1] // DN_DK
    head0 = pl.program_id(1) * n_heads
    first = pl.program_id(2) == 0

    @pl.when(first)
    def _():
        s_out_ref[...] = s_in_ref[...]
        xs_scr[:, 0:SUBLANES, :] = jnp.zeros((3, SUBLANES, n_heads * DN_DK), F32)

    row = lax.broadcasted_iota(jnp.int32, (r, 1), 0)
    real = (row % sub) >= n_pad
    for idx, (x_ref, w_ref, y_scr) in enumerate(((xq_ref, wq_ref, yq_scr), (xk_ref, wk_ref, yk_scr),
                                                 (xv_ref, wv_ref, yv_scr))):
        xs_scr[idx, SUBLANES:SUBLANES + r, :] = x_ref[...]
        w = w_ref[...]
        y = jnp.zeros((r, n_heads * DN_DK), F32)
        for j in range(DN_CONV):
            off = SUBLANES - (DN_CONV - 1) + j
            y = y + xs_scr[idx, off:off + r, :] * w[j:j + 1, :]
        xs_scr[idx, 0:SUBLANES, :] = xs_scr[idx, r:r + SUBLANES, :]
        y_scr[...] = jnp.where(real, _silu(y), 0.0)

    lane = lax.broadcasted_iota(jnp.int32, (r, DN_HEADS), 1)

    def col(ref, head):
        return jnp.sum(jnp.where(lane == head, ref[...], 0.0), axis=1, keepdims=True)

    cols = [(col(beta_ref, head0 + hh), col(gc_ref, head0 + hh), col(gl_ref, head0 + hh)) for hh in range(n_heads)]

    ri = lax.broadcasted_iota(jnp.int32, (c, c), 0)
    ci = lax.broadcasted_iota(jnp.int32, (c, c), 1)
    same = (ri // sub) == (ci // sub)
    causal = same & (ri >= ci)
    strict = same & (ri > ci)
    n_sq = int(math.log2(sub)) - 1
    nw = nw_ref[...]

    lanes = [(ch, hh) for ch in range(nchunk) for hh in range(n_heads)]
    rows_of = lambda x, ch: x[ch * c:(ch + 1) * c, :]
    q = [yq_scr[ch * c:(ch + 1) * c, hh * DN_DK:(hh + 1) * DN_DK] for ch, hh in lanes]
    k = [yk_scr[ch * c:(ch + 1) * c, hh * DN_DK:(hh + 1) * DN_DK] for ch, hh in lanes]
    v = [yv_scr[ch * c:(ch + 1) * c, hh * DN_DV:(hh + 1) * DN_DV] for ch, hh in lanes]
    q = [x * lax.rsqrt(jnp.sum(x * x, axis=-1, keepdims=True) + RMS_EPS) * (DN_DK ** -0.5) for x in q]
    k = [x * lax.rsqrt(jnp.sum(x * x, axis=-1, keepdims=True) + RMS_EPS) for x in k]
    beta = [rows_of(cols[hh][0], ch) for ch, hh in lanes]
    gc = [rows_of(cols[hh][1], ch) for ch, hh in lanes]
    gl = [rows_of(cols[hh][2], ch) for ch, hh in lanes]
    gcrow = [gcrow_ref[hh:hh + 1, ch * c:(ch + 1) * c] for ch, hh in lanes]
    decay = [jnp.where(causal, jnp.exp(jnp.where(causal, g - gr, 0.0)), 0.0) for g, gr in zip(gc, gcrow)]
    kb = [x * b for x, b in zip(k, beta)]
    kbf = [x.astype(BF16) for x in k]
    lower = [jnp.where(strict, _dot_nt(a.astype(BF16), b) * d, 0.0) for a, b, d in zip(kb, kbf, decay)]
    tmat = _unit_lower_inverses(lower, n_sq)
    eg = [jnp.exp(g) for g in gc]
    uw = [_dot(t.astype(BF16), jnp.concatenate([x * b, y * e], axis=1).astype(BF16))
          for t, x, b, y, e in zip(tmat, v, beta, kb, eg)]
    u = [x[:, :DN_DV] for x in uw]
    wb = [x[:, DN_DV:].astype(BF16) for x in uw]
    intra = [jnp.where(causal, _dot_nt(a.astype(BF16), b) * d, 0.0).astype(BF16) for a, b, d in zip(q, kbf, decay)]
    qe = [(a * e).astype(BF16) for a, e in zip(q, eg)]
    kd = [(a * jnp.exp(gl_ - g)).astype(BF16) for a, gl_, g in zip(k, gl, gc)]

    heads = range(n_heads)
    for ch in range(nchunk):
        lane_of = [ch * n_heads + hh for hh in heads]
        v_new_parts = [[] for _ in heads]
        o_parts = [[] for _ in heads]
        for s in range(n_sub):
            a0 = s * sub
            sidx = (ch * n_sub + s) if per_chunk_state else 0
            state = [s_out_ref[sidx, hh] for hh in heads]
            sb = [x.astype(BF16) for x in state]
            v_new = [u[i][a0:a0 + sub, :] - _dot(wb[i][a0:a0 + sub, :], b) for i, b in zip(lane_of, sb)]
            for hh, i in enumerate(lane_of):
                o_parts[hh].append(_dot(qe[i][a0:a0 + sub, :], sb[hh]))
            for hh, i in enumerate(lane_of):
                s_out_ref[sidx, hh] = (state[hh] * jnp.exp(gl[i][a0:a0 + 1, :])
                                       + _dot_tn(kd[i][a0:a0 + sub, :], v_new[hh].astype(BF16)))
                v_new_parts[hh].append(v_new[hh])
        for hh, i in enumerate(lane_of):
            v_all = v_new_parts[hh][0] if n_sub == 1 else jnp.concatenate(v_new_parts[hh], axis=0)
            o_state = o_parts[hh][0] if n_sub == 1 else jnp.concatenate(o_parts[hh], axis=0)
            o = o_state + _dot(intra[i], v_all.astype(BF16))
            o = o * lax.rsqrt(jnp.mean(o * o, axis=-1, keepdims=True) + RMS_EPS) * nw
            r0, h0 = ch * c, hh * DN_DV
            o_ref[r0:r0 + c, h0:h0 + DN_DV] = o * _silu(gate_ref[r0:r0 + c, h0:h0 + DN_DV])


def _gdn_call(x, x_col0, conv_w, gate, gate_col0, norm_w, beta, gc, gl, gcrow, s_in, layer, *, rows_per_seq, chunk,
              sub, per_chunk_state, n_pad):
    m = x.shape[0]
    r = min(512, rows_per_seq) if not per_chunk_state else min(512, m)
    chunk = min(chunk, r)
    n_seq = m // rows_per_seq if not per_chunk_state else m // r
    nt = rows_per_seq // r if not per_chunk_state else 1
    n_state_blk = r // sub if per_chunk_state else 1
    hp = GDN_HEADS_PER_STEP
    wide = hp * DN_DK
    xb = x_col0 // wide
    gb = gate_col0 // wide
    hw = DN_WIDTH // wide
    assert x_col0 % wide == 0 and gate_col0 % wide == 0 and DN_HEADS % hp == 0

    def rowblk(b, h, t):
        return b * nt + t

    xspec = lambda off: pl.BlockSpec((r, wide), lambda b, h, t: (rowblk(b, h, t), xb + off + h))
    wspec = lambda off: pl.BlockSpec((DN_CONV, wide), lambda b, h, t: (0, off + h))
    colspec = pl.BlockSpec((r, DN_HEADS), lambda b, h, t: (rowblk(b, h, t), 0))
    sspec = pl.BlockSpec((n_state_blk, hp, DN_DK, DN_DV), lambda b, h, t: (b, h, 0, 0))
    return pl.pallas_call(
        functools.partial(_gdn_kernel, chunk=chunk, sub=sub, per_chunk_state=per_chunk_state, n_pad=n_pad),
        grid=(n_seq, DN_HEADS // hp, nt),
        in_specs=[xspec(0), xspec(hw), xspec(2 * hw), wspec(0), wspec(hw), wspec(2 * hw),
                  pl.BlockSpec((r, wide), lambda b, h, t: (rowblk(b, h, t), gb + h)),
                  pl.BlockSpec((1, DN_DV), lambda b, h, t: (0, 0)),
                  colspec, colspec, colspec,
                  pl.BlockSpec((None, hp, r), lambda b, h, t: (h, 0, rowblk(b, h, t))),
                  pl.BlockSpec((None, n_state_blk, hp, DN_DK, DN_DV), lambda b, h, t: (layer, b, h, 0, 0))],
        out_specs=[pl.BlockSpec((r, wide), lambda b, h, t: (rowblk(b, h, t), h)), sspec],
        out_shape=[jax.ShapeDtypeStruct((m, DN_WIDTH), F32), jax.ShapeDtypeStruct(s_in.shape[1:], F32)],
        scratch_shapes=[pltpu.VMEM((3, r + SUBLANES, wide), F32), pltpu.VMEM((r, wide), F32),
                        pltpu.VMEM((r, wide), F32), pltpu.VMEM((r, wide), F32)],
        compiler_params=_cparams(("parallel", "parallel", "arbitrary")),
        name="gdn_delta",
    )(x, x, x, conv_w, conv_w, conv_w, gate, norm_w.reshape(1, DN_DV), beta, gc, gl,
      gcrow.reshape(DN_HEADS // hp, hp, m), s_in)


def _pq_kernel(x_ref, w_ref, o_ref):
    for kv in range(NSA_KV_HEADS):
        acc = jnp.zeros((x_ref.shape[0], 2 * HEAD_DIM), F32)
        for j in range(CMP_STRIDE):
            c0 = j * KV_WIDTH + kv * HEAD_DIM
            acc = acc + _dot(x_ref[:, c0:c0 + HEAD_DIM].astype(BF16), w_ref[j])
        o_ref[:, kv * 2 * HEAD_DIM:(kv + 1) * 2 * HEAD_DIM] = acc


def _pq_weights(w1):
    return jnp.concatenate([w1[:CMP_STRIDE], w1[CMP_STRIDE:]], axis=-1).astype(BF16)


def _pq_call(chunks, wpq):
    nc, width = chunks.shape
    tc = 512 if nc % 512 == 0 else nc
    return pl.pallas_call(
        _pq_kernel,
        grid=(nc // tc,),
        in_specs=[pl.BlockSpec((tc, width), lambda i: (i, 0)),
                  pl.BlockSpec(wpq.shape, lambda i: (0, 0, 0))],
        out_specs=pl.BlockSpec((tc, 4 * HEAD_DIM), lambda i: (i, 0)),
        out_shape=jax.ShapeDtypeStruct((nc, 4 * HEAD_DIM), F32),
        compiler_params=_cparams(("parallel",)),
        name="nsa_compress_pq",
    )(chunks, wpq)


def _pq_pool_kernel(*refs):
    x_refs, w_ref, o_ref = refs[:CMP_STRIDE], refs[CMP_STRIDE], refs[CMP_STRIDE + 1]
    rows = o_ref.shape[1]
    acc = jnp.zeros((rows, 2 * HEAD_DIM), F32)
    for j in range(CMP_STRIDE):
        acc = acc + _dot(x_refs[j][...].reshape(rows, HEAD_DIM).astype(BF16), w_ref[j])
    o_ref[0] = acc[:, :HEAD_DIM]
    o_ref[1] = acc[:, HEAD_DIM:]


def _pq_pool_call(pool, layer, wpq):
    depth, n_phys, page, n_kv, hd = pool.shape
    nc = n_phys * (page // CMP_STRIDE)
    chunks = pool.reshape(depth, nc, CMP_STRIDE, n_kv, hd)
    tc = 512 if nc % 512 == 0 else nc
    xspec = lambda j: pl.BlockSpec((None, tc, None, n_kv, hd), lambda i: (layer, i, j, 0, 0))
    return pl.pallas_call(
        _pq_pool_kernel,
        grid=(nc // tc,),
        in_specs=[xspec(j) for j in range(CMP_STRIDE)] + [pl.BlockSpec(wpq.shape, lambda i: (0, 0, 0))],
        out_specs=pl.BlockSpec((2, tc * n_kv, HEAD_DIM), lambda i: (0, i, 0)),
        out_shape=jax.ShapeDtypeStruct((2, nc * n_kv, HEAD_DIM), F32),
        compiler_params=_cparams(("parallel",)),
        name="nsa_compress_pool",
    )(*([chunks] * CMP_STRIDE), wpq)


def _pe_term(pe, wpq):
    z = jnp.zeros((CMP_STRIDE, HEAD_DIM), F32)
    rows = jnp.stack([jnp.concatenate([pe[:CMP_STRIDE], z], axis=1).reshape(-1),
                      jnp.concatenate([pe[CMP_STRIDE:], z], axis=1).reshape(-1)])
    rows = jnp.concatenate([rows, jnp.zeros((SUBLANES - 2, rows.shape[1]), F32)], axis=0)
    pq = _pq_call(rows, wpq)
    return (pq[0, :HEAD_DIM] + pq[1, HEAD_DIM:2 * HEAD_DIM]).reshape(1, HEAD_DIM)


def _masked_softmax(s, valid):
    s = jnp.where(valid, s, NEG_INF)
    m = jnp.max(s, axis=-1, keepdims=True)
    e = jnp.exp(s - m)
    p = e / jnp.sum(e, axis=-1, keepdims=True)
    return jnp.where(valid, p, 0.0)


def _overlap_matrix(n_cmp, n_blk, transposed=False):
    shape, n_dim, j_dim = ((n_blk, n_cmp), 1, 0) if transposed else ((n_cmp, n_blk), 0, 1)
    n = lax.broadcasted_iota(jnp.int32, shape, n_dim) * CMP_STRIDE
    j = lax.broadcasted_iota(jnp.int32, shape, j_dim) * SEL_BLOCK
    ov = jnp.maximum(jnp.minimum(n + CMP_LEN, j + SEL_BLOCK) - jnp.maximum(n, j), 0)
    return ov.astype(F32) / CMP_LEN


AUG_HI = SEL_BLOCK
AUG_LO = SEL_BLOCK + 1


def _key_aug(pos, with_blocks):
    lane = lax.broadcasted_iota(jnp.int32, pos.shape, 1)
    hi = pos // SEL_BLOCK
    aug = jnp.where(lane == AUG_HI, hi, jnp.where(lane == AUG_LO, pos % SEL_BLOCK, 0))
    if with_blocks:
        aug = jnp.where(lane < SEL_BLOCK, jnp.where(hi == lane, 1, 0), aug)
    return aug.astype(F32).astype(BF16)


def _select_bias_t(imp_t, tpos_row):
    n_blk, t = imp_t.shape
    j = lax.broadcasted_iota(jnp.int32, (n_blk, t), 0)
    q_blk = tpos_row // SEL_BLOCK
    forced = (j == 0) | (j == q_blk) | (j == q_blk - 1)
    score = jnp.where(forced, FORCE_SCORE, jnp.where(j <= q_blk, imp_t, -1.0))
    rank = jnp.zeros((n_blk, t), F32)
    for jj in range(n_blk):
        sj = score[jj:jj + 1, :]
        ahead = (sj > score) | ((sj == score) & (j > jj))
        rank = rank + jnp.where(ahead, 1.0, 0.0)
    return jnp.where(rank < N_SEL, 0.0, NEG_INF)


def _select_bias(imp, tpos):
    t, n_blk = imp.shape
    j = lax.broadcasted_iota(jnp.int32, (t, n_blk), 1)
    q_blk = tpos // SEL_BLOCK
    forced = (j == 0) | (j == q_blk) | (j == q_blk - 1)
    score = jnp.where(forced, FORCE_SCORE, jnp.where(j <= q_blk, imp, -1.0))
    rank = jnp.zeros((t, n_blk), F32)
    for jj in range(n_blk):
        sj = score[:, jj:jj + 1]
        ahead = (sj > score) | ((sj == score) & (j > jj))
        rank = rank + jnp.where(ahead, 1.0, 0.0)
    return jnp.where(rank < N_SEL, 0.0, NEG_INF)


def _block_onehot(n_blk, k0, tk):
    j = lax.broadcasted_iota(jnp.int32, (n_blk, tk), 0)
    s = lax.broadcasted_iota(jnp.int32, (n_blk, tk), 1) + k0
    return jnp.where(s // SEL_BLOCK == j, 1.0, 0.0).astype(BF16)


def _compressed_tokens(hid, w2_ref):
    return _dot(_silu(hid).astype(BF16), w2_ref[...])


def _cmp_branch(q4b, cmpk, cmpv, tpos4, slope4, tq):
    n_cmp = cmpk.shape[0]
    scale = HEAD_DIM ** -0.5
    s = _dot_nt(q4b, cmpk.astype(BF16)) * scale
    c_end = lax.broadcasted_iota(jnp.int32, (1, n_cmp), 1) * CMP_STRIDE + (CMP_LEN - 1)
    dist = tpos4 - c_end
    valid = dist >= 0
    s = s - slope4 * dist.astype(F32)
    p = _masked_softmax(s, valid)
    o = _dot(p.astype(BF16), cmpv.astype(BF16))
    psum = p[0:tq]
    for g in range(1, NSA_GROUP):
        psum = psum + p[g * tq:(g + 1) * tq]
    return o, psum


def _slopes(kv, tq):
    g = lax.broadcasted_iota(jnp.int32, (NSA_GROUP * tq, 1), 0) // tq
    head = (g + kv * NSA_GROUP + 1).astype(F32)
    return jnp.exp(head * (-(8.0 / NSA_HEADS) * math.log(2.0)))


def _tail_col(tail, idx):
    lane = lax.broadcasted_iota(jnp.int32, tail.shape, 1)
    return jnp.sum(jnp.where(lane == idx, tail, 0.0), axis=1, keepdims=True)


def _combine_branches(tail, kv, o_cmp, o_slc, o_win, tq):
    outs = []
    for g in range(NSA_GROUP):
        head = kv * NSA_GROUP + g
        acc = jnp.zeros((tq, HEAD_DIM), F32)
        for br, o in enumerate((o_cmp, o_slc, o_win)):
            gate = _sigmoid(_tail_col(tail, TAIL_NSAG + br * NSA_HEADS + head))
            acc = acc + gate * o[g * tq:(g + 1) * tq]
        outs.append(acc)
    return outs


def _nsa_prompt_kernel(q_ref, pqk_ref, pqv_ref, pek_ref, pev_ref, w2k_ref, w2v_ref, sk_ref, sv_ref, wk_ref, wv_ref,
                       tail_ref, o_ref, ccat_scr, cmpv_scr, kcat_scr, sv_scr, wcat_scr, wv_scr, *, tk):
    tq = q_ref.shape[0]
    t_len = sk_ref.shape[0]
    n_cmp = pqk_ref.shape[0]
    n_blk = t_len // SEL_BLOCK
    kv = pl.program_id(1)
    i = pl.program_id(2)
    t0 = i * tq
    scale = HEAD_DIM ** -0.5

    @pl.when(i == 0)
    def _():
        kaug = _key_aug(lax.broadcasted_iota(jnp.int32, (t_len, LANES), 0), True)
        for cat_scr, k_ref, vb_scr, v_ref in ((kcat_scr, sk_ref, sv_scr, sv_ref), (wcat_scr, wk_ref, wv_scr, wv_ref)):
            cat_scr[:, 0:HEAD_DIM] = k_ref[...].astype(BF16)
            cat_scr[:, HEAD_DIM:] = kaug
            vb_scr[...] = v_ref[...].astype(BF16)
        tokens = []
        for pq_ref, pe_ref, w2_ref in ((pqk_ref, pek_ref, w2k_ref), (pqv_ref, pev_ref, w2v_ref)):
            p = pq_ref[:, 0:HEAD_DIM]
            qn = pltpu.roll(pq_ref[:, HEAD_DIM:2 * HEAD_DIM], n_cmp - 1, 0)
            tokens.append(_compressed_tokens(p + qn + pe_ref[...], w2_ref).astype(BF16))
        c_end = lax.broadcasted_iota(jnp.int32, (n_cmp, LANES), 0) * CMP_STRIDE + (CMP_LEN - 1)
        ccat_scr[:, 0:HEAD_DIM] = tokens[0]
        ccat_scr[:, HEAD_DIM:] = _key_aug(c_end, False)
        cmpv_scr[...] = tokens[1]

    qb = q_ref[...] * scale
    q4b = jnp.concatenate([qb[:, g * HEAD_DIM:(g + 1) * HEAD_DIM] for g in range(NSA_GROUP)], axis=0).astype(BF16)
    rows4 = NSA_GROUP * tq
    tpos4 = t0 + lax.broadcasted_iota(jnp.int32, (rows4, 1), 0) % tq
    slope4 = _slopes(kv, tq)
    lane = lax.broadcasted_iota(jnp.int32, (rows4, LANES), 1)
    alibi = jnp.where(lane == AUG_HI, slope4 * SEL_BLOCK, jnp.where(lane == AUG_LO, slope4, 0.0))
    q_plain = jnp.concatenate([q4b, alibi.astype(BF16)], axis=1)

    c_end = lax.broadcasted_iota(jnp.int32, (1, n_cmp), 1) * CMP_STRIDE + (CMP_LEN - 1)
    p = _masked_softmax(_dot_nt(q_plain, ccat_scr[...]), c_end <= tpos4)
    o_cmp = _dot(p.astype(BF16), cmpv_scr[...])
    psum = p[0:tq]
    for g in range(1, NSA_GROUP):
        psum = psum + p[g * tq:(g + 1) * tq]

    imp_t = _dot_nt(_overlap_matrix(n_cmp, n_blk, transposed=True), psum, HIGHEST)
    tpos_row = t0 + lax.broadcasted_iota(jnp.int32, (1, tq), 1)
    selb = _select_bias_t(imp_t, tpos_row).T
    selb = jnp.concatenate([selb, jnp.zeros((tq, LANES - n_blk), F32)], axis=1)
    selb4 = jnp.concatenate([selb] * NSA_GROUP, axis=0)
    q_sel = jnp.concatenate([q4b, jnp.where(lane < SEL_BLOCK, selb4, alibi).astype(BF16)], axis=1)

    def slc_tile(jt, carry, diagonal):
        m, l, acc = carry
        k0 = pl.multiple_of(jt * tk, tk)
        s = _dot_nt(q_sel, kcat_scr[pl.ds(k0, tk), :])
        if diagonal:
            s = jnp.where(k0 + lax.broadcasted_iota(jnp.int32, (1, tk), 1) <= tpos4, s, NEG_INF)
        m_new = jnp.maximum(m, jnp.max(s, axis=-1, keepdims=True))
        alpha = jnp.exp(m - m_new)
        e = jnp.exp(s - m_new)
        l = alpha * l + jnp.sum(e, axis=-1, keepdims=True)
        acc = alpha * acc + _dot(e.astype(BF16), sv_scr[pl.ds(k0, tk), :])
        return m_new, l, acc

    n_full = t0 // tk
    init = (jnp.full((rows4, 1), NEG_INF, F32), jnp.zeros((rows4, 1), F32), jnp.zeros((rows4, HEAD_DIM), F32))
    carry = lax.fori_loop(0, n_full, lambda jt, c: slc_tile(jt, c, False), init)
    _, l, acc = slc_tile(n_full, carry, True)
    o_slc = acc / l

    band = min(WINDOW + tq, t_len)
    b0 = pl.multiple_of(jnp.maximum(t0 + tq - band, 0), BF16_ROWS)
    dist = tpos4 - (b0 + lax.broadcasted_iota(jnp.int32, (1, band), 1))
    p = _masked_softmax(_dot_nt(q_plain, wcat_scr[pl.ds(b0, band), :]), (dist >= 0) & (dist < WINDOW))
    o_win = _dot(p.astype(BF16), wv_scr[pl.ds(b0, band), :])

    outs = _combine_branches(tail_ref[...], kv, o_cmp, o_slc, o_win, tq)
    for g in range(NSA_GROUP):
        o_ref[:, g * HEAD_DIM:(g + 1) * HEAD_DIM] = outs[g]


def _nsa_prompt_call(z, pqk, pqv, pek, pev, w2k, w2v, n_batch, t_len):
    tq = min(128, t_len)
    tk = min(512, t_len)
    nq = t_len // tq
    n_cmp = t_len // CMP_STRIDE
    qblk0 = OFF_NQ // (NSA_GROUP * HEAD_DIM)
    assert t_len <= SEL_BLOCK * SEL_BLOCK and 8 % NSA_HEADS == 0 and tk % tq == 0
    rowspec = lambda off: pl.BlockSpec((t_len, HEAD_DIM), lambda b, kv, i: (b, off // HEAD_DIM + kv))
    pqspec = pl.BlockSpec((n_cmp, 2 * HEAD_DIM), lambda b, kv, i: (b, kv))
    cspec = lambda shape: pl.BlockSpec(shape, lambda b, kv, i: (0, 0))
    return pl.pallas_call(
        functools.partial(_nsa_prompt_kernel, tk=tk),
        grid=(n_batch, NSA_KV_HEADS, nq),
        in_specs=[pl.BlockSpec((tq, NSA_GROUP * HEAD_DIM), lambda b, kv, i: (b * nq + i, qblk0 + kv)),
                  pqspec, pqspec, cspec((1, HEAD_DIM)), cspec((1, HEAD_DIM)),
                  cspec((HEAD_DIM, HEAD_DIM)), cspec((HEAD_DIM, HEAD_DIM)),
                  rowspec(OFF_ROWS + 2 * KV_WIDTH), rowspec(OFF_ROWS + 3 * KV_WIDTH),
                  rowspec(OFF_ROWS + 4 * KV_WIDTH), rowspec(OFF_ROWS + 5 * KV_WIDTH),
                  pl.BlockSpec((tq, LANES), lambda b, kv, i: (b * nq + i, OFF_TAIL // LANES))],
        out_specs=pl.BlockSpec((tq, NSA_GROUP * HEAD_DIM), lambda b, kv, i: (b * nq + i, kv)),
        out_shape=jax.ShapeDtypeStruct((n_batch * t_len, NSA_WIDTH), F32),
        scratch_shapes=[pltpu.VMEM((n_cmp, 2 * HEAD_DIM), BF16), pltpu.VMEM((n_cmp, HEAD_DIM), BF16),
                        pltpu.VMEM((t_len, 2 * HEAD_DIM), BF16), pltpu.VMEM((t_len, HEAD_DIM), BF16),
                        pltpu.VMEM((t_len, 2 * HEAD_DIM), BF16), pltpu.VMEM((t_len, HEAD_DIM), BF16)],
        compiler_params=_cparams(("parallel", "parallel", "arbitrary")),
        name="nsa_prompt",
    )(z, pqk, pqv, pek, pev, w2k, w2v, z, z, z, z, z)


def _nsa_sample_kernel(pt_ref, q_ref, new_ref, pqnk_ref, pqnv_ref, pek_ref, pev_ref, w2k_ref, w2v_ref, wink_ref,
                       winv_ref, tail_ref, pqk_hbm, pqv_hbm, sk_hbm, sv_hbm, o_ref,
                       pqk_buf, pqv_buf, sk_buf, sv_buf, sem, *, n_pages, past_len, layer):
    b = pl.program_id(0)
    nb = pl.num_programs(0)
    t_new = q_ref.shape[0]
    page = PAGE_ROWS
    cpp = page // CMP_STRIDE
    new_rows = LANES
    n_keys = past_len + new_rows
    n_cmp = pqk_buf.shape[2]
    n_blk = SEL_BLOCK
    scale = HEAD_DIM ** -0.5
    ppr = cpp * NSA_KV_HEADS
    kpr = page * NSA_KV_HEADS

    def copies(seq, slot):
        out = []
        for p in range(n_pages):
            pg = pt_ref[seq, p]
            out.append(pltpu.make_async_copy(pqk_hbm.at[:, pl.ds(pg * ppr, ppr)], pqk_buf.at[slot, :, pl.ds(p * ppr, ppr)], sem.at[slot, 0]))
            out.append(pltpu.make_async_copy(pqv_hbm.at[:, pl.ds(pg * ppr, ppr)], pqv_buf.at[slot, :, pl.ds(p * ppr, ppr)], sem.at[slot, 1]))
            out.append(pltpu.make_async_copy(sk_hbm.at[layer, pg], sk_buf.at[slot, pl.ds(p * kpr, kpr)], sem.at[slot, 2]))
            out.append(pltpu.make_async_copy(sv_hbm.at[layer, pg], sv_buf.at[slot, pl.ds(p * kpr, kpr)], sem.at[slot, 3]))
        return out

    slot = b % 2

    @pl.when(b == 0)
    def _():
        for cp in copies(0, 0):
            cp.start()

    @pl.when(b + 1 < nb)
    def _():
        for cp in copies(b + 1, 1 - slot):
            cp.start()

    new = new_ref[...]
    qb = q_ref[...]
    tail = tail_ref[...]
    tq = t_new
    rows4 = NSA_GROUP * tq
    tpos4 = past_len + lax.broadcasted_iota(jnp.int32, (rows4, 1), 0) % tq
    tpos = past_len + lax.broadcasted_iota(jnp.int32, (tq, 1), 0)

    win_len = wink_ref.shape[0] // NSA_KV_HEADS
    band = win_len + new_rows

    def new_tile(which, kv):
        c0 = which * KV_WIDTH + kv * HEAD_DIM
        return jnp.concatenate([new[:, c0:c0 + HEAD_DIM], jnp.zeros((new_rows - t_new, HEAD_DIM), F32)],
                               axis=0).astype(BF16)

    for cp in copies(b, slot):
        cp.wait()

    n_pool = n_pages * cpp
    rown = lax.broadcasted_iota(jnp.int32, (n_pool, 1), 0)
    onehot = _block_onehot(n_blk, 0, n_keys)
    for kv in range(NSA_KV_HEADS):
        q4b = jnp.concatenate([qb[:, (kv * NSA_GROUP + g) * HEAD_DIM:(kv * NSA_GROUP + g + 1) * HEAD_DIM]
                               for g in range(NSA_GROUP)], axis=0).astype(BF16)
        slope4 = _slopes(kv, tq)
        c0 = kv * 2 * HEAD_DIM
        cmp_tokens = []
        for pq_buf, pqn_ref, pe_ref, w2_ref in ((pqk_buf, pqnk_ref, pek_ref, w2k_ref), (pqv_buf, pqnv_ref, pev_ref, w2v_ref)):
            kv_rows = pl.ds(kv, n_pool, stride=NSA_KV_HEADS)
            p = pq_buf[slot, 0, kv_rows, :]
            qn = pltpu.roll(pq_buf[slot, 1, kv_rows, :], n_pool - 1, 0)
            pn = pqn_ref[:, c0:c0 + HEAD_DIM]
            qnn = pqn_ref[:, c0 + HEAD_DIM:c0 + 2 * HEAD_DIM]
            qn = jnp.where(rown == n_pool - 1, qnn, qn)
            hid_pool = p + qn
            rowt = lax.broadcasted_iota(jnp.int32, (n_cmp - n_pool, 1), 0)
            hid_tail = jnp.where(rowt == 0, pn, 0.0)
            hid = jnp.concatenate([hid_pool, hid_tail], axis=0) + pe_ref[...]
            cmp_tokens.append(_compressed_tokens(hid, w2_ref))
        o_cmp, psum = _cmp_branch(q4b, cmp_tokens[0], cmp_tokens[1], tpos4, slope4, tq)
        imp = _dot(psum, _overlap_matrix(n_cmp, n_blk), HIGHEST)
        selb = _select_bias(imp, tpos).astype(BF16)
        selb4 = jnp.concatenate([selb] * NSA_GROUP, axis=0)

        ks = sk_buf[slot, pl.ds(kv, past_len, stride=NSA_KV_HEADS), :].astype(BF16)
        vs = sv_buf[slot, pl.ds(kv, past_len, stride=NSA_KV_HEADS), :].astype(BF16)
        s = jnp.concatenate([_dot_nt(q4b, ks), _dot_nt(q4b, new_tile(2, kv))], axis=1) * scale + _dot(selb4, onehot)
        dist = tpos4 - lax.broadcasted_iota(jnp.int32, (1, n_keys), 1)
        p = _masked_softmax(s - slope4 * dist.astype(F32), dist >= 0).astype(BF16)
        o_slc = _dot(p[:, :past_len], vs) + _dot(p[:, past_len:], new_tile(3, kv))

        kw = wink_ref[pl.ds(kv, win_len, stride=NSA_KV_HEADS), :].astype(BF16)
        vw = winv_ref[pl.ds(kv, win_len, stride=NSA_KV_HEADS), :].astype(BF16)
        s = jnp.concatenate([_dot_nt(q4b, kw), _dot_nt(q4b, new_tile(4, kv))], axis=1) * scale
        k_pos = (past_len - win_len) + lax.broadcasted_iota(jnp.int32, (1, band), 1)
        dist = tpos4 - k_pos
        p = _masked_softmax(s - slope4 * dist.astype(F32), (dist >= 0) & (dist < WINDOW)).astype(BF16)
        o_win = _dot(p[:, :win_len], vw) + _dot(p[:, win_len:], new_tile(5, kv))

        outs = _combine_branches(tail, kv, o_cmp, o_slc, o_win, tq)
        for g in range(NSA_GROUP):
            h0 = (kv * NSA_GROUP + g) * HEAD_DIM
            o_ref[:, h0:h0 + HEAD_DIM] = outs[g]


PAGE_ROWS = 128


def _kv_rows(x):
    return x.reshape(x.shape[:-3] + (x.shape[-3] * x.shape[-2], x.shape[-1]))


def _nsa_sample_call(page_table, z3, pq_new_k, pq_new_v, pek, pev, w2k, w2v, win_k, win_v, pq_pool_k, pq_pool_v,
                     pool_sk, pool_sv, layer):
    bd, t_new, _ = z3.shape
    n_pages = page_table.shape[1]
    past_len = n_pages * PAGE_ROWS
    win_len = win_k.shape[2]
    n_cmp = 2 * n_pages * (PAGE_ROWS // CMP_STRIDE)
    assert (past_len + LANES) // SEL_BLOCK <= SEL_BLOCK and past_len // CMP_STRIDE + 8 <= n_cmp and t_new <= LANES

    def zspec(width, col0):
        return pl.BlockSpec((None, t_new, width), lambda b, pt: (b, 0, col0 // width))

    cspec = lambda shape: pl.BlockSpec(shape, lambda b, pt: (0,) * len(shape))
    pqnspec = pl.BlockSpec((None, 1, 4 * HEAD_DIM), lambda b, pt: (b, 0, 0))
    winspec = pl.BlockSpec((None, None, win_len * NSA_KV_HEADS, HEAD_DIM), lambda b, pt: (layer, b, 0, 0))
    anyspec = pl.BlockSpec(memory_space=pl.ANY)
    grid_spec = pltpu.PrefetchScalarGridSpec(
        num_scalar_prefetch=1,
        grid=(bd,),
        in_specs=[zspec(NSA_WIDTH, OFF_NQ), zspec(6 * KV_WIDTH, OFF_ROWS), pqnspec, pqnspec,
                  cspec((1, HEAD_DIM)), cspec((1, HEAD_DIM)), cspec((HEAD_DIM, HEAD_DIM)), cspec((HEAD_DIM, HEAD_DIM)),
                  winspec, winspec, zspec(LANES, OFF_TAIL), anyspec, anyspec, anyspec, anyspec],
        out_specs=pl.BlockSpec((None, t_new, NSA_WIDTH), lambda b, pt: (b, 0, 0)),
        scratch_shapes=[pltpu.VMEM((2, 2, n_cmp, HEAD_DIM), F32),
                        pltpu.VMEM((2, 2, n_cmp, HEAD_DIM), F32),
                        pltpu.VMEM((2, past_len * NSA_KV_HEADS, HEAD_DIM), F32),
                        pltpu.VMEM((2, past_len * NSA_KV_HEADS, HEAD_DIM), F32),
                        pltpu.SemaphoreType.DMA((2, 4))],
    )
    return pl.pallas_call(
        functools.partial(_nsa_sample_kernel, n_pages=n_pages, past_len=past_len, layer=layer),
        grid_spec=grid_spec,
        out_shape=jax.ShapeDtypeStruct((bd, t_new, NSA_WIDTH), F32),
        compiler_params=_cparams(("arbitrary",)),
        name="nsa_sample",
    )(page_table, z3, z3, pq_new_k, pq_new_v, pek, pev, w2k, w2v, _kv_rows(win_k), _kv_rows(win_v), z3,
      pq_pool_k, pq_pool_v, _kv_rows(pool_sk), _kv_rows(pool_sv))


def _merge_kernel(ya_ref, yb_ref, ga_ref, gb_ref, wa_ref, wb_ref, wo_ref, x_ref, gate_ref, g_ref, b_ref, o_ref,
                  *, alpha):
    a = _dot(ya_ref[...].astype(BF16), wa_ref[...])
    b = _dot(yb_ref[...].astype(BF16), wb_ref[...])
    mix = _sigmoid(ga_ref[...]) * a + _sigmoid(gb_ref[...]) * b
    y = _dot(mix.astype(BF16), wo_ref[...])
    o_ref[...] = _ln(alpha * x_ref[...] + gate_ref[...] * y) * g_ref[...] + b_ref[...]


def _merge_call(ya, yb, z, wa, wb, wo, layer, x, gate, ln_g, ln_b, rows_per_seq, alpha):
    m, d = x.shape
    tm = min(256, m)
    rspec = lambda w: pl.BlockSpec((tm, w), lambda i: (i, 0))
    cspec = lambda a: pl.BlockSpec((None,) + a.shape[1:], lambda i: (layer, 0, 0))
    return pl.pallas_call(
        functools.partial(_merge_kernel, alpha=alpha),
        grid=(m // tm,),
        in_specs=[rspec(DN_WIDTH), rspec(NSA_WIDTH),
                  pl.BlockSpec((tm, d), lambda i: (i, OFF_MERGE // d)),
                  pl.BlockSpec((tm, d), lambda i: (i, OFF_MERGE // d + 1)),
                  cspec(wa), cspec(wb), cspec(wo), rspec(d), _rowparam_spec(gate, tm, rows_per_seq),
                  pl.BlockSpec((1, d), lambda i: (0, 0)), pl.BlockSpec((1, d), lambda i: (0, 0))],
        out_specs=rspec(d),
        out_shape=jax.ShapeDtypeStruct((m, d), F32),
        compiler_params=_cparams(("parallel",)),
        name="merge_out_ln",
    )(ya, yb, z, z, wa, wb, wo, x, gate, ln_g.reshape(1, d), ln_b.reshape(1, d))


def _ffn_down_kernel(ua_ref, ub_ref, pa_ref, pb_ref, cwa_ref, cwb_ref, wd_ref, x_ref, gate_ref, g_ref, b_ref, o_ref,
                     acc_scr, *, alpha, rows_per_seq):
    i = pl.program_id(0)
    k = pl.program_id(1)
    tm = ua_ref.shape[0]
    seq_start = (i * tm) % rows_per_seq == 0
    row8 = lax.broadcasted_iota(jnp.int32, (SUBLANES, 1), 0)

    def conv(u_ref, p_ref, w_ref):
        u = u_ref[...]
        w = w_ref[...]
        prev = jnp.where(seq_start, 0.0, p_ref[...])
        y = u * w[FFN_CONV - 1:FFN_CONV, :]
        for sh in range(1, FFN_CONV):
            shifted = pltpu.roll(u, sh, 0)
            head = jnp.where(row8 < sh, pltpu.roll(prev, sh, 0), shifted[:SUBLANES])
            shifted = jnp.concatenate([head, shifted[SUBLANES:]], axis=0)
            y = y + shifted * w[FFN_CONV - 1 - sh:FFN_CONV - sh, :]
        return y

    act = _silu(conv(ua_ref, pa_ref, cwa_ref)) * conv(ub_ref, pb_ref, cwb_ref)
    part = _dot(act.astype(BF16), wd_ref[...])

    @pl.when(k == 0)
    def _():
        acc_scr[...] = part

    @pl.when(k > 0)
    def _():
        acc_scr[...] += part

    @pl.when(k == pl.num_programs(1) - 1)
    def _():
        o_ref[...] = _ln(alpha * x_ref[...] + gate_ref[...] * acc_scr[...]) * g_ref[...] + b_ref[...]


def _ffn_down_call(u, conv_w, wd, layer, x, gate, ln_g, ln_b, rows_per_seq, alpha):
    m, d = x.shape
    d_ff = wd.shape[1]
    tm = min(512, m)
    tk = 512
    nk = d_ff // tk
    sub_per_tile = tm // SUBLANES
    prev_idx = lambda i: jnp.maximum(i * sub_per_tile - 1, 0)
    return pl.pallas_call(
        functools.partial(_ffn_down_kernel, alpha=alpha, rows_per_seq=rows_per_seq),
        grid=(m // tm, nk),
        in_specs=[pl.BlockSpec((tm, tk), lambda i, k: (i, k)),
                  pl.BlockSpec((tm, tk), lambda i, k: (i, nk + k)),
                  pl.BlockSpec((SUBLANES, tk), lambda i, k: (prev_idx(i), k)),
                  pl.BlockSpec((SUBLANES, tk), lambda i, k: (prev_idx(i), nk + k)),
                  pl.BlockSpec((FFN_CONV, tk), lambda i, k: (0, k)),
                  pl.BlockSpec((FFN_CONV, tk), lambda i, k: (0, nk + k)),
                  pl.BlockSpec((None, tk, d), lambda i, k: (layer, k, 0)),
                  pl.BlockSpec((tm, d), lambda i, k: (i, 0)),
                  _rowparam_spec(gate, tm, rows_per_seq),
                  pl.BlockSpec((1, d), lambda i, k: (0, 0)), pl.BlockSpec((1, d), lambda i, k: (0, 0))],
        out_specs=pl.BlockSpec((tm, d), lambda i, k: (i, 0)),
        out_shape=jax.ShapeDtypeStruct((m, d), F32),
        scratch_shapes=[pltpu.VMEM((tm, d), F32)],
        compiler_params=_cparams(("parallel", "arbitrary")),
        name="ffn_down_ln",
    )(u, u, u, u, conv_w, conv_w, wd, x, gate, ln_g.reshape(1, d), ln_b.reshape(1, d))


def _ffn_fused_kernel(x_ref, xp_ref, sh_ref, sc_ref, wua_ref, wub_ref, cwa_ref, cwb_ref, wd_ref, gate_ref, g_ref,
                      b_ref, o_ref, ta_ref, tb_ref, h_scr, acc_scr, *, alpha, rows_per_seq):
    i = pl.program_id(0)
    k = pl.program_id(1)
    tm = x_ref.shape[0]
    halo = xp_ref.shape[0]

    @pl.when(k == 0)
    def _():
        seq_start = (i * tm) % rows_per_seq == 0
        mod = lambda x: _ln(x) * (1.0 + sc_ref[...]) + sh_ref[...]
        h_scr[0:halo, :] = jnp.where(seq_start, 0.0, mod(xp_ref[...])).astype(BF16)
        h_scr[halo:, :] = mod(x_ref[...]).astype(BF16)

    h = h_scr[...]

    def branch(w_ref, cw_ref, t_ref):
        u = _dot(h, w_ref[...])
        cw = cw_ref[...]
        y = u * cw[FFN_CONV - 1:FFN_CONV, :]
        for sh in range(1, FFN_CONV):
            y = y + pltpu.roll(u, sh, 0) * cw[FFN_CONV - 1 - sh:FFN_CONV - sh, :]
        t_ref[...] = u[halo + tm - SUBLANES:, :]
        return y[halo:, :]

    act = _silu(branch(wua_ref, cwa_ref, ta_ref)) * branch(wub_ref, cwb_ref, tb_ref)
    part = _dot(act.astype(BF16), wd_ref[...])

    @pl.when(k == 0)
    def _():
        acc_scr[...] = part

    @pl.when(k > 0)
    def _():
        acc_scr[...] += part

    @pl.when(k == pl.num_programs(1) - 1)
    def _():
        o_ref[...] = _ln(alpha * x_ref[...] + gate_ref[...] * acc_scr[...]) * g_ref[...] + b_ref[...]


def _ffn_fused_call(x, shift, scale, w_up, conv_w, wd, layer, gate, ln_g, ln_b, rows_per_seq, alpha):
    m, d = x.shape
    d_ff = wd.shape[1]
    tm = min(512, m)
    tk = 512
    nk = d_ff // tk
    halo = BF16_ROWS
    assert FFN_CONV - 1 <= halo and tm % halo == 0 and rows_per_seq % tm == 0
    prev_idx = lambda i: jnp.maximum(i * (tm // halo) - 1, 0)
    seqspec = lambda p: _rowparam_spec(p, tm, rows_per_seq)
    tail = jax.ShapeDtypeStruct((m // tm * SUBLANES, d_ff), F32)
    tspec = pl.BlockSpec((SUBLANES, tk), lambda i, k: (i, k))
    return pl.pallas_call(
        functools.partial(_ffn_fused_kernel, alpha=alpha, rows_per_seq=rows_per_seq),
        grid=(m // tm, nk),
        in_specs=[pl.BlockSpec((tm, d), lambda i, k: (i, 0)),
                  pl.BlockSpec((halo, d), lambda i, k: (prev_idx(i), 0)),
                  seqspec(shift), seqspec(scale),
                  pl.BlockSpec((None, d, tk), lambda i, k: (layer, 0, k)),
                  pl.BlockSpec((None, d, tk), lambda i, k: (layer, 0, nk + k)),
                  pl.BlockSpec((FFN_CONV, tk), lambda i, k: (0, k)),
                  pl.BlockSpec((FFN_CONV, tk), lambda i, k: (0, nk + k)),
                  pl.BlockSpec((None, tk, d), lambda i, k: (layer, k, 0)),
                  seqspec(gate),
                  pl.BlockSpec((1, d), lambda i, k: (0, 0)), pl.BlockSpec((1, d), lambda i, k: (0, 0))],
        out_specs=[pl.BlockSpec((tm, d), lambda i, k: (i, 0)), tspec, tspec],
        out_shape=[jax.ShapeDtypeStruct((m, d), F32), tail, tail],
        scratch_shapes=[pltpu.VMEM((halo + tm, d), BF16), pltpu.VMEM((tm, d), F32)],
        compiler_params=_cparams(("parallel", "arbitrary")),
        name="ffn_fused",
    )(x, x, shift, scale, w_up, w_up, conv_w, conv_w, wd, gate, ln_g.reshape(1, d), ln_b.reshape(1, d))


def _permute_w_in(w_in):
    d = w_in.shape[-2]
    sizes = (3 * DN_WIDTH, DN_HEADS, DN_HEADS, DN_WIDTH, NSA_WIDTH) + (KV_WIDTH,) * 6 + (3 * NSA_HEADS, 2 * d)
    offs = [0]
    for s in sizes:
        offs.append(offs[-1] + s)
    part = lambda i: w_in[..., offs[i]:offs[i + 1]].astype(BF16)
    cols = [part(0), part(3), part(12), part(4)] + [part(i) for i in range(5, 11)] + [part(1), part(2), part(11)]
    used = sum(c.shape[-1] for c in cols)
    cols.append(jnp.zeros(w_in.shape[:-1] + (IN_PAD_WIDTH - used,), BF16))
    return jnp.concatenate(cols, axis=-1)


def _pad_groups(x, n_pad):
    b, t, c = x.shape
    return jnp.concatenate([jnp.zeros((b, n_pad, c), x.dtype), x], axis=1).reshape(b * (n_pad + t), c)


def kernel(x_prompt, x_sample, cache_cmp_k, cache_cmp_v, cache_slc_k, cache_slc_v, cache_win_k, cache_win_v, state_dn, state_dn_conv, state_ffn_conv, page_table, c_prompt, c_sample, w_ada, b_ada, w_in, dn_conv_w, dn_a_log, dn_dt_bias, dn_norm_w, cmp_k_w1, cmp_k_pe, cmp_k_w2, cmp_v_w1, cmp_v_pe, cmp_v_w2, w_branch_a, w_branch_b, w_out, ln1_g, ln1_b, ffn_w_up, ffn_conv_w, ffn_w_down, ln2_g, ln2_b):
    depth = w_in.shape[0]
    bp, t_len, d = x_prompt.shape
    bd, t_new, _ = x_sample.shape
    n_phys = cache_cmp_k.shape[1]
    win_len = cache_win_k.shape[2]
    d_ff = ffn_w_down.shape[1]
    alpha = (2 * depth) ** 0.25
    grp = SUBLANES
    assert t_new <= grp - (DN_CONV - 1) and t_new >= DN_CONV - 1 and cache_cmp_k.shape[2] == PAGE_ROWS

    xp = x_prompt.reshape(bp * t_len, d)
    xs = x_sample.reshape(bd * t_new, d)
    c_all = jnp.concatenate([c_prompt, c_sample], axis=0)
    c_rows = -(-c_all.shape[0] // SUBLANES) * SUBLANES
    c_all = jnp.pad(c_all, ((0, c_rows - c_all.shape[0]), (0, 0)))

    outs_p = {k: [] for k in ('cmp_k', 'cmp_v', 'slc_k', 'slc_v', 'win_k', 'win_v', 'dn_state', 'dn_conv', 'ffn_conv')}
    outs_s = {k: [] for k in outs_p}

    w_ada_b = w_ada.astype(BF16)
    w_in_b = _permute_w_in(w_in)
    wa, wb, wo = w_branch_a.astype(BF16), w_branch_b.astype(BF16), w_out.astype(BF16)
    w_up, w_dn = ffn_w_up.astype(BF16), ffn_w_down.astype(BF16)
    zero_state = jnp.zeros((1, bp, DN_HEADS, DN_DK, DN_DV), F32)

    for l in range(depth):
        mod = _ada_call(c_all, w_ada_b, l, b_ada[l])
        mod_p = mod[:bp].reshape(bp, 6, 1, d)
        mod_s = mod[bp:bp + bd].reshape(bd, 6, d)
        mp = [mod_p[:, i] for i in range(6)]
        ms = [jnp.repeat(mod_s[:, i], t_new, axis=0) for i in range(6)]

        wpq_k = _pq_weights(cmp_k_w1[l])
        wpq_v = _pq_weights(cmp_v_w1[l])
        pek = _pe_term(cmp_k_pe[l], wpq_k)
        pev = _pe_term(cmp_v_pe[l], wpq_v)
        w2k = cmp_k_w2[l].astype(BF16)
        w2v = cmp_v_w2[l].astype(BF16)

        zp = _ln_mm_call(xp, mp[0], mp[1], w_in_b, l, t_len, "in_proj")
        zp3 = zp.reshape(bp, t_len, -1)
        rows_p = lambda i, t0: zp3[:, t0:, OFF_ROWS + i * KV_WIDTH:OFF_ROWS + (i + 1) * KV_WIDTH].reshape(
            bp, t_len - t0, NSA_KV_HEADS, HEAD_DIM)
        beta, gc, gl = _gdn_gates_call(zp, OFF_TAIL, dn_a_log[l], dn_dt_bias[l], DN_CHUNK, 0)
        gcrow = gc.T.reshape(DN_HEADS, 1, bp * t_len)
        ya_p, dn_state_p = _gdn_call(zp, OFF_QKV, dn_conv_w[l], zp, OFF_DNGATE, dn_norm_w[l], beta, gc, gl, gcrow,
                                     zero_state, 0, rows_per_seq=t_len, chunk=GDN_MATMUL_ROWS,
                                     sub=math.gcd(DN_CHUNK, t_len), per_chunk_state=False, n_pad=0)
        chunk_w = CMP_STRIDE * KV_WIDTH
        pqk = _pq_call(zp[:, OFF_ROWS:OFF_ROWS + KV_WIDTH].reshape(-1, chunk_w), wpq_k)
        pqv = _pq_call(zp[:, OFF_ROWS + KV_WIDTH:OFF_ROWS + 2 * KV_WIDTH].reshape(-1, chunk_w), wpq_v)
        yb_p = _nsa_prompt_call(zp, pqk, pqv, pek, pev, w2k, w2v, bp, t_len)
        x1p = _merge_call(ya_p, yb_p, zp, wa, wb, wo, l, xp, mp[2], ln1_g[l], ln1_b[l], t_len, alpha)
        xp, tail_a, tail_b = _ffn_fused_call(x1p, mp[3], mp[4], w_up, ffn_conv_w[l], w_dn, l, mp[5], ln2_g[l],
                                             ln2_b[l], t_len, alpha)
        tiles_per_seq = tail_a.shape[0] // SUBLANES // bp
        last_rows = lambda t: t.reshape(bp, tiles_per_seq, SUBLANES, d_ff)[:, -1, SUBLANES - (FFN_CONV - 1):]

        keep = min(WINDOW, t_len)
        for i, name in enumerate(('cmp_k', 'cmp_v', 'slc_k', 'slc_v')):
            outs_p[name].append(rows_p(i, 0))
        outs_p['win_k'].append(rows_p(4, t_len - keep))
        outs_p['win_v'].append(rows_p(5, t_len - keep))
        outs_p['dn_state'].append(dn_state_p)
        outs_p['dn_conv'].append(zp3[:, t_len - (DN_CONV - 1):, OFF_QKV:OFF_QKV + 3 * DN_WIDTH])
        outs_p['ffn_conv'].append(jnp.concatenate([last_rows(tail_a), last_rows(tail_b)], axis=-1))

        zs = _ln_mm_call(xs, ms[0], ms[1], w_in_b, l, t_new, "in_proj")
        zs3 = zs.reshape(bd, t_new, -1)
        rows_s = zs3[:, :, OFF_ROWS:OFF_ROWS + 6 * KV_WIDTH].reshape(bd, t_new, 6, NSA_KV_HEADS, HEAD_DIM)
        n_pad = grp - t_new
        qkv_g = jnp.concatenate([jnp.zeros((bd, n_pad - (DN_CONV - 1), 3 * DN_WIDTH), F32), state_dn_conv[l],
                                 zs3[:, :, OFF_QKV:OFF_QKV + 3 * DN_WIDTH]], axis=1).reshape(bd * grp, -1)
        zs_g = _pad_groups(jnp.concatenate([zs3[:, :, OFF_DNGATE:OFF_DNGATE + DN_WIDTH],
                                            zs3[:, :, OFF_TAIL:OFF_TAIL + LANES]], axis=-1), n_pad)
        beta, gc, gl = _gdn_gates_call(zs_g, DN_WIDTH, dn_a_log[l], dn_dt_bias[l], grp, n_pad)
        gcrow = gc.T.reshape(DN_HEADS, 1, bd * grp)
        ya_g, dn_state_s = _gdn_call(qkv_g, 0, dn_conv_w[l], zs_g, 0, dn_norm_w[l], beta, gc, gl, gcrow, state_dn, l,
                                     rows_per_seq=grp, chunk=GDN_MATMUL_ROWS, sub=grp, per_chunk_state=True, n_pad=n_pad)
        ya_s = ya_g.reshape(bd, grp, DN_WIDTH)[:, n_pad:].reshape(bd * t_new, DN_WIDTH)
        pq_pool_k = _pq_pool_call(cache_cmp_k, l, wpq_k)
        pq_pool_v = _pq_pool_call(cache_cmp_v, l, wpq_v)
        new_chunk = lambda i: jnp.pad(zs3[:, :, OFF_ROWS + i * KV_WIDTH:OFF_ROWS + (i + 1) * KV_WIDTH],
                                      ((0, 0), (0, CMP_STRIDE - t_new), (0, 0))).reshape(bd, chunk_w)
        pq_new_k = _pq_call(new_chunk(0), wpq_k).reshape(bd, 1, -1)
        pq_new_v = _pq_call(new_chunk(1), wpq_v).reshape(bd, 1, -1)
        yb_s = _nsa_sample_call(page_table, zs3, pq_new_k, pq_new_v, pek, pev, w2k, w2v, cache_win_k, cache_win_v,
                                pq_pool_k, pq_pool_v, cache_slc_k, cache_slc_v, l).reshape(bd * t_new, NSA_WIDTH)
        x1s = _merge_call(ya_s, yb_s, zs, wa, wb, wo, l, xs, ms[2], ln1_g[l], ln1_b[l], t_new, alpha)
        us = _ln_mm_call(x1s, ms[3], ms[4], w_up, l, t_new, "ffn_up")
        us3 = us.reshape(bd, t_new, -1)
        u_g = jnp.concatenate([jnp.zeros((bd, n_pad - (FFN_CONV - 1), 2 * d_ff), F32), state_ffn_conv[l], us3],
                              axis=1).reshape(bd * grp, -1)
        x1_g = _pad_groups(x1s.reshape(bd, t_new, d), n_pad)
        gate2_g = jnp.repeat(mod_s[:, 5], grp, axis=0)
        xs_g = _ffn_down_call(u_g, ffn_conv_w[l], w_dn, l, x1_g, gate2_g, ln2_g[l], ln2_b[l], grp, alpha)
        xs = xs_g.reshape(bd, grp, d)[:, n_pad:].reshape(bd * t_new, d)

        for i, name in enumerate(('cmp_k', 'cmp_v', 'slc_k', 'slc_v')):
            outs_s[name].append(rows_s[:, :, i])
        outs_s['win_k'].append(rows_s[:, :, 4])
        outs_s['win_v'].append(rows_s[:, :, 5])
        outs_s['dn_state'].append(dn_state_s)
        outs_s['dn_conv'].append(jnp.concatenate([state_dn_conv[l], zs3[:, :, OFF_QKV:OFF_QKV + 3 * DN_WIDTH]], axis=1)[:, t_new:])
        outs_s['ffn_conv'].append(jnp.concatenate([state_ffn_conv[l], us3], axis=1)[:, t_new:])

    order = ('cmp_k', 'cmp_v', 'slc_k', 'slc_v', 'win_k', 'win_v', 'dn_state', 'dn_conv', 'ffn_conv')
    stacked_s = {k: jnp.stack(outs_s[k]) for k in order}
    for name, cache in (('win_k', cache_win_k), ('win_v', cache_win_v)):
        stacked_s[name] = jnp.concatenate([cache, stacked_s[name]], axis=2)[:, :, t_new:]
    return ((xp.reshape(bp, t_len, d), xs.reshape(bd, t_new, d))
            + tuple(jnp.stack(outs_p[k]) for k in order) + tuple(stacked_s[k] for k in order))
```

```python
import functools
import math

import jax
import jax.numpy as jnp
from jax import lax
from jax.experimental import pallas as pl
from jax.experimental.pallas import tpu as pltpu

F32 = jnp.float32
BF16 = jnp.bfloat16
HIGHEST = lax.Precision.HIGHEST

DN_HEADS = 8
DN_DK = 128
DN_DV = 128
DN_WIDTH = DN_HEADS * DN_DV
DN_CONV = 4
DN_CHUNK = 64
NSA_HEADS = 8
NSA_KV_HEADS = 2
NSA_GROUP = NSA_HEADS // NSA_KV_HEADS
HEAD_DIM = 128
NSA_WIDTH = NSA_HEADS * HEAD_DIM
KV_WIDTH = NSA_KV_HEADS * HEAD_DIM
CMP_LEN = 32
CMP_STRIDE = 16
SEL_BLOCK = 64
N_SEL = 16
WINDOW = 512
FFN_CONV = 3
LN_EPS = 1e-5
RMS_EPS = 1e-6
NEG_INF = -1e30
FORCE_SCORE = 1e4

LANES = 128
SUBLANES = 8
BF16_ROWS = 16
VMEM_LIMIT_BYTES = 56 * 1024 * 1024
MXU_DIM = 256
GDN_MATMUL_ROWS = MXU_DIM
GDN_HEADS_PER_STEP = 2

OFF_QKV = 0
OFF_DNGATE = 3 * DN_WIDTH
OFF_MERGE = OFF_DNGATE + DN_WIDTH
OFF_NQ = OFF_MERGE + 2 * 2048
OFF_ROWS = OFF_NQ + NSA_WIDTH
OFF_TAIL = OFF_ROWS + 6 * KV_WIDTH
TAIL_BETA = 0
TAIL_A = DN_HEADS
TAIL_NSAG = 2 * DN_HEADS
IN_PAD_WIDTH = 11264


def _cparams(sem):
    return pltpu.CompilerParams(dimension_semantics=sem, vmem_limit_bytes=VMEM_LIMIT_BYTES)


def _dot(a, b, precision=None):
    return jnp.dot(a, b, preferred_element_type=F32, precision=precision)


def _dot_nt(a, b, precision=None):
    return lax.dot_general(a, b, (((1,), (1,)), ((), ())), preferred_element_type=F32, precision=precision)


def _dot_tn(a, b, precision=None):
    return lax.dot_general(a, b, (((0,), (0,)), ((), ())), preferred_element_type=F32, precision=precision)


def _split_bf16(x):
    hi = x.astype(BF16)
    return hi, (x - hi.astype(F32)).astype(BF16)


def _dot_split(a, b):
    (ah, al), (bh, bl) = a, b
    return _dot(ah, bh) + (_dot(al, bh) + _dot(ah, bl))


def _sigmoid(x):
    return 1.0 / (1.0 + jnp.exp(-x))


def _silu(x):
    return x * _sigmoid(x)


def _ln(x):
    mu = jnp.mean(x, axis=-1, keepdims=True)
    xc = x - mu
    var = jnp.mean(xc * xc, axis=-1, keepdims=True)
    return xc * lax.rsqrt(var + LN_EPS)


def _rowparam_spec(p, tm, rows_per_seq):
    if p.ndim == 2:
        return pl.BlockSpec((tm, p.shape[1]), lambda i, *_: (i, 0))
    blocks_per_seq = rows_per_seq // tm
    return pl.BlockSpec((None, 1, p.shape[2]), lambda i, *_: (i // blocks_per_seq, 0, 0))


def _ada_kernel(c_ref, w_ref, b_ref, o_ref):
    h = _silu(c_ref[...]).astype(BF16)
    o_ref[...] = _dot(h, w_ref[...].astype(BF16)) + b_ref[...]


def _ada_call(c, w, layer, b):
    m, k = c.shape
    n = w.shape[2]
    tn = 1024
    return pl.pallas_call(
        _ada_kernel,
        grid=(n // tn,),
        in_specs=[pl.BlockSpec((m, k), lambda j: (0, 0)),
                  pl.BlockSpec((None, k, tn), lambda j: (layer, 0, j)),
                  pl.BlockSpec((1, tn), lambda j: (0, j))],
        out_specs=pl.BlockSpec((m, tn), lambda j: (0, j)),
        out_shape=jax.ShapeDtypeStruct((m, n), F32),
        compiler_params=_cparams(("parallel",)),
        name="ada_mod",
    )(c, w, b.reshape(1, n))


def _ln_mm_kernel(x_ref, sh_ref, sc_ref, w_ref, o_ref, h_scr):
    @pl.when(pl.program_id(1) == 0)
    def _():
        h = _ln(x_ref[...]) * (1.0 + sc_ref[...]) + sh_ref[...]
        h_scr[...] = h.astype(BF16)

    o_ref[...] = _dot(h_scr[...], w_ref[...])


def _ln_mm_call(x, shift, scale, w_bf16, layer, rows_per_seq, name):
    m, k = x.shape
    n = w_bf16.shape[2]
    tm = min(1024, m)
    tn = 512
    return pl.pallas_call(
        _ln_mm_kernel,
        grid=(m // tm, n // tn),
        in_specs=[pl.BlockSpec((tm, k), lambda i, j: (i, 0)),
                  _rowparam_spec(shift, tm, rows_per_seq),
                  _rowparam_spec(scale, tm, rows_per_seq),
                  pl.BlockSpec((None, k, tn), lambda i, j: (layer, 0, j))],
        out_specs=pl.BlockSpec((tm, tn), lambda i, j: (i, j)),
        out_shape=jax.ShapeDtypeStruct((m, n), F32),
        scratch_shapes=[pltpu.VMEM((tm, k), BF16)],
        compiler_params=_cparams(("parallel", "arbitrary")),
        name=name,
    )(x, shift, scale, w_bf16)


def _softplus(x):
    return jnp.maximum(x, 0.0) + jnp.log1p(jnp.exp(-jnp.abs(x)))


def _gdn_gates_kernel(tail_ref, alog_ref, dtb_ref, beta_ref, gc_ref, gl_ref, *, group, n_pad):
    r = tail_ref.shape[0]
    tail = tail_ref[...]
    braw = tail[:, TAIL_BETA:TAIL_BETA + DN_HEADS]
    araw = tail[:, TAIL_A:TAIL_A + DN_HEADS]
    row = lax.broadcasted_iota(jnp.int32, (r, DN_HEADS), 0)
    real = (row % group) >= n_pad
    g = -jnp.exp(alog_ref[...]) * _softplus(araw + dtb_ref[...])
    g = jnp.where(real, g, 0.0)
    beta = jnp.where(real, _sigmoid(braw), 0.0)
    ri = lax.broadcasted_iota(jnp.int32, (r, r), 0)
    ci = lax.broadcasted_iota(jnp.int32, (r, r), 1)
    same = (ri // group) == (ci // group)
    tri = jnp.where(same & (ri >= ci), 1.0, 0.0).astype(F32)
    ones = jnp.where(same, 1.0, 0.0).astype(F32)
    beta_ref[...] = beta
    gc_ref[...] = _dot(tri, g, HIGHEST)
    gl_ref[...] = _dot(ones, g, HIGHEST)


def _gdn_gates_call(z, tail_off, a_log, dt_bias, group, n_pad):
    m = z.shape[0]
    r = min(512, m)
    tail_blk = tail_off // LANES
    out = jax.ShapeDtypeStruct((m, DN_HEADS), F32)
    ospec = pl.BlockSpec((r, DN_HEADS), lambda i: (i, 0))
    return pl.pallas_call(
        functools.partial(_gdn_gates_kernel, group=group, n_pad=n_pad),
        grid=(m // r,),
        in_specs=[pl.BlockSpec((r, LANES), lambda i: (i, tail_blk)),
                  pl.BlockSpec((1, DN_HEADS), lambda i: (0, 0)),
                  pl.BlockSpec((1, DN_HEADS), lambda i: (0, 0))],
        out_specs=[ospec, ospec, ospec],
        out_shape=[out, out, out],
        compiler_params=_cparams(("parallel",)),
        name="gdn_gates",
    )(z, a_log.reshape(1, DN_HEADS), dt_bias.reshape(1, DN_HEADS))


def _unit_lower_inverses(lowers, n_sq):
    c = lowers[0].shape[0]
    eye = (lax.broadcasted_iota(jnp.int32, (c, c), 0) == lax.broadcasted_iota(jnp.int32, (c, c), 1)).astype(F32)
    p = [-x for x in lowers]
    t = [eye + x for x in p]
    ps = [_split_bf16(x) for x in p]
    for _ in range(n_sq):
        p = [_dot_split(x, x) for x in ps]
        ps = [_split_bf16(x) for x in p]
        t = [a + _dot_split(_split_bf16(a), b) for a, b in zip(t, ps)]
    return t


def _gdn_kernel(xq_ref, xk_ref, xv_ref, wq_ref, wk_ref, wv_ref, gate_ref, nw_ref, beta_ref, gc_ref, gl_ref,
                gcrow_ref, s_in_ref, o_ref, s_out_ref, xs_scr, yq_scr, yk_scr, yv_scr,
                *, chunk, sub, per_chunk_state, n_pad):
    c = chunk
    r = xq_ref.shape[0]
    nchunk = r // c
    n_sub = c // sub
    n_heads = xq_ref.shape[1] // DN_DK
    head0 = pl.program_id(1) * n_heads
    first = pl.program_id(2) == 0

    @pl.when(first)
    def _():
        s_out_ref[...] = s_in_ref[...]
        xs_scr[:, 0:SUBLANES, :] = jnp.zeros((3, SUBLANES, n_heads * DN_DK), F32)

    row = lax.broadcasted_iota(jnp.int32, (r, 1), 0)
    real = (row % sub) >= n_pad
    for idx, (x_ref, w_ref, y_scr) in enumerate(((xq_ref, wq_ref, yq_scr), (xk_ref, wk_ref, yk_scr),
                                                 (xv_ref, wv_ref, yv_scr))):
        xs_scr[idx, SUBLANES:SUBLANES + r, :] = x_ref[...]
        w = w_ref[...]
        y = jnp.zeros((r, n_heads * DN_DK), F32)
        for j in range(DN_CONV):
            off = SUBLANES - (DN_CONV - 1) + j
            y = y + xs_scr[idx, off:off + r, :] * w[j:j + 1, :]
        xs_scr[idx, 0:SUBLANES, :] = xs_scr[idx, r:r + SUBLANES, :]
        y_scr[...] = jnp.where(real, _silu(y), 0.0)

    lane = lax.broadcasted_iota(jnp.int32, (r, DN_HEADS), 1)

    def col(ref, head):
        return jnp.sum(jnp.where(lane == head, ref[...], 0.0), axis=1, keepdims=True)

    cols = [(col(beta_ref, head0 + hh), col(gc_ref, head0 + hh), col(gl_ref, head0 + hh)) for hh in range(n_heads)]

    ri = lax.broadcasted_iota(jnp.int32, (c, c), 0)
    ci = lax.broadcasted_iota(jnp.int32, (c, c), 1)
    same = (ri // sub) == (ci // sub)
    causal = same & (ri >= ci)
    strict = same & (ri > ci)
    n_sq = int(math.log2(sub)) - 1
    nw = nw_ref[...]

    lanes = [(ch, hh) for ch in range(nchunk) for hh in range(n_heads)]
    rows_of = lambda x, ch: x[ch * c:(ch + 1) * c, :]
    q = [yq_scr[ch * c:(ch + 1) * c, hh * DN_DK:(hh + 1) * DN_DK] for ch, hh in lanes]
    k = [yk_scr[ch * c:(ch + 1) * c, hh * DN_DK:(hh + 1) * DN_DK] for ch, hh in lanes]
    v = [yv_scr[ch * c:(ch + 1) * c, hh * DN_DV:(hh + 1) * DN_DV] for ch, hh in lanes]
    q = [x * lax.rsqrt(jnp.sum(x * x, axis=-1, keepdims=True) + RMS_EPS) * (DN_DK ** -0.5) for x in q]
    k = [x * lax.rsqrt(jnp.sum(x * x, axis=-1, keepdims=True) + RMS_EPS) for x in k]
    beta = [rows_of(cols[hh][0], ch) for ch, hh in lanes]
    gc = [rows_of(cols[hh][1], ch) for ch, hh in lanes]
    gl = [rows_of(cols[hh][2], ch) for ch, hh in lanes]
    gcrow = [gcrow_ref[hh:hh + 1, ch * c:(ch + 1) * c] for ch, hh in lanes]
    decay = [jnp.where(causal, jnp.exp(jnp.where(causal, g - gr, 0.0)), 0.0) for g, gr in zip(gc, gcrow)]
    kb = [x * b for x, b in zip(k, beta)]
    kbf = [x.astype(BF16) for x in k]
    lower = [jnp.where(strict, _dot_nt(a.astype(BF16), b) * d, 0.0) for a, b, d in zip(kb, kbf, decay)]
    tmat = _unit_lower_inverses(lower, n_sq)
    eg = [jnp.exp(g) for g in gc]
    uw = [_dot(t.astype(BF16), jnp.concatenate([x * b, y * e], axis=1).astype(BF16))
          for t, x, b, y, e in zip(tmat, v, beta, kb, eg)]
    u = [x[:, :DN_DV] for x in uw]
    wb = [x[:, DN_DV:].astype(BF16) for x in uw]
    intra = [jnp.where(causal, _dot_nt(a.astype(BF16), b) * d, 0.0).astype(BF16) for a, b, d in zip(q, kbf, decay)]
    qe = [(a * e).astype(BF16) for a, e in zip(q, eg)]
    kd = [(a * jnp.exp(gl_ - g)).astype(BF16) for a, gl_, g in zip(k, gl, gc)]

    heads = range(n_heads)
    for ch in range(nchunk):
        lane_of = [ch * n_heads + hh for hh in heads]
        v_new_parts = [[] for _ in heads]
        o_parts = [[] for _ in heads]
        for s in range(n_sub):
            a0 = s * sub
            sidx = (ch * n_sub + s) if per_chunk_state else 0
            state = [s_out_ref[sidx, hh] for hh in heads]
            sb = [x.astype(BF16) for x in state]
            v_new = [u[i][a0:a0 + sub, :] - _dot(wb[i][a0:a0 + sub, :], b) for i, b in zip(lane_of, sb)]
            for hh, i in enumerate(lane_of):
                o_parts[hh].append(_dot(qe[i][a0:a0 + sub, :], sb[hh]))
            for hh, i in enumerate(lane_of):
                s_out_ref[sidx, hh] = (state[hh] * jnp.exp(gl[i][a0:a0 + 1, :])
                                       + _dot_tn(kd[i][a0:a0 + sub, :], v_new[hh].astype(BF16)))
                v_new_parts[hh].append(v_new[hh])
        for hh, i in enumerate(lane_of):
            v_all = v_new_parts[hh][0] if n_sub == 1 else jnp.concatenate(v_new_parts[hh], axis=0)
            o_state = o_parts[hh][0] if n_sub == 1 else jnp.concatenate(o_parts[hh], axis=0)
            o = o_state + _dot(intra[i], v_all.astype(BF16))
            o = o * lax.rsqrt(jnp.mean(o * o, axis=-1, keepdims=True) + RMS_EPS) * nw
            r0, h0 = ch * c, hh * DN_DV
            o_ref[r0:r0 + c, h0:h0 + DN_DV] = o * _silu(gate_ref[r0:r0 + c, h0:h0 + DN_DV])


def _gdn_call(x, x_col0, conv_w, gate, gate_col0, norm_w, beta, gc, gl, gcrow, s_in, layer, *, rows_per_seq, chunk,
              sub, per_chunk_state, n_pad):
    m = x.shape[0]
    r = min(512, rows_per_seq) if not per_chunk_state else min(512, m)
    chunk = min(chunk, r)
    n_seq = m // rows_per_seq if not per_chunk_state else m // r
    nt = rows_per_seq // r if not per_chunk_state else 1
    n_state_blk = r // sub if per_chunk_state else 1
    hp = GDN_HEADS_PER_STEP
    wide = hp * DN_DK
    xb = x_col0 // wide
    gb = gate_col0 // wide
    hw = DN_WIDTH // wide
    assert x_col0 % wide == 0 and gate_col0 % wide == 0 and DN_HEADS % hp == 0

    def rowblk(b, h, t):
        return b * nt + t

    xspec = lambda off: pl.BlockSpec((r, wide), lambda b, h, t: (rowblk(b, h, t), xb + off + h))
    wspec = lambda off: pl.BlockSpec((DN_CONV, wide), lambda b, h, t: (0, off + h))
    colspec = pl.BlockSpec((r, DN_HEADS), lambda b, h, t: (rowblk(b, h, t), 0))
    sspec = pl.BlockSpec((n_state_blk, hp, DN_DK, DN_DV), lambda b, h, t: (b, h, 0, 0))
    return pl.pallas_call(
        functools.partial(_gdn_kernel, chunk=chunk, sub=sub, per_chunk_state=per_chunk_state, n_pad=n_pad),
        grid=(n_seq, DN_HEADS // hp, nt),
        in_specs=[xspec(0), xspec(hw), xspec(2 * hw), wspec(0), wspec(hw), wspec(2 * hw),
                  pl.BlockSpec((r, wide), lambda b, h, t: (rowblk(b, h, t), gb + h)),
                  pl.BlockSpec((1, DN_DV), lambda b, h, t: (0, 0)),
                  colspec, colspec, colspec,
                  pl.BlockSpec((None, hp, r), lambda b, h, t: (h, 0, rowblk(b, h, t))),
                  pl.BlockSpec((None, n_state_blk, hp, DN_DK, DN_DV), lambda b, h, t: (layer, b, h, 0, 0))],
        out_specs=[pl.BlockSpec((r, wide), lambda b, h, t: (rowblk(b, h, t), h)), sspec],
        out_shape=[jax.ShapeDtypeStruct((m, DN_WIDTH), F32), jax.ShapeDtypeStruct(s_in.shape[1:], F32)],
        scratch_shapes=[pltpu.VMEM((3, r + SUBLANES, wide), F32), pltpu.VMEM((r, wide), F32),
                        pltpu.VMEM((r, wide), F32), pltpu.VMEM((r, wide), F32)],
        compiler_params=_cparams(("parallel", "parallel", "arbitrary")),
        name="gdn_delta",
    )(x, x, x, conv_w, conv_w, conv_w, gate, norm_w.reshape(1, DN_DV), beta, gc, gl,
      gcrow.reshape(DN_HEADS // hp, hp, m), s_in)


def _pq_kernel(x_ref, w_ref, o_ref):
    for kv in range(NSA_KV_HEADS):
        acc = jnp.zeros((x_ref.shape[0], 2 * HEAD_DIM), F32)
        for j in range(CMP_STRIDE):
            c0 = j * KV_WIDTH + kv * HEAD_DIM
            acc = acc + _dot(x_ref[:, c0:c0 + HEAD_DIM].astype(BF16), w_ref[j])
        o_ref[:, kv * 2 * HEAD_DIM:(kv + 1) * 2 * HEAD_DIM] = acc


def _pq_weights(w1):
    return jnp.concatenate([w1[:CMP_STRIDE], w1[CMP_STRIDE:]], axis=-1).astype(BF16)


def _pq_call(chunks, wpq):
    nc, width = chunks.shape
    tc = 512 if nc % 512 == 0 else nc
    return pl.pallas_call(
        _pq_kernel,
        grid=(nc // tc,),
        in_specs=[pl.BlockSpec((tc, width), lambda i: (i, 0)),
                  pl.BlockSpec(wpq.shape, lambda i: (0, 0, 0))],
        out_specs=pl.BlockSpec((tc, 4 * HEAD_DIM), lambda i: (i, 0)),
        out_shape=jax.ShapeDtypeStruct((nc, 4 * HEAD_DIM), F32),
        compiler_params=_cparams(("parallel",)),
        name="nsa_compress_pq",
    )(chunks, wpq)


def _pq_pool_kernel(*refs):
    x_refs, w_ref, o_ref = refs[:CMP_STRIDE], refs[CMP_STRIDE], refs[CMP_STRIDE + 1]
    rows = o_ref.shape[1]
    acc = jnp.zeros((rows, 2 * HEAD_DIM), F32)
    for j in range(CMP_STRIDE):
        acc = acc + _dot(x_refs[j][...].reshape(rows, HEAD_DIM).astype(BF16), w_ref[j])
    o_ref[0] = acc[:, :HEAD_DIM]
    o_ref[1] = acc[:, HEAD_DIM:]


def _pq_pool_call(pool, layer, wpq):
    depth, n_phys, page, n_kv, hd = pool.shape
    nc = n_phys * (page // CMP_STRIDE)
    chunks = pool.reshape(depth, nc, CMP_STRIDE, n_kv, hd)
    tc = 512 if nc % 512 == 0 else nc
    xspec = lambda j: pl.BlockSpec((None, tc, None, n_kv, hd), lambda i: (layer, i, j, 0, 0))
    return pl.pallas_call(
        _pq_pool_kernel,
        grid=(nc // tc,),
        in_specs=[xspec(j) for j in range(CMP_STRIDE)] + [pl.BlockSpec(wpq.shape, lambda i: (0, 0, 0))],
        out_specs=pl.BlockSpec((2, tc * n_kv, HEAD_DIM), lambda i: (0, i, 0)),
        out_shape=jax.ShapeDtypeStruct((2, nc * n_kv, HEAD_DIM), F32),
        compiler_params=_cparams(("parallel",)),
        name="nsa_compress_pool",
    )(*([chunks] * CMP_STRIDE), wpq)


def _pe_term(pe, wpq):
    z = jnp.zeros((CMP_STRIDE, HEAD_DIM), F32)
    rows = jnp.stack([jnp.concatenate([pe[:CMP_STRIDE], z], axis=1).reshape(-1),
                      jnp.concatenate([pe[CMP_STRIDE:], z], axis=1).reshape(-1)])
    rows = jnp.concatenate([rows, jnp.zeros((SUBLANES - 2, rows.shape[1]), F32)], axis=0)
    pq = _pq_call(rows, wpq)
    return (pq[0, :HEAD_DIM] + pq[1, HEAD_DIM:2 * HEAD_DIM]).reshape(1, HEAD_DIM)


def _masked_softmax(s, valid):
    s = jnp.where(valid, s, NEG_INF)
    m = jnp.max(s, axis=-1, keepdims=True)
    e = jnp.exp(s - m)
    p = e / jnp.sum(e, axis=-1, keepdims=True)
    return jnp.where(valid, p, 0.0)


def _overlap_matrix(n_cmp, n_blk, transposed=False):
    shape, n_dim, j_dim = ((n_blk, n_cmp), 1, 0) if transposed else ((n_cmp, n_blk), 0, 1)
    n = lax.broadcasted_iota(jnp.int32, shape, n_dim) * CMP_STRIDE
    j = lax.broadcasted_iota(jnp.int32, shape, j_dim) * SEL_BLOCK
    ov = jnp.maximum(jnp.minimum(n + CMP_LEN, j + SEL_BLOCK) - jnp.maximum(n, j), 0)
    return ov.astype(F32) / CMP_LEN


AUG_HI = SEL_BLOCK
AUG_LO = SEL_BLOCK + 1


def _key_aug(pos, with_blocks):
    lane = lax.broadcasted_iota(jnp.int32, pos.shape, 1)
    hi = pos // SEL_BLOCK
    aug = jnp.where(lane == AUG_HI, hi, jnp.where(lane == AUG_LO, pos % SEL_BLOCK, 0))
    if with_blocks:
        aug = jnp.where(lane < SEL_BLOCK, jnp.where(hi == lane, 1, 0), aug)
    return aug.astype(F32).astype(BF16)


def _select_bias_t(imp_t, tpos_row):
    n_blk, t = imp_t.shape
    j = lax.broadcasted_iota(jnp.int32, (n_blk, t), 0)
    q_blk = tpos_row // SEL_BLOCK
    forced = (j == 0) | (j == q_blk) | (j == q_blk - 1)
    score = jnp.where(forced, FORCE_SCORE, jnp.where(j <= q_blk, imp_t, -1.0))
    rank = jnp.zeros((n_blk, t), F32)
    for jj in range(n_blk):
        sj = score[jj:jj + 1, :]
        ahead = (sj > score) | ((sj == score) & (j > jj))
        rank = rank + jnp.where(ahead, 1.0, 0.0)
    return jnp.where(rank < N_SEL, 0.0, NEG_INF)


def _select_bias(imp, tpos):
    t, n_blk = imp.shape
    j = lax.broadcasted_iota(jnp.int32, (t, n_blk), 1)
    q_blk = tpos // SEL_BLOCK
    forced = (j == 0) | (j == q_blk) | (j == q_blk - 1)
    score = jnp.where(forced, FORCE_SCORE, jnp.where(j <= q_blk, imp, -1.0))
    rank = jnp.zeros((t, n_blk), F32)
    for jj in range(n_blk):
        sj = score[:, jj:jj + 1]
        ahead = (sj > score) | ((sj == score) & (j > jj))
        rank = rank + jnp.where(ahead, 1.0, 0.0)
    return jnp.where(rank < N_SEL, 0.0, NEG_INF)


def _select_bias_few(imp, tpos):
    t, n_blk = imp.shape
    j = lax.broadcasted_iota(jnp.int32, (t, n_blk), 1)
    q_blk = tpos // SEL_BLOCK
    forced = (j == 0) | (j == q_blk) | (j == q_blk - 1)
    score = jnp.where(forced, FORCE_SCORE, jnp.where(j <= q_blk, imp, -1.0))
    assert t <= SUBLANES and n_blk <= LANES
    tile = jnp.concatenate([score, jnp.zeros((t, LANES - n_blk), F32)], axis=1)
    tile = jnp.concatenate([tile, jnp.zeros((SUBLANES - t, LANES), F32)], axis=0)
    score_t = tile.T[:n_blk, :]
    jj = lax.broadcasted_iota(jnp.int32, (n_blk, n_blk), 0)
    jl = lax.broadcasted_iota(jnp.int32, (n_blk, n_blk), 1)
    ranks = []
    for i in range(t):
        other = score_t[:, i:i + 1]
        mine = score[i:i + 1, :]
        ahead = (other > mine) | ((other == mine) & (jj < jl))
        ranks.append(jnp.sum(jnp.where(ahead, 1.0, 0.0), axis=0, keepdims=True))
    rank = jnp.concatenate(ranks, axis=0)
    return jnp.where(rank < N_SEL, 0.0, NEG_INF)


def _block_onehot(n_blk, k0, tk):
    j = lax.broadcasted_iota(jnp.int32, (n_blk, tk), 0)
    s = lax.broadcasted_iota(jnp.int32, (n_blk, tk), 1) + k0
    return jnp.where(s // SEL_BLOCK == j, 1.0, 0.0).astype(BF16)


def _compressed_tokens(hid, w2_ref):
    return _dot(_silu(hid).astype(BF16), w2_ref[...])


def _cmp_branch(q4b, cmpk, cmpv, tpos4, slope4, tq):
    n_cmp = cmpk.shape[0]
    scale = HEAD_DIM ** -0.5
    s = _dot_nt(q4b, cmpk.astype(BF16)) * scale
    c_end = lax.broadcasted_iota(jnp.int32, (1, n_cmp), 1) * CMP_STRIDE + (CMP_LEN - 1)
    dist = tpos4 - c_end
    valid = dist >= 0
    s = s - slope4 * dist.astype(F32)
    p = _masked_softmax(s, valid)
    o = _dot(p.astype(BF16), cmpv.astype(BF16))
    psum = p[0:tq]
    for g in range(1, NSA_GROUP):
        psum = psum + p[g * tq:(g + 1) * tq]
    return o, psum


def _slopes(kv, tq):
    g = lax.broadcasted_iota(jnp.int32, (NSA_GROUP * tq, 1), 0) // tq
    head = (g + kv * NSA_GROUP + 1).astype(F32)
    return jnp.exp(head * (-(8.0 / NSA_HEADS) * math.log(2.0)))


def _tail_col(tail, idx):
    lane = lax.broadcasted_iota(jnp.int32, tail.shape, 1)
    return jnp.sum(jnp.where(lane == idx, tail, 0.0), axis=1, keepdims=True)


def _combine_branches(tail, kv, o_cmp, o_slc, o_win, tq):
    outs = []
    for g in range(NSA_GROUP):
        head = kv * NSA_GROUP + g
        acc = jnp.zeros((tq, HEAD_DIM), F32)
        for br, o in enumerate((o_cmp, o_slc, o_win)):
            gate = _sigmoid(_tail_col(tail, TAIL_NSAG + br * NSA_HEADS + head))
            acc = acc + gate * o[g * tq:(g + 1) * tq]
        outs.append(acc)
    return outs


def _nsa_prompt_kernel(q_ref, pqk_ref, pqv_ref, pek_ref, pev_ref, w2k_ref, w2v_ref, sk_ref, sv_ref, wk_ref, wv_ref,
                       tail_ref, o_ref, ccat_scr, cmpv_scr, kcat_scr, sv_scr, wcat_scr, wv_scr, *, tk):
    tq = q_ref.shape[0]
    t_len = sk_ref.shape[0]
    n_cmp = pqk_ref.shape[0]
    n_blk = t_len // SEL_BLOCK
    kv = pl.program_id(1)
    i = pl.program_id(2)
    t0 = i * tq
    scale = HEAD_DIM ** -0.5

    @pl.when(i == 0)
    def _():
        kaug = _key_aug(lax.broadcasted_iota(jnp.int32, (t_len, LANES), 0), True)
        for cat_scr, k_ref, vb_scr, v_ref in ((kcat_scr, sk_ref, sv_scr, sv_ref), (wcat_scr, wk_ref, wv_scr, wv_ref)):
            cat_scr[:, 0:HEAD_DIM] = k_ref[...].astype(BF16)
            cat_scr[:, HEAD_DIM:] = kaug
            vb_scr[...] = v_ref[...].astype(BF16)
        tokens = []
        for pq_ref, pe_ref, w2_ref in ((pqk_ref, pek_ref, w2k_ref), (pqv_ref, pev_ref, w2v_ref)):
            p = pq_ref[:, 0:HEAD_DIM]
            qn = pltpu.roll(pq_ref[:, HEAD_DIM:2 * HEAD_DIM], n_cmp - 1, 0)
            tokens.append(_compressed_tokens(p + qn + pe_ref[...], w2_ref).astype(BF16))
        c_end = lax.broadcasted_iota(jnp.int32, (n_cmp, LANES), 0) * CMP_STRIDE + (CMP_LEN - 1)
        ccat_scr[:, 0:HEAD_DIM] = tokens[0]
        ccat_scr[:, HEAD_DIM:] = _key_aug(c_end, False)
        cmpv_scr[...] = tokens[1]

    qb = q_ref[...] * scale
    q4b = jnp.concatenate([qb[:, g * HEAD_DIM:(g + 1) * HEAD_DIM] for g in range(NSA_GROUP)], axis=0).astype(BF16)
    rows4 = NSA_GROUP * tq
    tpos4 = t0 + lax.broadcasted_iota(jnp.int32, (rows4, 1), 0) % tq
    slope4 = _slopes(kv, tq)
    lane = lax.broadcasted_iota(jnp.int32, (rows4, LANES), 1)
    alibi = jnp.where(lane == AUG_HI, slope4 * SEL_BLOCK, jnp.where(lane == AUG_LO, slope4, 0.0))
    q_plain = jnp.concatenate([q4b, alibi.astype(BF16)], axis=1)

    c_end = lax.broadcasted_iota(jnp.int32, (1, n_cmp), 1) * CMP_STRIDE + (CMP_LEN - 1)
    p = _masked_softmax(_dot_nt(q_plain, ccat_scr[...]), c_end <= tpos4)
    o_cmp = _dot(p.astype(BF16), cmpv_scr[...])
    psum = p[0:tq]
    for g in range(1, NSA_GROUP):
        psum = psum + p[g * tq:(g + 1) * tq]

    band = min(WINDOW + tq, t_len)
    b0 = pl.multiple_of(jnp.maximum(t0 + tq - band, 0), BF16_ROWS)
    dist = tpos4 - (b0 + lax.broadcasted_iota(jnp.int32, (1, band), 1))
    pw = _masked_softmax(_dot_nt(q_plain, wcat_scr[pl.ds(b0, band), :]), (dist >= 0) & (dist < WINDOW))
    o_win = _dot(pw.astype(BF16), wv_scr[pl.ds(b0, band), :])

    imp_t =_dot_nt(_overlap_matrix(n_cmp, n_blk, transposed=True), psum, HIGHEST)
    tpos_row = t0 + lax.broadcasted_iota(jnp.int32, (1, tq), 1)
    selb = _select_bias_t(imp_t, tpos_row).T
    selb = jnp.concatenate([selb, jnp.zeros((tq, LANES - n_blk), F32)], axis=1)
    selb4 = jnp.concatenate([selb] * NSA_GROUP, axis=0)
    q_sel = jnp.concatenate([q4b, jnp.where(lane < SEL_BLOCK, selb4, alibi).astype(BF16)], axis=1)

    def slc_tile(jt, carry, diagonal):
        m, l, acc = carry
        k0 = pl.multiple_of(jt * tk, tk)
        s = _dot_nt(q_sel, kcat_scr[pl.ds(k0, tk), :])
        if diagonal:
            s = jnp.where(k0 + lax.broadcasted_iota(jnp.int32, (1, tk), 1) <= tpos4, s, NEG_INF)
        m_new = jnp.maximum(m, jnp.max(s, axis=-1, keepdims=True))
        alpha = jnp.exp(m - m_new)
        e = jnp.exp(s - m_new)
        l = alpha * l + jnp.sum(e, axis=-1, keepdims=True)
        acc = alpha * acc + _dot(e.astype(BF16), sv_scr[pl.ds(k0, tk), :])
        return m_new, l, acc

    n_full = t0 // tk
    init = (jnp.full((rows4, 1), NEG_INF, F32), jnp.zeros((rows4, 1), F32), jnp.zeros((rows4, HEAD_DIM), F32))
    carry = lax.fori_loop(0, n_full, lambda jt, c: slc_tile(jt, c, False), init)
    _, l, acc = slc_tile(n_full, carry, True)
    o_slc = acc / l

    outs = _combine_branches(tail_ref[...], kv, o_cmp, o_slc, o_win, tq)
    for g in range(NSA_GROUP):
        o_ref[:, g * HEAD_DIM:(g + 1) * HEAD_DIM] = outs[g]


def _nsa_prompt_call(z, pqk, pqv, pek, pev, w2k, w2v, n_batch, t_len):
    tq = min(128, t_len)
    tk = min(512, t_len)
    nq = t_len // tq
    n_cmp = t_len // CMP_STRIDE
    qblk0 = OFF_NQ // (NSA_GROUP * HEAD_DIM)
    assert t_len <= SEL_BLOCK * SEL_BLOCK and 8 % NSA_HEADS == 0 and tk % tq == 0
    rowspec = lambda off: pl.BlockSpec((t_len, HEAD_DIM), lambda b, kv, i: (b, off // HEAD_DIM + kv))
    pqspec = pl.BlockSpec((n_cmp, 2 * HEAD_DIM), lambda b, kv, i: (b, kv))
    cspec = lambda shape: pl.BlockSpec(shape, lambda b, kv, i: (0, 0))
    return pl.pallas_call(
        functools.partial(_nsa_prompt_kernel, tk=tk),
        grid=(n_batch, NSA_KV_HEADS, nq),
        in_specs=[pl.BlockSpec((tq, NSA_GROUP * HEAD_DIM), lambda b, kv, i: (b * nq + i, qblk0 + kv)),
                  pqspec, pqspec, cspec((1, HEAD_DIM)), cspec((1, HEAD_DIM)),
                  cspec((HEAD_DIM, HEAD_DIM)), cspec((HEAD_DIM, HEAD_DIM)),
                  rowspec(OFF_ROWS + 2 * KV_WIDTH), rowspec(OFF_ROWS + 3 * KV_WIDTH),
                  rowspec(OFF_ROWS + 4 * KV_WIDTH), rowspec(OFF_ROWS + 5 * KV_WIDTH),
                  pl.BlockSpec((tq, LANES), lambda b, kv, i: (b * nq + i, OFF_TAIL // LANES))],
        out_specs=pl.BlockSpec((tq, NSA_GROUP * HEAD_DIM), lambda b, kv, i: (b * nq + i, kv)),
        out_shape=jax.ShapeDtypeStruct((n_batch * t_len, NSA_WIDTH), F32),
        scratch_shapes=[pltpu.VMEM((n_cmp, 2 * HEAD_DIM), BF16), pltpu.VMEM((n_cmp, HEAD_DIM), BF16),
                        pltpu.VMEM((t_len, 2 * HEAD_DIM), BF16), pltpu.VMEM((t_len, HEAD_DIM), BF16),
                        pltpu.VMEM((t_len, 2 * HEAD_DIM), BF16), pltpu.VMEM((t_len, HEAD_DIM), BF16)],
        compiler_params=_cparams(("parallel", "parallel", "arbitrary")),
        name="nsa_prompt",
    )(z, pqk, pqv, pek, pev, w2k, w2v, z, z, z, z, z)


def _nsa_sample_kernel(pt_ref, q_ref, new_ref, pqnk_ref, pqnv_ref, pek_ref, pev_ref, w2k_ref, w2v_ref, wink_ref,
                       winv_ref, tail_ref, pqk_hbm, pqv_hbm, sk_hbm, sv_hbm, o_ref,
                       pqk_buf, pqv_buf, sk_buf, sv_buf, sem, *, n_pages, past_len, layer):
    b = pl.program_id(0)
    nb = pl.num_programs(0)
    t_new = q_ref.shape[0]
    page = PAGE_ROWS
    cpp = page // CMP_STRIDE
    new_rows = LANES
    n_keys = past_len + new_rows
    n_cmp = pqk_buf.shape[2]
    n_blk = SEL_BLOCK
    scale = HEAD_DIM ** -0.5
    ppr = cpp * NSA_KV_HEADS
    kpr = page * NSA_KV_HEADS

    def copies(seq, slot):
        out = []
        for p in range(n_pages):
            pg = pt_ref[seq, p]
            out.append(pltpu.make_async_copy(pqk_hbm.at[:, pl.ds(pg * ppr, ppr)], pqk_buf.at[slot, :, pl.ds(p * ppr, ppr)], sem.at[slot, 0]))
            out.append(pltpu.make_async_copy(pqv_hbm.at[:, pl.ds(pg * ppr, ppr)], pqv_buf.at[slot, :, pl.ds(p * ppr, ppr)], sem.at[slot, 1]))
            out.append(pltpu.make_async_copy(sk_hbm.at[layer, pg], sk_buf.at[slot, pl.ds(p * kpr, kpr)], sem.at[slot, 2]))
            out.append(pltpu.make_async_copy(sv_hbm.at[layer, pg], sv_buf.at[slot, pl.ds(p * kpr, kpr)], sem.at[slot, 3]))
        return out

    slot = b % 2

    @pl.when(b == 0)
    def _():
        for cp in copies(0, 0):
            cp.start()

    @pl.when(b + 1 < nb)
    def _():
        for cp in copies(b + 1, 1 - slot):
            cp.start()

    new = new_ref[...]
    qb = q_ref[...]
    tail = tail_ref[...]
    tq = t_new
    rows4 = NSA_GROUP * tq
    tpos4 = past_len + lax.broadcasted_iota(jnp.int32, (rows4, 1), 0) % tq
    tpos = past_len + lax.broadcasted_iota(jnp.int32, (tq, 1), 0)

    win_len = wink_ref.shape[0] // NSA_KV_HEADS
    band = win_len + new_rows

    def new_tile(which, kv):
        c0 = which * KV_WIDTH + kv * HEAD_DIM
        return jnp.concatenate([new[:, c0:c0 + HEAD_DIM], jnp.zeros((new_rows - t_new, HEAD_DIM), F32)],
                               axis=0).astype(BF16)

    for cp in copies(b, slot):
        cp.wait()

    n_pool = n_pages * cpp
    rown = lax.broadcasted_iota(jnp.int32, (n_pool, 1), 0)
    onehot = _block_onehot(n_blk, 0, n_keys)
    overlap = _overlap_matrix(n_cmp, n_blk)
    rowt = lax.broadcasted_iota(jnp.int32, (n_cmp - n_pool, 1), 0)
    kvs = range(NSA_KV_HEADS)
    q4b = [jnp.concatenate([qb[:, (kv * NSA_GROUP + g) * HEAD_DIM:(kv * NSA_GROUP + g + 1) * HEAD_DIM]
                            for g in range(NSA_GROUP)], axis=0).astype(BF16) for kv in kvs]
    slope4 = [_slopes(kv, tq) for kv in kvs]
    kv_rows = [pl.ds(kv, n_pool, stride=NSA_KV_HEADS) for kv in kvs]
    cmp_tokens = []
    for pq_buf, pqn_ref, pe_ref, w2_ref in ((pqk_buf, pqnk_ref, pek_ref, w2k_ref), (pqv_buf, pqnv_ref, pev_ref, w2v_ref)):
        p = [pq_buf[slot, 0, kv_rows[kv], :] for kv in kvs]
        qn = [pltpu.roll(pq_buf[slot, 1, kv_rows[kv], :], n_pool - 1, 0) for kv in kvs]
        pn = [pqn_ref[:, kv * 2 * HEAD_DIM:kv * 2 * HEAD_DIM + HEAD_DIM] for kv in kvs]
        qnn = [pqn_ref[:, kv * 2 * HEAD_DIM + HEAD_DIM:(kv + 1) * 2 * HEAD_DIM] for kv in kvs]
        hid = [jnp.concatenate([p[kv] + jnp.where(rown == n_pool - 1, qnn[kv], qn[kv]),
                                jnp.where(rowt == 0, pn[kv], 0.0)], axis=0) + pe_ref[...] for kv in kvs]
        cmp_tokens.append([_compressed_tokens(h, w2_ref) for h in hid])
    cmp_out = [_cmp_branch(q4b[kv], cmp_tokens[0][kv], cmp_tokens[1][kv], tpos4, slope4[kv], tq) for kv in kvs]

    kw = [wink_ref[pl.ds(kv, win_len, stride=NSA_KV_HEADS), :].astype(BF16) for kv in kvs]
    vw = [winv_ref[pl.ds(kv, win_len, stride=NSA_KV_HEADS), :].astype(BF16) for kv in kvs]
    dist_w = tpos4 - ((past_len - win_len) + lax.broadcasted_iota(jnp.int32, (1, band), 1))
    valid_w = (dist_w >= 0) & (dist_w < WINDOW)
    s = [jnp.concatenate([_dot_nt(q4b[kv], kw[kv]), _dot_nt(q4b[kv], new_tile(4, kv))], axis=1) * scale for kv in kvs]
    pw = [_masked_softmax(s[kv] - slope4[kv] * dist_w.astype(F32), valid_w).astype(BF16) for kv in kvs]
    o_win = [_dot(pw[kv][:, :win_len], vw[kv]) + _dot(pw[kv][:, win_len:], new_tile(5, kv)) for kv in kvs]

    imp = [_dot(cmp_out[kv][1], overlap, HIGHEST) for kv in kvs]
    selb = [_select_bias_few(x, tpos).astype(BF16) for x in imp]
    selb4 = [jnp.concatenate([x] * NSA_GROUP, axis=0) for x in selb]
    ks = [sk_buf[slot, pl.ds(kv, past_len, stride=NSA_KV_HEADS), :].astype(BF16) for kv in kvs]
    vs = [sv_buf[slot, pl.ds(kv, past_len, stride=NSA_KV_HEADS), :].astype(BF16) for kv in kvs]
    dist_s = tpos4 - lax.broadcasted_iota(jnp.int32, (1, n_keys), 1)
    s = [jnp.concatenate([_dot_nt(q4b[kv], ks[kv]), _dot_nt(q4b[kv], new_tile(2, kv))], axis=1) * scale
         + _dot(selb4[kv], onehot) for kv in kvs]
    ps = [_masked_softmax(s[kv] - slope4[kv] * dist_s.astype(F32), dist_s >= 0).astype(BF16) for kv in kvs]
    o_slc = [_dot(ps[kv][:, :past_len], vs[kv]) + _dot(ps[kv][:, past_len:], new_tile(3, kv)) for kv in kvs]

    for kv in kvs:
        outs = _combine_branches(tail, kv, cmp_out[kv][0], o_slc[kv], o_win[kv], tq)
        for g in range(NSA_GROUP):
            h0 = (kv * NSA_GROUP + g) * HEAD_DIM
            o_ref[:, h0:h0 + HEAD_DIM] = outs[g]


PAGE_ROWS = 128


def _kv_rows(x):
    return x.reshape(x.shape[:-3] + (x.shape[-3] * x.shape[-2], x.shape[-1]))


def _nsa_sample_call(page_table, z3, pq_new_k, pq_new_v, pek, pev, w2k, w2v, win_k, win_v, pq_pool_k, pq_pool_v,
                     pool_sk, pool_sv, layer):
    bd, t_new, _ = z3.shape
    n_pages = page_table.shape[1]
    past_len = n_pages * PAGE_ROWS
    win_len = win_k.shape[2]
    n_cmp = 2 * n_pages * (PAGE_ROWS // CMP_STRIDE)
    assert (past_len + LANES) // SEL_BLOCK <= SEL_BLOCK and past_len // CMP_STRIDE + 8 <= n_cmp and t_new <= LANES

    def zspec(width, col0):
        return pl.BlockSpec((None, t_new, width), lambda b, pt: (b, 0, col0 // width))

    cspec = lambda shape: pl.BlockSpec(shape, lambda b, pt: (0,) * len(shape))
    pqnspec = pl.BlockSpec((None, 1, 4 * HEAD_DIM), lambda b, pt: (b, 0, 0))
    winspec = pl.BlockSpec((None, None, win_len * NSA_KV_HEADS, HEAD_DIM), lambda b, pt: (layer, b, 0, 0))
    anyspec = pl.BlockSpec(memory_space=pl.ANY)
    grid_spec = pltpu.PrefetchScalarGridSpec(
        num_scalar_prefetch=1,
        grid=(bd,),
        in_specs=[zspec(NSA_WIDTH, OFF_NQ), zspec(6 * KV_WIDTH, OFF_ROWS), pqnspec, pqnspec,
                  cspec((1, HEAD_DIM)), cspec((1, HEAD_DIM)), cspec((HEAD_DIM, HEAD_DIM)), cspec((HEAD_DIM, HEAD_DIM)),
                  winspec, winspec, zspec(LANES, OFF_TAIL), anyspec, anyspec, anyspec, anyspec],
        out_specs=pl.BlockSpec((None, t_new, NSA_WIDTH), lambda b, pt: (b, 0, 0)),
        scratch_shapes=[pltpu.VMEM((2, 2, n_cmp, HEAD_DIM), F32),
                        pltpu.VMEM((2, 2, n_cmp, HEAD_DIM), F32),
                        pltpu.VMEM((2, past_len * NSA_KV_HEADS, HEAD_DIM), F32),
                        pltpu.VMEM((2, past_len * NSA_KV_HEADS, HEAD_DIM), F32),
                        pltpu.SemaphoreType.DMA((2, 4))],
    )
    return pl.pallas_call(
        functools.partial(_nsa_sample_kernel, n_pages=n_pages, past_len=past_len, layer=layer),
        grid_spec=grid_spec,
        out_shape=jax.ShapeDtypeStruct((bd, t_new, NSA_WIDTH), F32),
        compiler_params=_cparams(("arbitrary",)),
        name="nsa_sample",
    )(page_table, z3, z3, pq_new_k, pq_new_v, pek, pev, w2k, w2v, _kv_rows(win_k), _kv_rows(win_v), z3,
      pq_pool_k, pq_pool_v, _kv_rows(pool_sk), _kv_rows(pool_sv))


def _merge_kernel(ya_ref, yb_ref, ga_ref, gb_ref, wa_ref, wb_ref, wo_ref, x_ref, gate_ref, g_ref, b_ref, o_ref,
                  *, alpha):
    a = _dot(ya_ref[...].astype(BF16), wa_ref[...])
    b = _dot(yb_ref[...].astype(BF16), wb_ref[...])
    mix = _sigmoid(ga_ref[...]) * a + _sigmoid(gb_ref[...]) * b
    y = _dot(mix.astype(BF16), wo_ref[...])
    o_ref[...] = _ln(alpha * x_ref[...] + gate_ref[...] * y) * g_ref[...] + b_ref[...]


def _merge_call(ya, yb, z, wa, wb, wo, layer, x, gate, ln_g, ln_b, rows_per_seq, alpha):
    m, d = x.shape
    tm = min(256, m)
    rspec = lambda w: pl.BlockSpec((tm, w), lambda i: (i, 0))
    cspec = lambda a: pl.BlockSpec((None,) + a.shape[1:], lambda i: (layer, 0, 0))
    return pl.pallas_call(
        functools.partial(_merge_kernel, alpha=alpha),
        grid=(m // tm,),
        in_specs=[rspec(DN_WIDTH), rspec(NSA_WIDTH),
                  pl.BlockSpec((tm, d), lambda i: (i, OFF_MERGE // d)),
                  pl.BlockSpec((tm, d), lambda i: (i, OFF_MERGE // d + 1)),
                  cspec(wa), cspec(wb), cspec(wo), rspec(d), _rowparam_spec(gate, tm, rows_per_seq),
                  pl.BlockSpec((1, d), lambda i: (0, 0)), pl.BlockSpec((1, d), lambda i: (0, 0))],
        out_specs=rspec(d),
        out_shape=jax.ShapeDtypeStruct((m, d), F32),
        compiler_params=_cparams(("parallel",)),
        name="merge_out_ln",
    )(ya, yb, z, z, wa, wb, wo, x, gate, ln_g.reshape(1, d), ln_b.reshape(1, d))


def _ffn_down_kernel(ua_ref, ub_ref, pa_ref, pb_ref, cwa_ref, cwb_ref, wd_ref, x_ref, gate_ref, g_ref, b_ref, o_ref,
                     acc_scr, *, alpha, rows_per_seq):
    i = pl.program_id(0)
    k = pl.program_id(1)
    tm = ua_ref.shape[0]
    seq_start = (i * tm) % rows_per_seq == 0
    row8 = lax.broadcasted_iota(jnp.int32, (SUBLANES, 1), 0)

    def conv(u_ref, p_ref, w_ref):
        u = u_ref[...]
        w = w_ref[...]
        prev = jnp.where(seq_start, 0.0, p_ref[...])
        y = u * w[FFN_CONV - 1:FFN_CONV, :]
        for sh in range(1, FFN_CONV):
            shifted = pltpu.roll(u, sh, 0)
            head = jnp.where(row8 < sh, pltpu.roll(prev, sh, 0), shifted[:SUBLANES])
            shifted = jnp.concatenate([head, shifted[SUBLANES:]], axis=0)
            y = y + shifted * w[FFN_CONV - 1 - sh:FFN_CONV - sh, :]
        return y

    act = _silu(conv(ua_ref, pa_ref, cwa_ref)) * conv(ub_ref, pb_ref, cwb_ref)
    part = _dot(act.astype(BF16), wd_ref[...])

    @pl.when(k == 0)
    def _():
        acc_scr[...] = part

    @pl.when(k > 0)
    def _():
        acc_scr[...] += part

    @pl.when(k == pl.num_programs(1) - 1)
    def _():
        o_ref[...] = _ln(alpha * x_ref[...] + gate_ref[...] * acc_scr[...]) * g_ref[...] + b_ref[...]


def _ffn_down_call(u, conv_w, wd, layer, x, gate, ln_g, ln_b, rows_per_seq, alpha):
    m, d = x.shape
    d_ff = wd.shape[1]
    tm = min(512, m)
    tk = 512
    nk = d_ff // tk
    sub_per_tile = tm // SUBLANES
    prev_idx = lambda i: jnp.maximum(i * sub_per_tile - 1, 0)
    return pl.pallas_call(
        functools.partial(_ffn_down_kernel, alpha=alpha, rows_per_seq=rows_per_seq),
        grid=(m // tm, nk),
        in_specs=[pl.BlockSpec((tm, tk), lambda i, k: (i, k)),
                  pl.BlockSpec((tm, tk), lambda i, k: (i, nk + k)),
                  pl.BlockSpec((SUBLANES, tk), lambda i, k: (prev_idx(i), k)),
                  pl.BlockSpec((SUBLANES, tk), lambda i, k: (prev_idx(i), nk + k)),
                  pl.BlockSpec((FFN_CONV, tk), lambda i, k: (0, k)),
                  pl.BlockSpec((FFN_CONV, tk), lambda i, k: (0, nk + k)),
                  pl.BlockSpec((None, tk, d), lambda i, k: (layer, k, 0)),
                  pl.BlockSpec((tm, d), lambda i, k: (i, 0)),
                  _rowparam_spec(gate, tm, rows_per_seq),
                  pl.BlockSpec((1, d), lambda i, k: (0, 0)), pl.BlockSpec((1, d), lambda i, k: (0, 0))],
        out_specs=pl.BlockSpec((tm, d), lambda i, k: (i, 0)),
        out_shape=jax.ShapeDtypeStruct((m, d), F32),
        scratch_shapes=[pltpu.VMEM((tm, d), F32)],
        compiler_params=_cparams(("parallel", "arbitrary")),
        name="ffn_down_ln",
    )(u, u, u, u, conv_w, conv_w, wd, x, gate, ln_g.reshape(1, d), ln_b.reshape(1, d))


def _ffn_fused_kernel(x_ref, xp_ref, sh_ref, sc_ref, wua_ref, wub_ref, cwa_ref, cwb_ref, wd_ref, gate_ref, g_ref,
                      b_ref, o_ref, ta_ref, tb_ref, h_scr, acc_scr, *, alpha, rows_per_seq):
    i = pl.program_id(0)
    k = pl.program_id(1)
    tm = x_ref.shape[0]
    halo = xp_ref.shape[0]

    @pl.when(k == 0)
    def _():
        seq_start = (i * tm) % rows_per_seq == 0
        mod = lambda x: _ln(x) * (1.0 + sc_ref[...]) + sh_ref[...]
        h_scr[0:halo, :] = jnp.where(seq_start, 0.0, mod(xp_ref[...])).astype(BF16)
        h_scr[halo:, :] = mod(x_ref[...]).astype(BF16)

    h = h_scr[...]

    def branch(w_ref, cw_ref, t_ref):
        u = _dot(h, w_ref[...])
        cw = cw_ref[...]
        y = u * cw[FFN_CONV - 1:FFN_CONV, :]
        for sh in range(1, FFN_CONV):
            y = y + pltpu.roll(u, sh, 0) * cw[FFN_CONV - 1 - sh:FFN_CONV - sh, :]
        t_ref[...] = u[halo + tm - SUBLANES:, :]
        return y[halo:, :]

    act = _silu(branch(wua_ref, cwa_ref, ta_ref)) * branch(wub_ref, cwb_ref, tb_ref)
    part = _dot(act.astype(BF16), wd_ref[...])

    @pl.when(k == 0)
    def _():
        acc_scr[...] = part

    @pl.when(k > 0)
    def _():
        acc_scr[...] += part

    @pl.when(k == pl.num_programs(1) - 1)
    def _():
        o_ref[...] = _ln(alpha * x_ref[...] + gate_ref[...] * acc_scr[...]) * g_ref[...] + b_ref[...]


def _ffn_fused_call(x, shift, scale, w_up, conv_w, wd, layer, gate, ln_g, ln_b, rows_per_seq, alpha):
    m, d = x.shape
    d_ff = wd.shape[1]
    tm = min(512, m)
    tk = 512
    nk = d_ff // tk
    halo = BF16_ROWS
    assert FFN_CONV - 1 <= halo and tm % halo == 0 and rows_per_seq % tm == 0
    prev_idx = lambda i: jnp.maximum(i * (tm // halo) - 1, 0)
    seqspec = lambda p: _rowparam_spec(p, tm, rows_per_seq)
    tail = jax.ShapeDtypeStruct((m // tm * SUBLANES, d_ff), F32)
    tspec = pl.BlockSpec((SUBLANES, tk), lambda i, k: (i, k))
    return pl.pallas_call(
        functools.partial(_ffn_fused_kernel, alpha=alpha, rows_per_seq=rows_per_seq),
        grid=(m // tm, nk),
        in_specs=[pl.BlockSpec((tm, d), lambda i, k: (i, 0)),
                  pl.BlockSpec((halo, d), lambda i, k: (prev_idx(i), 0)),
                  seqspec(shift), seqspec(scale),
                  pl.BlockSpec((None, d, tk), lambda i, k: (layer, 0, k)),
                  pl.BlockSpec((None, d, tk), lambda i, k: (layer, 0, nk + k)),
                  pl.BlockSpec((FFN_CONV, tk), lambda i, k: (0, k)),
                  pl.BlockSpec((FFN_CONV, tk), lambda i, k: (0, nk + k)),
                  pl.BlockSpec((None, tk, d), lambda i, k: (layer, k, 0)),
                  seqspec(gate),
                  pl.BlockSpec((1, d), lambda i, k: (0, 0)), pl.BlockSpec((1, d), lambda i, k: (0, 0))],
        out_specs=[pl.BlockSpec((tm, d), lambda i, k: (i, 0)), tspec, tspec],
        out_shape=[jax.ShapeDtypeStruct((m, d), F32), tail, tail],
        scratch_shapes=[pltpu.VMEM((halo + tm, d), BF16), pltpu.VMEM((tm, d), F32)],
        compiler_params=_cparams(("parallel", "arbitrary")),
        name="ffn_fused",
    )(x, x, shift, scale, w_up, w_up, conv_w, conv_w, wd, gate, ln_g.reshape(1, d), ln_b.reshape(1, d))


def _permute_w_in(w_in):
    d = w_in.shape[-2]
    sizes = (3 * DN_WIDTH, DN_HEADS, DN_HEADS, DN_WIDTH, NSA_WIDTH) + (KV_WIDTH,) * 6 + (3 * NSA_HEADS, 2 * d)
    offs = [0]
    for s in sizes:
        offs.append(offs[-1] + s)
    part = lambda i: w_in[..., offs[i]:offs[i + 1]].astype(BF16)
    cols = [part(0), part(3), part(12), part(4)] + [part(i) for i in range(5, 11)] + [part(1), part(2), part(11)]
    used = sum(c.shape[-1] for c in cols)
    cols.append(jnp.zeros(w_in.shape[:-1] + (IN_PAD_WIDTH - used,), BF16))
    return jnp.concatenate(cols, axis=-1)


def _pad_groups(x, n_pad):
    b, t, c = x.shape
    return jnp.concatenate([jnp.zeros((b, n_pad, c), x.dtype), x], axis=1).reshape(b * (n_pad + t), c)


def kernel(x_prompt, x_sample, cache_cmp_k, cache_cmp_v, cache_slc_k, cache_slc_v, cache_win_k, cache_win_v, state_dn, state_dn_conv, state_ffn_conv, page_table, c_prompt, c_sample, w_ada, b_ada, w_in, dn_conv_w, dn_a_log, dn_dt_bias, dn_norm_w, cmp_k_w1, cmp_k_pe, cmp_k_w2, cmp_v_w1, cmp_v_pe, cmp_v_w2, w_branch_a, w_branch_b, w_out, ln1_g, ln1_b, ffn_w_up, ffn_conv_w, ffn_w_down, ln2_g, ln2_b):
    depth = w_in.shape[0]
    bp, t_len, d = x_prompt.shape
    bd, t_new, _ = x_sample.shape
    n_phys = cache_cmp_k.shape[1]
    win_len = cache_win_k.shape[2]
    d_ff = ffn_w_down.shape[1]
    alpha = (2 * depth) ** 0.25
    grp = SUBLANES
    assert t_new <= grp - (DN_CONV - 1) and t_new >= DN_CONV - 1 and cache_cmp_k.shape[2] == PAGE_ROWS

    xp = x_prompt.reshape(bp * t_len, d)
    xs = x_sample.reshape(bd * t_new, d)
    c_all = jnp.concatenate([c_prompt, c_sample], axis=0)
    c_rows = -(-c_all.shape[0] // SUBLANES) * SUBLANES
    c_all = jnp.pad(c_all, ((0, c_rows - c_all.shape[0]), (0, 0)))

    outs_p = {k: [] for k in ('cmp_k', 'cmp_v', 'slc_k', 'slc_v', 'win_k', 'win_v', 'dn_state', 'dn_conv', 'ffn_conv')}
    outs_s = {k: [] for k in outs_p}

    w_in_b = _permute_w_in(w_in)
    wa, wb, wo = w_branch_a.astype(BF16), w_branch_b.astype(BF16), w_out.astype(BF16)
    w_up, w_dn = ffn_w_up.astype(BF16), ffn_w_down.astype(BF16)
    zero_state = jnp.zeros((1, bp, DN_HEADS, DN_DK, DN_DV), F32)

    for l in range(depth):
        mod = _ada_call(c_all, w_ada, l, b_ada[l])
        mod_p = mod[:bp].reshape(bp, 6, 1, d)
        mod_s = mod[bp:bp + bd].reshape(bd, 6, d)
        mp = [mod_p[:, i] for i in range(6)]
        ms = [jnp.repeat(mod_s[:, i], t_new, axis=0) for i in range(6)]

        wpq_k = _pq_weights(cmp_k_w1[l])
        wpq_v = _pq_weights(cmp_v_w1[l])
        pek = _pe_term(cmp_k_pe[l], wpq_k)
        pev = _pe_term(cmp_v_pe[l], wpq_v)
        w2k = cmp_k_w2[l].astype(BF16)
        w2v = cmp_v_w2[l].astype(BF16)

        zp = _ln_mm_call(xp, mp[0], mp[1], w_in_b, l, t_len, "in_proj")
        zp3 = zp.reshape(bp, t_len, -1)
        rows_p = lambda i, t0: zp3[:, t0:, OFF_ROWS + i * KV_WIDTH:OFF_ROWS + (i + 1) * KV_WIDTH].reshape(
            bp, t_len - t0, NSA_KV_HEADS, HEAD_DIM)
        beta, gc, gl = _gdn_gates_call(zp, OFF_TAIL, dn_a_log[l], dn_dt_bias[l], DN_CHUNK, 0)
        gcrow = gc.T.reshape(DN_HEADS, 1, bp * t_len)
        ya_p, dn_state_p = _gdn_call(zp, OFF_QKV, dn_conv_w[l], zp, OFF_DNGATE, dn_norm_w[l], beta, gc, gl, gcrow,
                                     zero_state, 0, rows_per_seq=t_len, chunk=GDN_MATMUL_ROWS,
                                     sub=math.gcd(DN_CHUNK, t_len), per_chunk_state=False, n_pad=0)
        chunk_w = CMP_STRIDE * KV_WIDTH
        pqk = _pq_call(zp[:, OFF_ROWS:OFF_ROWS + KV_WIDTH].reshape(-1, chunk_w), wpq_k)
        pqv = _pq_call(zp[:, OFF_ROWS + KV_WIDTH:OFF_ROWS + 2 * KV_WIDTH].reshape(-1, chunk_w), wpq_v)
        yb_p = _nsa_prompt_call(zp, pqk, pqv, pek, pev, w2k, w2v, bp, t_len)
        x1p = _merge_call(ya_p, yb_p, zp, wa, wb, wo, l, xp, mp[2], ln1_g[l], ln1_b[l], t_len, alpha)
        xp, tail_a, tail_b = _ffn_fused_call(x1p, mp[3], mp[4], w_up, ffn_conv_w[l], w_dn, l, mp[5], ln2_g[l],
                                             ln2_b[l], t_len, alpha)
        tiles_per_seq = tail_a.shape[0] // SUBLANES // bp
        last_rows = lambda t: t.reshape(bp, tiles_per_seq, SUBLANES, d_ff)[:, -1, SUBLANES - (FFN_CONV - 1):]

        keep = min(WINDOW, t_len)
        for i, name in enumerate(('cmp_k', 'cmp_v', 'slc_k', 'slc_v')):
            outs_p[name].append(rows_p(i, 0))
        outs_p['win_k'].append(rows_p(4, t_len - keep))
        outs_p['win_v'].append(rows_p(5, t_len - keep))
        outs_p['dn_state'].append(dn_state_p)
        outs_p['dn_conv'].append(zp3[:, t_len - (DN_CONV - 1):, OFF_QKV:OFF_QKV + 3 * DN_WIDTH])
        outs_p['ffn_conv'].append(jnp.concatenate([last_rows(tail_a), last_rows(tail_b)], axis=-1))

        zs = _ln_mm_call(xs, ms[0], ms[1], w_in_b, l, t_new, "in_proj")
        zs3 = zs.reshape(bd, t_new, -1)
        rows_s = zs3[:, :, OFF_ROWS:OFF_ROWS + 6 * KV_WIDTH].reshape(bd, t_new, 6, NSA_KV_HEADS, HEAD_DIM)
        n_pad = grp - t_new
        qkv_g = jnp.concatenate([jnp.zeros((bd, n_pad - (DN_CONV - 1), 3 * DN_WIDTH), F32), state_dn_conv[l],
                                 zs3[:, :, OFF_QKV:OFF_QKV + 3 * DN_WIDTH]], axis=1).reshape(bd * grp, -1)
        zs_g = _pad_groups(jnp.concatenate([zs3[:, :, OFF_DNGATE:OFF_DNGATE + DN_WIDTH],
                                            zs3[:, :, OFF_TAIL:OFF_TAIL + LANES]], axis=-1), n_pad)
        beta, gc, gl = _gdn_gates_call(zs_g, DN_WIDTH, dn_a_log[l], dn_dt_bias[l], grp, n_pad)
        gcrow = gc.T.reshape(DN_HEADS, 1, bd * grp)
        ya_g, dn_state_s = _gdn_call(qkv_g, 0, dn_conv_w[l], zs_g, 0, dn_norm_w[l], beta, gc, gl, gcrow, state_dn, l,
                                     rows_per_seq=grp, chunk=GDN_MATMUL_ROWS, sub=grp, per_chunk_state=True, n_pad=n_pad)
        ya_s = ya_g.reshape(bd, grp, DN_WIDTH)[:, n_pad:].reshape(bd * t_new, DN_WIDTH)
        pq_pool_k = _pq_pool_call(cache_cmp_k, l, wpq_k)
        pq_pool_v = _pq_pool_call(cache_cmp_v, l, wpq_v)
        new_chunk = lambda i: jnp.pad(zs3[:, :, OFF_ROWS + i * KV_WIDTH:OFF_ROWS + (i + 1) * KV_WIDTH],
                                      ((0, 0), (0, CMP_STRIDE - t_new), (0, 0))).reshape(bd, chunk_w)
        pq_new_k = _pq_call(new_chunk(0), wpq_k).reshape(bd, 1, -1)
        pq_new_v = _pq_call(new_chunk(1), wpq_v).reshape(bd, 1, -1)
        yb_s = _nsa_sample_call(page_table, zs3, pq_new_k, pq_new_v, pek, pev, w2k, w2v, cache_win_k, cache_win_v,
                                pq_pool_k, pq_pool_v, cache_slc_k, cache_slc_v, l).reshape(bd * t_new, NSA_WIDTH)
        x1s = _merge_call(ya_s, yb_s, zs, wa, wb, wo, l, xs, ms[2], ln1_g[l], ln1_b[l], t_new, alpha)
        us = _ln_mm_call(x1s, ms[3], ms[4], w_up, l, t_new, "ffn_up")
        us3 = us.reshape(bd, t_new, -1)
        u_g = jnp.concatenate([jnp.zeros((bd, n_pad - (FFN_CONV - 1), 2 * d_ff), F32), state_ffn_conv[l], us3],
                              axis=1).reshape(bd * grp, -1)
        x1_g = _pad_groups(x1s.reshape(bd, t_new, d), n_pad)
        gate2_g = jnp.repeat(mod_s[:, 5], grp, axis=0)
        xs_g = _ffn_down_call(u_g, ffn_conv_w[l], w_dn, l, x1_g, gate2_g, ln2_g[l], ln2_b[l], grp, alpha)
        xs = xs_g.reshape(bd, grp, d)[:, n_pad:].reshape(bd * t_new, d)

        for i, name in enumerate(('cmp_k', 'cmp_v', 'slc_k', 'slc_v')):
            outs_s[name].append(rows_s[:, :, i])
        outs_s['win_k'].append(rows_s[:, :, 4])
        outs_s['win_v'].append(rows_s[:, :, 5])
        outs_s['dn_state'].append(dn_state_s)
        outs_s['dn_conv'].append(jnp.concatenate([state_dn_conv[l], zs3[:, :, OFF_QKV:OFF_QKV + 3 * DN_WIDTH]], axis=1)[:, t_new:])
        outs_s['ffn_conv'].append(jnp.concatenate([state_ffn_conv[l], us3], axis=1)[:, t_new:])

    order = ('cmp_k', 'cmp_v', 'slc_k', 'slc_v', 'win_k', 'win_v', 'dn_state', 'dn_conv', 'ffn_conv')
    stacked_s = {k: jnp.stack(outs_s[k]) for k in order}
    for name, cache in (('win_k', cache_win_k), ('win_v', cache_win_v)):
        stacked_s[name] = jnp.concatenate([cache, stacked_s[name]], axis=2)[:, :, t_new:]
    return ((xp.reshape(bp, t_len, d), xs.reshape(bd, t_new, d))
            + tuple(jnp.stack(outs_p[k]) for k in order) + tuple(stacked_s[k] for k in order))
```

```python
import functools
import math

import jax
import jax.numpy as jnp
from jax import lax
from jax.experimental import pallas as pl
from jax.experimental.pallas import tpu as pltpu

F32 = jnp.float32
BF16 = jnp.bfloat16
HIGHEST = lax.Precision.HIGHEST

DN_HEADS = 8
DN_DK = 128
DN_DV = 128
DN_WIDTH = DN_HEADS * DN_DV
DN_CONV = 4
DN_CHUNK = 64
NSA_HEADS = 8
NSA_KV_HEADS = 2
NSA_GROUP = NSA_HEADS // NSA_KV_HEADS
HEAD_DIM = 128
NSA_WIDTH = NSA_HEADS * HEAD_DIM
KV_WIDTH = NSA_KV_HEADS * HEAD_DIM
CMP_LEN = 32
CMP_STRIDE = 16
SEL_BLOCK = 64
N_SEL = 16
WINDOW = 512
FFN_CONV = 3
LN_EPS = 1e-5
RMS_EPS = 1e-6
NEG_INF = -1e30
FORCE_SCORE = 1e4

LANES = 128
SUBLANES = 8
BF16_ROWS = 16
VMEM_LIMIT_BYTES = 56 * 1024 * 1024
MXU_DIM = 256
GDN_MATMUL_ROWS = MXU_DIM
NSA_SAMPLE_SEQS_PER_STEP = 2
GDN_HEADS_PER_STEP = 2

OFF_QKV = 0
OFF_DNGATE = 3 * DN_WIDTH
OFF_MERGE = OFF_DNGATE + DN_WIDTH
OFF_NQ = OFF_MERGE + 2 * 2048
OFF_ROWS = OFF_NQ + NSA_WIDTH
OFF_TAIL = OFF_ROWS + 6 * KV_WIDTH
TAIL_BETA = 0
TAIL_A = DN_HEADS
TAIL_NSAG = 2 * DN_HEADS
IN_PAD_WIDTH = 11264


def _cparams(sem):
    return pltpu.CompilerParams(dimension_semantics=sem, vmem_limit_bytes=VMEM_LIMIT_BYTES)


def _dot(a, b, precision=None):
    return jnp.dot(a, b, preferred_element_type=F32, precision=precision)


def _dot_nt(a, b, precision=None):
    return lax.dot_general(a, b, (((1,), (1,)), ((), ())), preferred_element_type=F32, precision=precision)


def _dot_tn(a, b, precision=None):
    return lax.dot_general(a, b, (((0,), (0,)), ((), ())), preferred_element_type=F32, precision=precision)


def _split_bf16(x):
    hi = x.astype(BF16)
    return hi, (x - hi.astype(F32)).astype(BF16)


def _dot_split(a, b):
    (ah, al), (bh, bl) = a, b
    return _dot(ah, bh) + (_dot(al, bh) + _dot(ah, bl))


def _sigmoid(x):
    return 1.0 / (1.0 + jnp.exp(-x))


def _silu(x):
    return x * _sigmoid(x)


def _ln(x):
    mu = jnp.mean(x, axis=-1, keepdims=True)
    xc = x - mu
    var = jnp.mean(xc * xc, axis=-1, keepdims=True)
    return xc * lax.rsqrt(var + LN_EPS)


def _rowparam_spec(p, tm, rows_per_seq):
    if p.ndim == 2:
        return pl.BlockSpec((tm, p.shape[1]), lambda i, *_: (i, 0))
    blocks_per_seq = rows_per_seq // tm
    return pl.BlockSpec((None, 1, p.shape[2]), lambda i, *_: (i // blocks_per_seq, 0, 0))


def _ada_kernel(c_ref, w_ref, b_ref, o_ref):
    h = _silu(c_ref[...]).astype(BF16)
    o_ref[...] = _dot(h, w_ref[...].astype(BF16)) + b_ref[...]


def _ada_call(c, w, layer, b):
    m, k = c.shape
    n = w.shape[2]
    tn = 1024
    return pl.pallas_call(
        _ada_kernel,
        grid=(n // tn,),
        in_specs=[pl.BlockSpec((m, k), lambda j: (0, 0)),
                  pl.BlockSpec((None, k, tn), lambda j: (layer, 0, j)),
                  pl.BlockSpec((1, tn), lambda j: (0, j))],
        out_specs=pl.BlockSpec((m, tn), lambda j: (0, j)),
        out_shape=jax.ShapeDtypeStruct((m, n), F32),
        compiler_params=_cparams(("parallel",)),
        name="ada_mod",
    )(c, w, b.reshape(1, n))


def _ln_mm_kernel(x_ref, sh_ref, sc_ref, w_ref, o_ref, h_scr):
    @pl.when(pl.program_id(1) == 0)
    def _():
        h = _ln(x_ref[...]) * (1.0 + sc_ref[...]) + sh_ref[...]
        h_scr[...] = h.astype(BF16)

    o_ref[...] = _dot(h_scr[...], w_ref[...])


def _ln_mm_call(x, shift, scale, w_bf16, layer, rows_per_seq, name):
    m, k = x.shape
    n = w_bf16.shape[2]
    tm = min(1024, m)
    tn = 512
    return pl.pallas_call(
        _ln_mm_kernel,
        grid=(m // tm, n // tn),
        in_specs=[pl.BlockSpec((tm, k), lambda i, j: (i, 0)),
                  _rowparam_spec(shift, tm, rows_per_seq),
                  _rowparam_spec(scale, tm, rows_per_seq),
                  pl.BlockSpec((None, k, tn), lambda i, j: (layer, 0, j))],
        out_specs=pl.BlockSpec((tm, tn), lambda i, j: (i, j)),
        out_shape=jax.ShapeDtypeStruct((m, n), F32),
        scratch_shapes=[pltpu.VMEM((tm, k), BF16)],
        compiler_params=_cparams(("parallel", "arbitrary")),
        name=name,
    )(x, shift, scale, w_bf16)


def _softplus(x):
    return jnp.maximum(x, 0.0) + jnp.log1p(jnp.exp(-jnp.abs(x)))


def _gdn_gates_kernel(tail_ref, alog_ref, dtb_ref, beta_ref, gc_ref, gl_ref, *, group, n_pad):
    r = tail_ref.shape[0]
    tail = tail_ref[...]
    braw = tail[:, TAIL_BETA:TAIL_BETA + DN_HEADS]
    araw = tail[:, TAIL_A:TAIL_A + DN_HEADS]
    row = lax.broadcasted_iota(jnp.int32, (r, DN_HEADS), 0)
    real = (row % group) >= n_pad
    g = -jnp.exp(alog_ref[...]) * _softplus(araw + dtb_ref[...])
    g = jnp.where(real, g, 0.0)
    beta = jnp.where(real, _sigmoid(braw), 0.0)
    ri = lax.broadcasted_iota(jnp.int32, (r, r), 0)
    ci = lax.broadcasted_iota(jnp.int32, (r, r), 1)
    same = (ri // group) == (ci // group)
    tri = jnp.where(same & (ri >= ci), 1.0, 0.0).astype(F32)
    ones = jnp.where(same, 1.0, 0.0).astype(F32)
    beta_ref[...] = beta
    gc_ref[...] = _dot(tri, g, HIGHEST)
    gl_ref[...] = _dot(ones, g, HIGHEST)


def _gdn_gates_call(z, tail_off, a_log, dt_bias, group, n_pad):
    m = z.shape[0]
    r = min(512, m)
    tail_blk = tail_off // LANES
    out = jax.ShapeDtypeStruct((m, DN_HEADS), F32)
    ospec = pl.BlockSpec((r, DN_HEADS), lambda i: (i, 0))
    return pl.pallas_call(
        functools.partial(_gdn_gates_kernel, group=group, n_pad=n_pad),
        grid=(m // r,),
        in_specs=[pl.BlockSpec((r, LANES), lambda i: (i, tail_blk)),
                  pl.BlockSpec((1, DN_HEADS), lambda i: (0, 0)),
                  pl.BlockSpec((1, DN_HEADS), lambda i: (0, 0))],
        out_specs=[ospec, ospec, ospec],
        out_shape=[out, out, out],
        compiler_params=_cparams(("parallel",)),
        name="gdn_gates",
    )(z, a_log.reshape(1, DN_HEADS), dt_bias.reshape(1, DN_HEADS))


def _unit_lower_inverses(lowers, n_sq):
    c = lowers[0].shape[0]
    eye = (lax.broadcasted_iota(jnp.int32, (c, c), 0) == lax.broadcasted_iota(jnp.int32, (c, c), 1)).astype(F32)
    p = [-x for x in lowers]
    t = [eye + x for x in p]
    ps = [_split_bf16(x) for x in p]
    for _ in range(n_sq):
        p = [_dot_split(x, x) for x in ps]
        ps = [_split_bf16(x) for x in p]
        t = [a + _dot_split(_split_bf16(a), b) for a, b in zip(t, ps)]
    return t


def _gdn_kernel(xq_ref, xk_ref, xv_ref, wq_ref, wk_ref, wv_ref, gate_ref, nw_ref, beta_ref, gc_ref, gl_ref,
                gcrow_ref, s_in_ref, o_ref, s_out_ref, xs_scr, yq_scr, yk_scr, yv_scr,
                *, chunk, sub, per_chunk_state, n_pad):
    c = chunk
    r = xq_ref.shape[0]
    nchunk = r // c
    n_sub = c // sub
    n_heads = xq_ref.shape[1] // DN_DK
    head0 = pl.program_id(1) * n_heads
    first = pl.program_id(2) == 0

    @pl.when(first)
    def _():
        s_out_ref[...] = s_in_ref[...]
        xs_scr[:, 0:SUBLANES, :] = jnp.zeros((3, SUBLANES, n_heads * DN_DK), F32)

    row = lax.broadcasted_iota(jnp.int32, (r, 1), 0)
    real = (row % sub) >= n_pad
    for idx, (x_ref, w_ref, y_scr) in enumerate(((xq_ref, wq_ref, yq_scr), (xk_ref, wk_ref, yk_scr),
                                                 (xv_ref, wv_ref, yv_scr))):
        xs_scr[idx, SUBLANES:SUBLANES + r, :] = x_ref[...]
        w = w_ref[...]
        y = jnp.zeros((r, n_heads * DN_DK), F32)
        for j in range(DN_CONV):
            off = SUBLANES - (DN_CONV - 1) + j
            y = y + xs_scr[idx, off:off + r, :] * w[j:j + 1, :]
        xs_scr[idx, 0:SUBLANES, :] = xs_scr[idx, r:r + SUBLANES, :]
        y_scr[...] = jnp.where(real, _silu(y), 0.0)

    lane = lax.broadcasted_iota(jnp.int32, (r, DN_HEADS), 1)

    def col(ref, head):
        return jnp.sum(jnp.where(lane == head, ref[...], 0.0), axis=1, keepdims=True)

    cols = [(col(beta_ref, head0 + hh), col(gc_ref, head0 + hh), col(gl_ref, head0 + hh)) for hh in range(n_heads)]

    ri = lax.broadcasted_iota(jnp.int32, (c, c), 0)
    ci = lax.broadcasted_iota(jnp.int32, (c, c), 1)
    same = (ri // sub) == (ci // sub)
    causal = same & (ri >= ci)
    strict = same & (ri > ci)
    n_sq = int(math.log2(sub)) - 1
    nw = nw_ref[...]

    lanes = [(ch, hh) for ch in range(nchunk) for hh in range(n_heads)]
    rows_of = lambda x, ch: x[ch * c:(ch + 1) * c, :]
    q = [yq_scr[ch * c:(ch + 1) * c, hh * DN_DK:(hh + 1) * DN_DK] for ch, hh in lanes]
    k = [yk_scr[ch * c:(ch + 1) * c, hh * DN_DK:(hh + 1) * DN_DK] for ch, hh in lanes]
    v = [yv_scr[ch * c:(ch + 1) * c, hh * DN_DV:(hh + 1) * DN_DV] for ch, hh in lanes]
    q = [x * lax.rsqrt(jnp.sum(x * x, axis=-1, keepdims=True) + RMS_EPS) * (DN_DK ** -0.5) for x in q]
    k = [x * lax.rsqrt(jnp.sum(x * x, axis=-1, keepdims=True) + RMS_EPS) for x in k]
    beta = [rows_of(cols[hh][0], ch) for ch, hh in lanes]
    gc = [rows_of(cols[hh][1], ch) for ch, hh in lanes]
    gl = [rows_of(cols[hh][2], ch) for ch, hh in lanes]
    gcrow = [gcrow_ref[hh:hh + 1, ch * c:(ch + 1) * c] for ch, hh in lanes]
    decay = [jnp.where(causal, jnp.exp(jnp.where(causal, g - gr, 0.0)), 0.0) for g, gr in zip(gc, gcrow)]
    kb = [x * b for x, b in zip(k, beta)]
    kbf = [x.astype(BF16) for x in k]
    lower = [jnp.where(strict, _dot_nt(a.astype(BF16), b) * d, 0.0) for a, b, d in zip(kb, kbf, decay)]
    tmat = _unit_lower_inverses(lower, n_sq)
    eg = [jnp.exp(g) for g in gc]
    uw = [_dot(t.astype(BF16), jnp.concatenate([x * b, y * e], axis=1).astype(BF16))
          for t, x, b, y, e in zip(tmat, v, beta, kb, eg)]
    u = [x[:, :DN_DV] for x in uw]
    wb = [x[:, DN_DV:].astype(BF16) for x in uw]
    intra = [jnp.where(causal, _dot_nt(a.astype(BF16), b) * d, 0.0).astype(BF16) for a, b, d in zip(q, kbf, decay)]
    qe = [(a * e).astype(BF16) for a, e in zip(q, eg)]
    kd = [(a * jnp.exp(gl_ - g)).astype(BF16) for a, gl_, g in zip(k, gl, gc)]

    heads = range(n_heads)
    for ch in range(nchunk):
        lane_of = [ch * n_heads + hh for hh in heads]
        v_new_parts = [[] for _ in heads]
        o_parts = [[] for _ in heads]
        for s in range(n_sub):
            a0 = s * sub
            sidx = (ch * n_sub + s) if per_chunk_state else 0
            state = [s_out_ref[sidx, hh] for hh in heads]
            sb = [x.astype(BF16) for x in state]
            v_new = [u[i][a0:a0 + sub, :] - _dot(wb[i][a0:a0 + sub, :], b) for i, b in zip(lane_of, sb)]
            for hh, i in enumerate(lane_of):
                o_parts[hh].append(_dot(qe[i][a0:a0 + sub, :], sb[hh]))
            for hh, i in enumerate(lane_of):
                s_out_ref[sidx, hh] = (state[hh] * jnp.exp(gl[i][a0:a0 + 1, :])
                                       + _dot_tn(kd[i][a0:a0 + sub, :], v_new[hh].astype(BF16)))
                v_new_parts[hh].append(v_new[hh])
        for hh, i in enumerate(lane_of):
            v_all = v_new_parts[hh][0] if n_sub == 1 else jnp.concatenate(v_new_parts[hh], axis=0)
            o_state = o_parts[hh][0] if n_sub == 1 else jnp.concatenate(o_parts[hh], axis=0)
            o = o_state + _dot(intra[i], v_all.astype(BF16))
            o = o * lax.rsqrt(jnp.mean(o * o, axis=-1, keepdims=True) + RMS_EPS) * nw
            r0, h0 = ch * c, hh * DN_DV
            o_ref[r0:r0 + c, h0:h0 + DN_DV] = o * _silu(gate_ref[r0:r0 + c, h0:h0 + DN_DV])


def _gdn_call(x, x_col0, conv_w, gate, gate_col0, norm_w, beta, gc, gl, gcrow, s_in, layer, *, rows_per_seq, chunk,
              sub, per_chunk_state, n_pad):
    m = x.shape[0]
    r = min(512, rows_per_seq) if not per_chunk_state else min(512, m)
    chunk = min(chunk, r)
    n_seq = m // rows_per_seq if not per_chunk_state else m // r
    nt = rows_per_seq // r if not per_chunk_state else 1
    n_state_blk = r // sub if per_chunk_state else 1
    hp = GDN_HEADS_PER_STEP
    wide = hp * DN_DK
    xb = x_col0 // wide
    gb = gate_col0 // wide
    hw = DN_WIDTH // wide
    assert x_col0 % wide == 0 and gate_col0 % wide == 0 and DN_HEADS % hp == 0

    def rowblk(b, h, t):
        return b * nt + t

    xspec = lambda off: pl.BlockSpec((r, wide), lambda b, h, t: (rowblk(b, h, t), xb + off + h))
    wspec = lambda off: pl.BlockSpec((DN_CONV, wide), lambda b, h, t: (0, off + h))
    colspec = pl.BlockSpec((r, DN_HEADS), lambda b, h, t: (rowblk(b, h, t), 0))
    sspec = pl.BlockSpec((n_state_blk, hp, DN_DK, DN_DV), lambda b, h, t: (b, h, 0, 0))
    return pl.pallas_call(
        functools.partial(_gdn_kernel, chunk=chunk, sub=sub, per_chunk_state=per_chunk_state, n_pad=n_pad),
        grid=(n_seq, DN_HEADS // hp, nt),
        in_specs=[xspec(0), xspec(hw), xspec(2 * hw), wspec(0), wspec(hw), wspec(2 * hw),
                  pl.BlockSpec((r, wide), lambda b, h, t: (rowblk(b, h, t), gb + h)),
                  pl.BlockSpec((1, DN_DV), lambda b, h, t: (0, 0)),
                  colspec, colspec, colspec,
                  pl.BlockSpec((None, hp, r), lambda b, h, t: (h, 0, rowblk(b, h, t))),
                  pl.BlockSpec((None, n_state_blk, hp, DN_DK, DN_DV), lambda b, h, t: (layer, b, h, 0, 0))],
        out_specs=[pl.BlockSpec((r, wide), lambda b, h, t: (rowblk(b, h, t), h)), sspec],
        out_shape=[jax.ShapeDtypeStruct((m, DN_WIDTH), F32), jax.ShapeDtypeStruct(s_in.shape[1:], F32)],
        scratch_shapes=[pltpu.VMEM((3, r + SUBLANES, wide), F32), pltpu.VMEM((r, wide), F32),
                        pltpu.VMEM((r, wide), F32), pltpu.VMEM((r, wide), F32)],
        compiler_params=_cparams(("parallel", "parallel", "arbitrary")),
        name="gdn_delta",
    )(x, x, x, conv_w, conv_w, conv_w, gate, norm_w.reshape(1, DN_DV), beta, gc, gl,
      gcrow.reshape(DN_HEADS // hp, hp, m), s_in)


def _pq_kernel(x_ref, w_ref, o_ref):
    for kv in range(NSA_KV_HEADS):
        acc = jnp.zeros((x_ref.shape[0], 2 * HEAD_DIM), F32)
        for j in range(CMP_STRIDE):
            c0 = j * KV_WIDTH + kv * HEAD_DIM
            acc = acc + _dot(x_ref[:, c0:c0 + HEAD_DIM].astype(BF16), w_ref[j])
        o_ref[:, kv * 2 * HEAD_DIM:(kv + 1) * 2 * HEAD_DIM] = acc


def _pq_weights(w1):
    return jnp.concatenate([w1[:CMP_STRIDE], w1[CMP_STRIDE:]], axis=-1).astype(BF16)


def _pq_call(chunks, wpq):
    nc, width = chunks.shape
    tc = 512 if nc % 512 == 0 else nc
    return pl.pallas_call(
        _pq_kernel,
        grid=(nc // tc,),
        in_specs=[pl.BlockSpec((tc, width), lambda i: (i, 0)),
                  pl.BlockSpec(wpq.shape, lambda i: (0, 0, 0))],
        out_specs=pl.BlockSpec((tc, 4 * HEAD_DIM), lambda i: (i, 0)),
        out_shape=jax.ShapeDtypeStruct((nc, 4 * HEAD_DIM), F32),
        compiler_params=_cparams(("parallel",)),
        name="nsa_compress_pq",
    )(chunks, wpq)


def _pq_pool_kernel(*refs):
    x_refs, w_ref, o_ref = refs[:CMP_STRIDE], refs[CMP_STRIDE], refs[CMP_STRIDE + 1]
    rows = o_ref.shape[1]
    acc = jnp.zeros((rows, 2 * HEAD_DIM), F32)
    for j in range(CMP_STRIDE):
        acc = acc + _dot(x_refs[j][...].reshape(rows, HEAD_DIM).astype(BF16), w_ref[j])
    o_ref[0] = acc[:, :HEAD_DIM]
    o_ref[1] = acc[:, HEAD_DIM:]


def _pq_pool_call(pool, layer, wpq):
    depth, n_phys, page, n_kv, hd = pool.shape
    nc = n_phys * (page // CMP_STRIDE)
    chunks = pool.reshape(depth, nc, CMP_STRIDE, n_kv, hd)
    tc = 512 if nc % 512 == 0 else nc
    xspec = lambda j: pl.BlockSpec((None, tc, None, n_kv, hd), lambda i: (layer, i, j, 0, 0))
    return pl.pallas_call(
        _pq_pool_kernel,
        grid=(nc // tc,),
        in_specs=[xspec(j) for j in range(CMP_STRIDE)] + [pl.BlockSpec(wpq.shape, lambda i: (0, 0, 0))],
        out_specs=pl.BlockSpec((2, tc * n_kv, HEAD_DIM), lambda i: (0, i, 0)),
        out_shape=jax.ShapeDtypeStruct((2, nc * n_kv, HEAD_DIM), F32),
        compiler_params=_cparams(("parallel",)),
        name="nsa_compress_pool",
    )(*([chunks] * CMP_STRIDE), wpq)


def _pe_term(pe, wpq):
    z = jnp.zeros((CMP_STRIDE, HEAD_DIM), F32)
    rows = jnp.stack([jnp.concatenate([pe[:CMP_STRIDE], z], axis=1).reshape(-1),
                      jnp.concatenate([pe[CMP_STRIDE:], z], axis=1).reshape(-1)])
    rows = jnp.concatenate([rows, jnp.zeros((SUBLANES - 2, rows.shape[1]), F32)], axis=0)
    pq = _pq_call(rows, wpq)
    return (pq[0, :HEAD_DIM] + pq[1, HEAD_DIM:2 * HEAD_DIM]).reshape(1, HEAD_DIM)


def _masked_softmax(s, valid):
    s = jnp.where(valid, s, NEG_INF)
    m = jnp.max(s, axis=-1, keepdims=True)
    e = jnp.exp(s - m)
    p = e / jnp.sum(e, axis=-1, keepdims=True)
    return jnp.where(valid, p, 0.0)


def _overlap_matrix(n_cmp, n_blk, transposed=False):
    shape, n_dim, j_dim = ((n_blk, n_cmp), 1, 0) if transposed else ((n_cmp, n_blk), 0, 1)
    n = lax.broadcasted_iota(jnp.int32, shape, n_dim) * CMP_STRIDE
    j = lax.broadcasted_iota(jnp.int32, shape, j_dim) * SEL_BLOCK
    ov = jnp.maximum(jnp.minimum(n + CMP_LEN, j + SEL_BLOCK) - jnp.maximum(n, j), 0)
    return ov.astype(F32) / CMP_LEN


AUG_HI = SEL_BLOCK
AUG_LO = SEL_BLOCK + 1


def _key_aug(pos, with_blocks):
    lane = lax.broadcasted_iota(jnp.int32, pos.shape, 1)
    hi = pos // SEL_BLOCK
    aug = jnp.where(lane == AUG_HI, hi, jnp.where(lane == AUG_LO, pos % SEL_BLOCK, 0))
    if with_blocks:
        aug = jnp.where(lane < SEL_BLOCK, jnp.where(hi == lane, 1, 0), aug)
    return aug.astype(F32).astype(BF16)


def _select_bias_t(imp_t, tpos_row):
    n_blk, t = imp_t.shape
    j = lax.broadcasted_iota(jnp.int32, (n_blk, t), 0)
    q_blk = tpos_row // SEL_BLOCK
    forced = (j == 0) | (j == q_blk) | (j == q_blk - 1)
    score = jnp.where(forced, FORCE_SCORE, jnp.where(j <= q_blk, imp_t, -1.0))
    rank = jnp.zeros((n_blk, t), F32)
    for jj in range(n_blk):
        sj = score[jj:jj + 1, :]
        ahead = (sj > score) | ((sj == score) & (j > jj))
        rank = rank + jnp.where(ahead, 1.0, 0.0)
    return jnp.where(rank < N_SEL, 0.0, NEG_INF)


def _select_bias(imp, tpos):
    t, n_blk = imp.shape
    j = lax.broadcasted_iota(jnp.int32, (t, n_blk), 1)
    q_blk = tpos // SEL_BLOCK
    forced = (j == 0) | (j == q_blk) | (j == q_blk - 1)
    score = jnp.where(forced, FORCE_SCORE, jnp.where(j <= q_blk, imp, -1.0))
    rank = jnp.zeros((t, n_blk), F32)
    for jj in range(n_blk):
        sj = score[:, jj:jj + 1]
        ahead = (sj > score) | ((sj == score) & (j > jj))
        rank = rank + jnp.where(ahead, 1.0, 0.0)
    return jnp.where(rank < N_SEL, 0.0, NEG_INF)


def _select_bias_few(imp, tpos):
    t, n_blk = imp.shape
    j = lax.broadcasted_iota(jnp.int32, (t, n_blk), 1)
    q_blk = tpos // SEL_BLOCK
    forced = (j == 0) | (j == q_blk) | (j == q_blk - 1)
    score = jnp.where(forced, FORCE_SCORE, jnp.where(j <= q_blk, imp, -1.0))
    assert t <= SUBLANES and n_blk <= LANES
    tile = jnp.concatenate([score, jnp.zeros((t, LANES - n_blk), F32)], axis=1)
    tile = jnp.concatenate([tile, jnp.zeros((SUBLANES - t, LANES), F32)], axis=0)
    score_t = tile.T[:n_blk, :]
    jj = lax.broadcasted_iota(jnp.int32, (n_blk, n_blk), 0)
    jl = lax.broadcasted_iota(jnp.int32, (n_blk, n_blk), 1)
    ranks = []
    for i in range(t):
        other = score_t[:, i:i + 1]
        mine = score[i:i + 1, :]
        ahead = (other > mine) | ((other == mine) & (jj < jl))
        ranks.append(jnp.sum(jnp.where(ahead, 1.0, 0.0), axis=0, keepdims=True))
    rank = jnp.concatenate(ranks, axis=0)
    return jnp.where(rank < N_SEL, 0.0, NEG_INF)


def _block_onehot(n_blk, k0, tk):
    j = lax.broadcasted_iota(jnp.int32, (n_blk, tk), 0)
    s = lax.broadcasted_iota(jnp.int32, (n_blk, tk), 1) + k0
    return jnp.where(s // SEL_BLOCK == j, 1.0, 0.0).astype(BF16)


def _compressed_tokens(hid, w2_ref):
    return _dot(_silu(hid).astype(BF16), w2_ref[...])


def _cmp_branch(q4b, cmpk, cmpv, tpos4, slope4, tq):
    n_cmp = cmpk.shape[0]
    scale = HEAD_DIM ** -0.5
    s = _dot_nt(q4b, cmpk.astype(BF16)) * scale
    c_end = lax.broadcasted_iota(jnp.int32, (1, n_cmp), 1) * CMP_STRIDE + (CMP_LEN - 1)
    dist = tpos4 - c_end
    valid = dist >= 0
    s = s - slope4 * dist.astype(F32)
    p = _masked_softmax(s, valid)
    o = _dot(p.astype(BF16), cmpv.astype(BF16))
    psum = p[0:tq]
    for g in range(1, NSA_GROUP):
        psum = psum + p[g * tq:(g + 1) * tq]
    return o, psum


def _slopes(kv, tq):
    g = lax.broadcasted_iota(jnp.int32, (NSA_GROUP * tq, 1), 0) // tq
    head = (g + kv * NSA_GROUP + 1).astype(F32)
    return jnp.exp(head * (-(8.0 / NSA_HEADS) * math.log(2.0)))


def _tail_col(tail, idx):
    lane = lax.broadcasted_iota(jnp.int32, tail.shape, 1)
    return jnp.sum(jnp.where(lane == idx, tail, 0.0), axis=1, keepdims=True)


def _combine_branches(tail, kv, o_cmp, o_slc, o_win, tq):
    outs = []
    for g in range(NSA_GROUP):
        head = kv * NSA_GROUP + g
        acc = jnp.zeros((tq, HEAD_DIM), F32)
        for br, o in enumerate((o_cmp, o_slc, o_win)):
            gate = _sigmoid(_tail_col(tail, TAIL_NSAG + br * NSA_HEADS + head))
            acc = acc + gate * o[g * tq:(g + 1) * tq]
        outs.append(acc)
    return outs


def _nsa_prompt_kernel(q_ref, pqk_ref, pqv_ref, pek_ref, pev_ref, w2k_ref, w2v_ref, sk_ref, sv_ref, wk_ref, wv_ref,
                       tail_ref, o_ref, ccat_scr, cmpv_scr, kcat_scr, sv_scr, wcat_scr, wv_scr, *, tk):
    tq = q_ref.shape[0]
    t_len = sk_ref.shape[0]
    n_cmp = pqk_ref.shape[0]
    n_blk = t_len // SEL_BLOCK
    kv = pl.program_id(1)
    i = pl.program_id(2)
    t0 = i * tq
    scale = HEAD_DIM ** -0.5

    @pl.when(i == 0)
    def _():
        kaug = _key_aug(lax.broadcasted_iota(jnp.int32, (t_len, LANES), 0), True)
        for cat_scr, k_ref, vb_scr, v_ref in ((kcat_scr, sk_ref, sv_scr, sv_ref), (wcat_scr, wk_ref, wv_scr, wv_ref)):
            cat_scr[:, 0:HEAD_DIM] = k_ref[...].astype(BF16)
            cat_scr[:, HEAD_DIM:] = kaug
            vb_scr[...] = v_ref[...].astype(BF16)
        tokens = []
        for pq_ref, pe_ref, w2_ref in ((pqk_ref, pek_ref, w2k_ref), (pqv_ref, pev_ref, w2v_ref)):
            p = pq_ref[:, 0:HEAD_DIM]
            qn = pltpu.roll(pq_ref[:, HEAD_DIM:2 * HEAD_DIM], n_cmp - 1, 0)
            tokens.append(_compressed_tokens(p + qn + pe_ref[...], w2_ref).astype(BF16))
        c_end = lax.broadcasted_iota(jnp.int32, (n_cmp, LANES), 0) * CMP_STRIDE + (CMP_LEN - 1)
        ccat_scr[:, 0:HEAD_DIM] = tokens[0]
        ccat_scr[:, HEAD_DIM:] = _key_aug(c_end, False)
        cmpv_scr[...] = tokens[1]

    qb = q_ref[...] * scale
    q4b = jnp.concatenate([qb[:, g * HEAD_DIM:(g + 1) * HEAD_DIM] for g in range(NSA_GROUP)], axis=0).astype(BF16)
    rows4 = NSA_GROUP * tq
    tpos4 = t0 + lax.broadcasted_iota(jnp.int32, (rows4, 1), 0) % tq
    slope4 = _slopes(kv, tq)
    lane = lax.broadcasted_iota(jnp.int32, (rows4, LANES), 1)
    alibi = jnp.where(lane == AUG_HI, slope4 * SEL_BLOCK, jnp.where(lane == AUG_LO, slope4, 0.0))
    q_plain = jnp.concatenate([q4b, alibi.astype(BF16)], axis=1)

    c_end = lax.broadcasted_iota(jnp.int32, (1, n_cmp), 1) * CMP_STRIDE + (CMP_LEN - 1)
    p = _masked_softmax(_dot_nt(q_plain, ccat_scr[...]), c_end <= tpos4)
    o_cmp = _dot(p.astype(BF16), cmpv_scr[...])
    psum = p[0:tq]
    for g in range(1, NSA_GROUP):
        psum = psum + p[g * tq:(g + 1) * tq]

    band = min(WINDOW + tq, t_len)
    b0 = pl.multiple_of(jnp.maximum(t0 + tq - band, 0), BF16_ROWS)
    dist = tpos4 - (b0 + lax.broadcasted_iota(jnp.int32, (1, band), 1))
    pw = _masked_softmax(_dot_nt(q_plain, wcat_scr[pl.ds(b0, band), :]), (dist >= 0) & (dist < WINDOW))
    o_win = _dot(pw.astype(BF16), wv_scr[pl.ds(b0, band), :])

    imp_t =_dot_nt(_overlap_matrix(n_cmp, n_blk, transposed=True), psum, HIGHEST)
    tpos_row = t0 + lax.broadcasted_iota(jnp.int32, (1, tq), 1)
    selb = _select_bias_t(imp_t, tpos_row).T
    selb = jnp.concatenate([selb, jnp.zeros((tq, LANES - n_blk), F32)], axis=1)
    selb4 = jnp.concatenate([selb] * NSA_GROUP, axis=0)
    q_sel = jnp.concatenate([q4b, jnp.where(lane < SEL_BLOCK, selb4, alibi).astype(BF16)], axis=1)

    def slc_tile(jt, carry, diagonal):
        m, l, acc = carry
        k0 = pl.multiple_of(jt * tk, tk)
        s = _dot_nt(q_sel, kcat_scr[pl.ds(k0, tk), :])
        if diagonal:
            s = jnp.where(k0 + lax.broadcasted_iota(jnp.int32, (1, tk), 1) <= tpos4, s, NEG_INF)
        m_new = jnp.maximum(m, jnp.max(s, axis=-1, keepdims=True))
        alpha = jnp.exp(m - m_new)
        e = jnp.exp(s - m_new)
        l = alpha * l + jnp.sum(e, axis=-1, keepdims=True)
        acc = alpha * acc + _dot(e.astype(BF16), sv_scr[pl.ds(k0, tk), :])
        return m_new, l, acc

    n_full = t0 // tk
    init = (jnp.full((rows4, 1), NEG_INF, F32), jnp.zeros((rows4, 1), F32), jnp.zeros((rows4, HEAD_DIM), F32))
    carry = lax.fori_loop(0, n_full, lambda jt, c: slc_tile(jt, c, False), init)
    _, l, acc = slc_tile(n_full, carry, True)
    o_slc = acc / l

    outs = _combine_branches(tail_ref[...], kv, o_cmp, o_slc, o_win, tq)
    for g in range(NSA_GROUP):
        o_ref[:, g * HEAD_DIM:(g + 1) * HEAD_DIM] = outs[g]


def _nsa_prompt_call(z, pqk, pqv, pek, pev, w2k, w2v, n_batch, t_len):
    tq = min(256, t_len)
    tk = min(512, t_len)
    nq = t_len // tq
    n_cmp = t_len // CMP_STRIDE
    qblk0 = OFF_NQ // (NSA_GROUP * HEAD_DIM)
    assert t_len <= SEL_BLOCK * SEL_BLOCK and 8 % NSA_HEADS == 0 and tk % tq == 0
    rowspec = lambda off: pl.BlockSpec((t_len, HEAD_DIM), lambda b, kv, i: (b, off // HEAD_DIM + kv))
    pqspec = pl.BlockSpec((n_cmp, 2 * HEAD_DIM), lambda b, kv, i: (b, kv))
    cspec = lambda shape: pl.BlockSpec(shape, lambda b, kv, i: (0, 0))
    return pl.pallas_call(
        functools.partial(_nsa_prompt_kernel, tk=tk),
        grid=(n_batch, NSA_KV_HEADS, nq),
        in_specs=[pl.BlockSpec((tq, NSA_GROUP * HEAD_DIM), lambda b, kv, i: (b * nq + i, qblk0 + kv)),
                  pqspec, pqspec, cspec((1, HEAD_DIM)), cspec((1, HEAD_DIM)),
                  cspec((HEAD_DIM, HEAD_DIM)), cspec((HEAD_DIM, HEAD_DIM)),
                  rowspec(OFF_ROWS + 2 * KV_WIDTH), rowspec(OFF_ROWS + 3 * KV_WIDTH),
                  rowspec(OFF_ROWS + 4 * KV_WIDTH), rowspec(OFF_ROWS + 5 * KV_WIDTH),
                  pl.BlockSpec((tq, LANES), lambda b, kv, i: (b * nq + i, OFF_TAIL // LANES))],
        out_specs=pl.BlockSpec((tq, NSA_GROUP * HEAD_DIM), lambda b, kv, i: (b * nq + i, kv)),
        out_shape=jax.ShapeDtypeStruct((n_batch * t_len, NSA_WIDTH), F32),
        scratch_shapes=[pltpu.VMEM((n_cmp, 2 * HEAD_DIM), BF16), pltpu.VMEM((n_cmp, HEAD_DIM), BF16),
                        pltpu.VMEM((t_len, 2 * HEAD_DIM), BF16), pltpu.VMEM((t_len, HEAD_DIM), BF16),
                        pltpu.VMEM((t_len, 2 * HEAD_DIM), BF16), pltpu.VMEM((t_len, HEAD_DIM), BF16)],
        compiler_params=_cparams(("parallel", "parallel", "arbitrary")),
        name="nsa_prompt",
    )(z, pqk, pqv, pek, pev, w2k, w2v, z, z, z, z, z)


def _nsa_sample_kernel(pt_ref, q_ref, new_ref, pqnk_ref, pqnv_ref, pek_ref, pev_ref, w2k_ref, w2v_ref, wink_ref,
                       winv_ref, tail_ref, pqk_hbm, pqv_hbm, sk_hbm, sv_hbm, o_ref,
                       pqk_buf, pqv_buf, sk_buf, sv_buf, sem, *, n_pages, past_len, layer):
    b = pl.program_id(0)
    nb = pl.num_programs(0)
    n_seq, t_new = q_ref.shape[0], q_ref.shape[1]
    page = PAGE_ROWS
    cpp = page // CMP_STRIDE
    new_rows = LANES
    n_keys = past_len + new_rows
    n_cmp = pqk_buf.shape[3]
    n_blk = SEL_BLOCK
    scale = HEAD_DIM ** -0.5
    ppr = cpp * NSA_KV_HEADS
    kpr = page * NSA_KV_HEADS

    def copies(step, slot):
        out = []
        for sq in range(n_seq):
            for p in range(n_pages):
                pg = pt_ref[step * n_seq + sq, p]
                out.append(pltpu.make_async_copy(pqk_hbm.at[:, pl.ds(pg * ppr, ppr)],
                                                 pqk_buf.at[slot, sq, :, pl.ds(p * ppr, ppr)], sem.at[slot, 0]))
                out.append(pltpu.make_async_copy(pqv_hbm.at[:, pl.ds(pg * ppr, ppr)],
                                                 pqv_buf.at[slot, sq, :, pl.ds(p * ppr, ppr)], sem.at[slot, 1]))
                out.append(pltpu.make_async_copy(sk_hbm.at[layer, pg], sk_buf.at[slot, sq, pl.ds(p * kpr, kpr)],
                                                 sem.at[slot, 2]))
                out.append(pltpu.make_async_copy(sv_hbm.at[layer, pg], sv_buf.at[slot, sq, pl.ds(p * kpr, kpr)],
                                                 sem.at[slot, 3]))
        return out

    slot = b % 2

    @pl.when(b == 0)
    def _():
        for cp in copies(0, 0):
            cp.start()

    @pl.when(b + 1 < nb)
    def _():
        for cp in copies(b + 1, 1 - slot):
            cp.start()

    new = new_ref[...]
    qb = q_ref[...]
    tq = t_new
    rows4 = NSA_GROUP * tq
    tpos4 = past_len + lax.broadcasted_iota(jnp.int32, (rows4, 1), 0) % tq
    tpos = past_len + lax.broadcasted_iota(jnp.int32, (tq, 1), 0)

    win_len = wink_ref.shape[1] // NSA_KV_HEADS
    band = win_len + new_rows

    def new_tile(which, sq, kv):
        c0 = which * KV_WIDTH + kv * HEAD_DIM
        return jnp.concatenate([new[sq, :, c0:c0 + HEAD_DIM], jnp.zeros((new_rows - t_new, HEAD_DIM), F32)],
                               axis=0).astype(BF16)

    for cp in copies(b, slot):
        cp.wait()

    n_pool = n_pages * cpp
    rown = lax.broadcasted_iota(jnp.int32, (n_pool, 1), 0)
    onehot = _block_onehot(n_blk, 0, n_keys)
    overlap = _overlap_matrix(n_cmp, n_blk)
    rowt = lax.broadcasted_iota(jnp.int32, (n_cmp - n_pool, 1), 0)
    lanes = [(sq, kv) for sq in range(n_seq) for kv in range(NSA_KV_HEADS)]
    ln = range(len(lanes))
    q4b = [jnp.concatenate([qb[sq, :, (kv * NSA_GROUP + g) * HEAD_DIM:(kv * NSA_GROUP + g + 1) * HEAD_DIM]
                            for g in range(NSA_GROUP)], axis=0).astype(BF16) for sq, kv in lanes]
    slope4 = [_slopes(kv, tq) for _, kv in lanes]
    kv_rows = [pl.ds(kv, n_pool, stride=NSA_KV_HEADS) for _, kv in lanes]
    cmp_tokens = []
    for pq_buf, pqn_ref, pe_ref, w2_ref in ((pqk_buf, pqnk_ref, pek_ref, w2k_ref), (pqv_buf, pqnv_ref, pev_ref, w2v_ref)):
        p = [pq_buf[slot, sq, 0, kv_rows[i], :] for i, (sq, kv) in enumerate(lanes)]
        qn = [pltpu.roll(pq_buf[slot, sq, 1, kv_rows[i], :], n_pool - 1, 0) for i, (sq, kv) in enumerate(lanes)]
        pn = [pqn_ref[sq, :, kv * 2 * HEAD_DIM:kv * 2 * HEAD_DIM + HEAD_DIM] for sq, kv in lanes]
        qnn = [pqn_ref[sq, :, kv * 2 * HEAD_DIM + HEAD_DIM:(kv + 1) * 2 * HEAD_DIM] for sq, kv in lanes]
        hid = [jnp.concatenate([p[i] + jnp.where(rown == n_pool - 1, qnn[i], qn[i]),
                                jnp.where(rowt == 0, pn[i], 0.0)], axis=0) + pe_ref[...] for i in ln]
        cmp_tokens.append([_compressed_tokens(h, w2_ref) for h in hid])
    cmp_out = [_cmp_branch(q4b[i], cmp_tokens[0][i], cmp_tokens[1][i], tpos4, slope4[i], tq) for i in ln]

    kw = [wink_ref[sq, pl.ds(kv, win_len, stride=NSA_KV_HEADS), :].astype(BF16) for sq, kv in lanes]
    vw = [winv_ref[sq, pl.ds(kv, win_len, stride=NSA_KV_HEADS), :].astype(BF16) for sq, kv in lanes]
    dist_w = tpos4 - ((past_len - win_len) + lax.broadcasted_iota(jnp.int32, (1, band), 1))
    valid_w = (dist_w >= 0) & (dist_w < WINDOW)
    s = [jnp.concatenate([_dot_nt(q4b[i], kw[i]), _dot_nt(q4b[i], new_tile(4, sq, kv))], axis=1) * scale
         for i, (sq, kv) in enumerate(lanes)]
    pw = [_masked_softmax(s[i] - slope4[i] * dist_w.astype(F32), valid_w).astype(BF16) for i in ln]
    o_win = [_dot(pw[i][:, :win_len], vw[i]) + _dot(pw[i][:, win_len:], new_tile(5, sq, kv))
             for i, (sq, kv) in enumerate(lanes)]

    imp = [_dot(cmp_out[i][1], overlap, HIGHEST) for i in ln]
    selb = [_select_bias_few(x, tpos).astype(BF16) for x in imp]
    selb4 = [jnp.concatenate([x] * NSA_GROUP, axis=0) for x in selb]
    ks = [sk_buf[slot, sq, pl.ds(kv, past_len, stride=NSA_KV_HEADS), :].astype(BF16) for sq, kv in lanes]
    vs = [sv_buf[slot, sq, pl.ds(kv, past_len, stride=NSA_KV_HEADS), :].astype(BF16) for sq, kv in lanes]
    dist_s = tpos4 - lax.broadcasted_iota(jnp.int32, (1, n_keys), 1)
    s = [jnp.concatenate([_dot_nt(q4b[i], ks[i]), _dot_nt(q4b[i], new_tile(2, sq, kv))], axis=1) * scale
         + _dot(selb4[i], onehot) for i, (sq, kv) in enumerate(lanes)]
    ps = [_masked_softmax(s[i] - slope4[i] * dist_s.astype(F32), dist_s >= 0).astype(BF16) for i in ln]
    o_slc = [_dot(ps[i][:, :past_len], vs[i]) + _dot(ps[i][:, past_len:], new_tile(3, sq, kv))
             for i, (sq, kv) in enumerate(lanes)]

    for i, (sq, kv) in enumerate(lanes):
        outs = _combine_branches(tail_ref[sq], kv, cmp_out[i][0], o_slc[i], o_win[i], tq)
        for g in range(NSA_GROUP):
            h0 = (kv * NSA_GROUP + g) * HEAD_DIM
            o_ref[sq, :, h0:h0 + HEAD_DIM] = outs[g]


PAGE_ROWS = 128


def _kv_rows(x):
    return x.reshape(x.shape[:-3] + (x.shape[-3] * x.shape[-2], x.shape[-1]))


def _nsa_sample_call(page_table, z3, pq_new_k, pq_new_v, pek, pev, w2k, w2v, win_k, win_v, pq_pool_k, pq_pool_v,
                     pool_sk, pool_sv, layer):
    bd, t_new, _ = z3.shape
    n_pages = page_table.shape[1]
    past_len = n_pages * PAGE_ROWS
    win_len = win_k.shape[2]
    n_cmp = 2 * n_pages * (PAGE_ROWS // CMP_STRIDE)
    assert (past_len + LANES) // SEL_BLOCK <= SEL_BLOCK and past_len // CMP_STRIDE + 8 <= n_cmp and t_new <= LANES

    ns = NSA_SAMPLE_SEQS_PER_STEP if bd % NSA_SAMPLE_SEQS_PER_STEP == 0 else 1

    def zspec(width, col0):
        return pl.BlockSpec((ns, t_new, width), lambda b, pt: (b, 0, col0 // width))

    cspec = lambda shape: pl.BlockSpec(shape, lambda b, pt: (0,) * len(shape))
    pqnspec = pl.BlockSpec((ns, 1, 4 * HEAD_DIM), lambda b, pt: (b, 0, 0))
    winspec = pl.BlockSpec((None, ns, win_len * NSA_KV_HEADS, HEAD_DIM), lambda b, pt: (layer, b, 0, 0))
    anyspec = pl.BlockSpec(memory_space=pl.ANY)
    grid_spec = pltpu.PrefetchScalarGridSpec(
        num_scalar_prefetch=1,
        grid=(bd // ns,),
        in_specs=[zspec(NSA_WIDTH, OFF_NQ), zspec(6 * KV_WIDTH, OFF_ROWS), pqnspec, pqnspec,
                  cspec((1, HEAD_DIM)), cspec((1, HEAD_DIM)), cspec((HEAD_DIM, HEAD_DIM)), cspec((HEAD_DIM, HEAD_DIM)),
                  winspec, winspec, zspec(LANES, OFF_TAIL), anyspec, anyspec, anyspec, anyspec],
        out_specs=pl.BlockSpec((ns, t_new, NSA_WIDTH), lambda b, pt: (b, 0, 0)),
        scratch_shapes=[pltpu.VMEM((2, ns, 2, n_cmp, HEAD_DIM), F32),
                        pltpu.VMEM((2, ns, 2, n_cmp, HEAD_DIM), F32),
                        pltpu.VMEM((2, ns, past_len * NSA_KV_HEADS, HEAD_DIM), F32),
                        pltpu.VMEM((2, ns, past_len * NSA_KV_HEADS, HEAD_DIM), F32),
                        pltpu.SemaphoreType.DMA((2, 4))],
    )
    return pl.pallas_call(
        functools.partial(_nsa_sample_kernel, n_pages=n_pages, past_len=past_len, layer=layer),
        grid_spec=grid_spec,
        out_shape=jax.ShapeDtypeStruct((bd, t_new, NSA_WIDTH), F32),
        compiler_params=_cparams(("arbitrary",)),
        name="nsa_sample",
    )(page_table, z3, z3, pq_new_k, pq_new_v, pek, pev, w2k, w2v, _kv_rows(win_k), _kv_rows(win_v), z3,
      pq_pool_k, pq_pool_v, _kv_rows(pool_sk), _kv_rows(pool_sv))


def _merge_kernel(ya_ref, yb_ref, ga_ref, gb_ref, wa_ref, wb_ref, wo_ref, x_ref, gate_ref, g_ref, b_ref, o_ref,
                  *, alpha):
    a = _dot(ya_ref[...].astype(BF16), wa_ref[...])
    b = _dot(yb_ref[...].astype(BF16), wb_ref[...])
    mix = _sigmoid(ga_ref[...]) * a + _sigmoid(gb_ref[...]) * b
    y = _dot(mix.astype(BF16), wo_ref[...])
    o_ref[...] = _ln(alpha * x_ref[...] + gate_ref[...] * y) * g_ref[...] + b_ref[...]


def _merge_call(ya, yb, z, wa, wb, wo, layer, x, gate, ln_g, ln_b, rows_per_seq, alpha):
    m, d = x.shape
    tm = min(256, m)
    rspec = lambda w: pl.BlockSpec((tm, w), lambda i: (i, 0))
    cspec = lambda a: pl.BlockSpec((None,) + a.shape[1:], lambda i: (layer, 0, 0))
    return pl.pallas_call(
        functools.partial(_merge_kernel, alpha=alpha),
        grid=(m // tm,),
        in_specs=[rspec(DN_WIDTH), rspec(NSA_WIDTH),
                  pl.BlockSpec((tm, d), lambda i: (i, OFF_MERGE // d)),
                  pl.BlockSpec((tm, d), lambda i: (i, OFF_MERGE // d + 1)),
                  cspec(wa), cspec(wb), cspec(wo), rspec(d), _rowparam_spec(gate, tm, rows_per_seq),
                  pl.BlockSpec((1, d), lambda i: (0, 0)), pl.BlockSpec((1, d), lambda i: (0, 0))],
        out_specs=rspec(d),
        out_shape=jax.ShapeDtypeStruct((m, d), F32),
        compiler_params=_cparams(("parallel",)),
        name="merge_out_ln",
    )(ya, yb, z, z, wa, wb, wo, x, gate, ln_g.reshape(1, d), ln_b.reshape(1, d))


def _ffn_down_kernel(ua_ref, ub_ref, pa_ref, pb_ref, cwa_ref, cwb_ref, wd_ref, x_ref, gate_ref, g_ref, b_ref, o_ref,
                     acc_scr, *, alpha, rows_per_seq):
    i = pl.program_id(0)
    k = pl.program_id(1)
    tm = ua_ref.shape[0]
    seq_start = (i * tm) % rows_per_seq == 0
    row8 = lax.broadcasted_iota(jnp.int32, (SUBLANES, 1), 0)

    def conv(u_ref, p_ref, w_ref):
        u = u_ref[...]
        w = w_ref[...]
        prev = jnp.where(seq_start, 0.0, p_ref[...])
        y = u * w[FFN_CONV - 1:FFN_CONV, :]
        for sh in range(1, FFN_CONV):
            shifted = pltpu.roll(u, sh, 0)
            head = jnp.where(row8 < sh, pltpu.roll(prev, sh, 0), shifted[:SUBLANES])
            shifted = jnp.concatenate([head, shifted[SUBLANES:]], axis=0)
            y = y + shifted * w[FFN_CONV - 1 - sh:FFN_CONV - sh, :]
        return y

    act = _silu(conv(ua_ref, pa_ref, cwa_ref)) * conv(ub_ref, pb_ref, cwb_ref)
    part = _dot(act.astype(BF16), wd_ref[...])

    @pl.when(k == 0)
    def _():
        acc_scr[...] = part

    @pl.when(k > 0)
    def _():
        acc_scr[...] += part

    @pl.when(k == pl.num_programs(1) - 1)
    def _():
        o_ref[...] = _ln(alpha * x_ref[...] + gate_ref[...] * acc_scr[...]) * g_ref[...] + b_ref[...]


def _ffn_down_call(u, conv_w, wd, layer, x, gate, ln_g, ln_b, rows_per_seq, alpha):
    m, d = x.shape
    d_ff = wd.shape[1]
    tm = min(512, m)
    tk = 512
    nk = d_ff // tk
    sub_per_tile = tm // SUBLANES
    prev_idx = lambda i: jnp.maximum(i * sub_per_tile - 1, 0)
    return pl.pallas_call(
        functools.partial(_ffn_down_kernel, alpha=alpha, rows_per_seq=rows_per_seq),
        grid=(m // tm, nk),
        in_specs=[pl.BlockSpec((tm, tk), lambda i, k: (i, k)),
                  pl.BlockSpec((tm, tk), lambda i, k: (i, nk + k)),
                  pl.BlockSpec((SUBLANES, tk), lambda i, k: (prev_idx(i), k)),
                  pl.BlockSpec((SUBLANES, tk), lambda i, k: (prev_idx(i), nk + k)),
                  pl.BlockSpec((FFN_CONV, tk), lambda i, k: (0, k)),
                  pl.BlockSpec((FFN_CONV, tk), lambda i, k: (0, nk + k)),
                  pl.BlockSpec((None, tk, d), lambda i, k: (layer, k, 0)),
                  pl.BlockSpec((tm, d), lambda i, k: (i, 0)),
                  _rowparam_spec(gate, tm, rows_per_seq),
                  pl.BlockSpec((1, d), lambda i, k: (0, 0)), pl.BlockSpec((1, d), lambda i, k: (0, 0))],
        out_specs=pl.BlockSpec((tm, d), lambda i, k: (i, 0)),
        out_shape=jax.ShapeDtypeStruct((m, d), F32),
        scratch_shapes=[pltpu.VMEM((tm, d), F32)],
        compiler_params=_cparams(("parallel", "arbitrary")),
        name="ffn_down_ln",
    )(u, u, u, u, conv_w, conv_w, wd, x, gate, ln_g.reshape(1, d), ln_b.reshape(1, d))


def _ffn_fused_kernel(x_ref, xp_ref, sh_ref, sc_ref, wua_ref, wub_ref, cwa_ref, cwb_ref, wd_ref, gate_ref, g_ref,
                      b_ref, o_ref, ta_ref, tb_ref, h_scr, acc_scr, *, alpha, rows_per_seq):
    i = pl.program_id(0)
    k = pl.program_id(1)
    tm = x_ref.shape[0]
    halo = xp_ref.shape[0]

    @pl.when(k == 0)
    def _():
        seq_start = (i * tm) % rows_per_seq == 0
        mod = lambda x: _ln(x) * (1.0 + sc_ref[...]) + sh_ref[...]
        h_scr[0:halo, :] = jnp.where(seq_start, 0.0, mod(xp_ref[...])).astype(BF16)
        h_scr[halo:, :] = mod(x_ref[...]).astype(BF16)

    h = h_scr[...]

    def branch(w_ref, cw_ref, t_ref):
        u = _dot(h, w_ref[...])
        cw = cw_ref[...]
        y = u * cw[FFN_CONV - 1:FFN_CONV, :]
        for sh in range(1, FFN_CONV):
            y = y + pltpu.roll(u, sh, 0) * cw[FFN_CONV - 1 - sh:FFN_CONV - sh, :]
        t_ref[...] = u[halo + tm - SUBLANES:, :]
        return y[halo:, :]

    act = _silu(branch(wua_ref, cwa_ref, ta_ref)) * branch(wub_ref, cwb_ref, tb_ref)
    part = _dot(act.astype(BF16), wd_ref[...])

    @pl.when(k == 0)
    def _():
        acc_scr[...] = part

    @pl.when(k > 0)
    def _():
        acc_scr[...] += part

    @pl.when(k == pl.num_programs(1) - 1)
    def _():
        o_ref[...] = _ln(alpha * x_ref[...] + gate_ref[...] * acc_scr[...]) * g_ref[...] + b_ref[...]


def _ffn_fused_call(x, shift, scale, w_up, conv_w, wd, layer, gate, ln_g, ln_b, rows_per_seq, alpha):
    m, d = x.shape
    d_ff = wd.shape[1]
    tm = min(512, m)
    tk = 512
    nk = d_ff // tk
    halo = BF16_ROWS
    assert FFN_CONV - 1 <= halo and tm % halo == 0 and rows_per_seq % tm == 0
    prev_idx = lambda i: jnp.maximum(i * (tm // halo) - 1, 0)
    seqspec = lambda p: _rowparam_spec(p, tm, rows_per_seq)
    tail = jax.ShapeDtypeStruct((m // tm * SUBLANES, d_ff), F32)
    tspec = pl.BlockSpec((SUBLANES, tk), lambda i, k: (i, k))
    return pl.pallas_call(
        functools.partial(_ffn_fused_kernel, alpha=alpha, rows_per_seq=rows_per_seq),
        grid=(m // tm, nk),
        in_specs=[pl.BlockSpec((tm, d), lambda i, k: (i, 0)),
                  pl.BlockSpec((halo, d), lambda i, k: (prev_idx(i), 0)),
                  seqspec(shift), seqspec(scale),
                  pl.BlockSpec((None, d, tk), lambda i, k: (layer, 0, k)),
                  pl.BlockSpec((None, d, tk), lambda i, k: (layer, 0, nk + k)),
                  pl.BlockSpec((FFN_CONV, tk), lambda i, k: (0, k)),
                  pl.BlockSpec((FFN_CONV, tk), lambda i, k: (0, nk + k)),
                  pl.BlockSpec((None, tk, d), lambda i, k: (layer, k, 0)),
                  seqspec(gate),
                  pl.BlockSpec((1, d), lambda i, k: (0, 0)), pl.BlockSpec((1, d), lambda i, k: (0, 0))],
        out_specs=[pl.BlockSpec((tm, d), lambda i, k: (i, 0)), tspec, tspec],
        out_shape=[jax.ShapeDtypeStruct((m, d), F32), tail, tail],
        scratch_shapes=[pltpu.VMEM((halo + tm, d), BF16), pltpu.VMEM((tm, d), F32)],
        compiler_params=_cparams(("parallel", "arbitrary")),
        name="ffn_fused",
    )(x, x, shift, scale, w_up, w_up, conv_w, conv_w, wd, gate, ln_g.reshape(1, d), ln_b.reshape(1, d))


def _permute_w_in(w_in):
    d = w_in.shape[-2]
    sizes = (3 * DN_WIDTH, DN_HEADS, DN_HEADS, DN_WIDTH, NSA_WIDTH) + (KV_WIDTH,) * 6 + (3 * NSA_HEADS, 2 * d)
    offs = [0]
    for s in sizes:
        offs.append(offs[-1] + s)
    part = lambda i: w_in[..., offs[i]:offs[i + 1]].astype(BF16)
    cols = [part(0), part(3), part(12), part(4)] + [part(i) for i in range(5, 11)] + [part(1), part(2), part(11)]
    used = sum(c.shape[-1] for c in cols)
    cols.append(jnp.zeros(w_in.shape[:-1] + (IN_PAD_WIDTH - used,), BF16))
    return jnp.concatenate(cols, axis=-1)


def _pad_groups(x, n_pad):
    b, t, c = x.shape
    return jnp.concatenate([jnp.zeros((b, n_pad, c), x.dtype), x], axis=1).reshape(b * (n_pad + t), c)


def kernel(x_prompt, x_sample, cache_cmp_k, cache_cmp_v, cache_slc_k, cache_slc_v, cache_win_k, cache_win_v, state_dn, state_dn_conv, state_ffn_conv, page_table, c_prompt, c_sample, w_ada, b_ada, w_in, dn_conv_w, dn_a_log, dn_dt_bias, dn_norm_w, cmp_k_w1, cmp_k_pe, cmp_k_w2, cmp_v_w1, cmp_v_pe, cmp_v_w2, w_branch_a, w_branch_b, w_out, ln1_g, ln1_b, ffn_w_up, ffn_conv_w, ffn_w_down, ln2_g, ln2_b):
    depth = w_in.shape[0]
    bp, t_len, d = x_prompt.shape
    bd, t_new, _ = x_sample.shape
    n_phys = cache_cmp_k.shape[1]
    win_len = cache_win_k.shape[2]
    d_ff = ffn_w_down.shape[1]
    alpha = (2 * depth) ** 0.25
    grp = SUBLANES
    assert t_new <= grp - (DN_CONV - 1) and t_new >= DN_CONV - 1 and cache_cmp_k.shape[2] == PAGE_ROWS

    xp = x_prompt.reshape(bp * t_len, d)
    xs = x_sample.reshape(bd * t_new, d)
    c_all = jnp.concatenate([c_prompt, c_sample], axis=0)
    c_rows = -(-c_all.shape[0] // SUBLANES) * SUBLANES
    c_all = jnp.pad(c_all, ((0, c_rows - c_all.shape[0]), (0, 0)))

    outs_p = {k: [] for k in ('cmp_k', 'cmp_v', 'slc_k', 'slc_v', 'win_k', 'win_v', 'dn_state', 'dn_conv', 'ffn_conv')}
    outs_s = {k: [] for k in outs_p}

    w_in_b = _permute_w_in(w_in)
    wa, wb, wo = w_branch_a.astype(BF16), w_branch_b.astype(BF16), w_out.astype(BF16)
    w_up, w_dn = ffn_w_up.astype(BF16), ffn_w_down.astype(BF16)
    zero_state = jnp.zeros((1, bp, DN_HEADS, DN_DK, DN_DV), F32)

    for l in range(depth):
        mod = _ada_call(c_all, w_ada, l, b_ada[l])
        mod_p = mod[:bp].reshape(bp, 6, 1, d)
        mod_s = mod[bp:bp + bd].reshape(bd, 6, d)
        mp = [mod_p[:, i] for i in range(6)]
        ms = [jnp.repeat(mod_s[:, i], t_new, axis=0) for i in range(6)]

        wpq_k = _pq_weights(cmp_k_w1[l])
        wpq_v = _pq_weights(cmp_v_w1[l])
        pek = _pe_term(cmp_k_pe[l], wpq_k)
        pev = _pe_term(cmp_v_pe[l], wpq_v)
        w2k = cmp_k_w2[l].astype(BF16)
        w2v = cmp_v_w2[l].astype(BF16)

        zp = _ln_mm_call(xp, mp[0], mp[1], w_in_b, l, t_len, "in_proj")
        zp3 = zp.reshape(bp, t_len, -1)
        rows_p = lambda i, t0: zp3[:, t0:, OFF_ROWS + i * KV_WIDTH:OFF_ROWS + (i + 1) * KV_WIDTH].reshape(
            bp, t_len - t0, NSA_KV_HEADS, HEAD_DIM)
        beta, gc, gl = _gdn_gates_call(zp, OFF_TAIL, dn_a_log[l], dn_dt_bias[l], DN_CHUNK, 0)
        gcrow = gc.T.reshape(DN_HEADS, 1, bp * t_len)
        ya_p, dn_state_p = _gdn_call(zp, OFF_QKV, dn_conv_w[l], zp, OFF_DNGATE, dn_norm_w[l], beta, gc, gl, gcrow,
                                     zero_state, 0, rows_per_seq=t_len, chunk=GDN_MATMUL_ROWS,
                                     sub=math.gcd(DN_CHUNK, t_len), per_chunk_state=False, n_pad=0)
        chunk_w = CMP_STRIDE * KV_WIDTH
        pqk = _pq_call(zp[:, OFF_ROWS:OFF_ROWS + KV_WIDTH].reshape(-1, chunk_w), wpq_k)
        pqv = _pq_call(zp[:, OFF_ROWS + KV_WIDTH:OFF_ROWS + 2 * KV_WIDTH].reshape(-1, chunk_w), wpq_v)
        yb_p = _nsa_prompt_call(zp, pqk, pqv, pek, pev, w2k, w2v, bp, t_len)
        x1p = _merge_call(ya_p, yb_p, zp, wa, wb, wo, l, xp, mp[2], ln1_g[l], ln1_b[l], t_len, alpha)
        xp, tail_a, tail_b = _ffn_fused_call(x1p, mp[3], mp[4], w_up, ffn_conv_w[l], w_dn, l, mp[5], ln2_g[l],
                                             ln2_b[l], t_len, alpha)
        tiles_per_seq = tail_a.shape[0] // SUBLANES // bp
        last_rows = lambda t: t.reshape(bp, tiles_per_seq, SUBLANES, d_ff)[:, -1, SUBLANES - (FFN_CONV - 1):]

        keep = min(WINDOW, t_len)
        for i, name in enumerate(('cmp_k', 'cmp_v', 'slc_k', 'slc_v')):
            outs_p[name].append(rows_p(i, 0))
        outs_p['win_k'].append(rows_p(4, t_len - keep))
        outs_p['win_v'].append(rows_p(5, t_len - keep))
        outs_p['dn_state'].append(dn_state_p)
        outs_p['dn_conv'].append(zp3[:, t_len - (DN_CONV - 1):, OFF_QKV:OFF_QKV + 3 * DN_WIDTH])
        outs_p['ffn_conv'].append(jnp.concatenate([last_rows(tail_a), last_rows(tail_b)], axis=-1))

        zs = _ln_mm_call(xs, ms[0], ms[1], w_in_b, l, t_new, "in_proj")
        zs3 = zs.reshape(bd, t_new, -1)
        rows_s = zs3[:, :, OFF_ROWS:OFF_ROWS + 6 * KV_WIDTH].reshape(bd, t_new, 6, NSA_KV_HEADS, HEAD_DIM)
        n_pad = grp - t_new
        qkv_g = jnp.concatenate([jnp.zeros((bd, n_pad - (DN_CONV - 1), 3 * DN_WIDTH), F32), state_dn_conv[l],
                                 zs3[:, :, OFF_QKV:OFF_QKV + 3 * DN_WIDTH]], axis=1).reshape(bd * grp, -1)
        zs_g = _pad_groups(jnp.concatenate([zs3[:, :, OFF_DNGATE:OFF_DNGATE + DN_WIDTH],
                                            zs3[:, :, OFF_TAIL:OFF_TAIL + LANES]], axis=-1), n_pad)
        beta, gc, gl = _gdn_gates_call(zs_g, DN_WIDTH, dn_a_log[l], dn_dt_bias[l], grp, n_pad)
        gcrow = gc.T.reshape(DN_HEADS, 1, bd * grp)
        ya_g, dn_state_s = _gdn_call(qkv_g, 0, dn_conv_w[l], zs_g, 0, dn_norm_w[l], beta, gc, gl, gcrow, state_dn, l,
                                     rows_per_seq=grp, chunk=GDN_MATMUL_ROWS, sub=grp, per_chunk_state=True, n_pad=n_pad)
        ya_s = ya_g.reshape(bd, grp, DN_WIDTH)[:, n_pad:].reshape(bd * t_new, DN_WIDTH)
        pq_pool_k = _pq_pool_call(cache_cmp_k, l, wpq_k)
        pq_pool_v = _pq_pool_call(cache_cmp_v, l, wpq_v)
        new_chunk = lambda i: jnp.pad(zs3[:, :, OFF_ROWS + i * KV_WIDTH:OFF_ROWS + (i + 1) * KV_WIDTH],
                                      ((0, 0), (0, CMP_STRIDE - t_new), (0, 0))).reshape(bd, chunk_w)
        pq_new_k = _pq_call(new_chunk(0), wpq_k).reshape(bd, 1, -1)
        pq_new_v = _pq_call(new_chunk(1), wpq_v).reshape(bd, 1, -1)
        yb_s = _nsa_sample_call(page_table, zs3, pq_new_k, pq_new_v, pek, pev, w2k, w2v, cache_win_k, cache_win_v,
                                pq_pool_k, pq_pool_v, cache_slc_k, cache_slc_v, l).reshape(bd * t_new, NSA_WIDTH)
        x1s = _merge_call(ya_s, yb_s, zs, wa, wb, wo, l, xs, ms[2], ln1_g[l], ln1_b[l], t_new, alpha)
        us = _ln_mm_call(x1s, ms[3], ms[4], w_up, l, t_new, "ffn_up")
        us3 = us.reshape(bd, t_new, -1)
        u_g = jnp.concatenate([jnp.zeros((bd, n_pad - (FFN_CONV - 1), 2 * d_ff), F32), state_ffn_conv[l], us3],
                              axis=1).reshape(bd * grp, -1)
        x1_g = _pad_groups(x1s.reshape(bd, t_new, d), n_pad)
        gate2_g = jnp.repeat(mod_s[:, 5], grp, axis=0)
        xs_g = _ffn_down_call(u_g, ffn_conv_w[l], w_dn, l, x1_g, gate2_g, ln2_g[l], ln2_b[l], grp, alpha)
        xs = xs_g.reshape(bd, grp, d)[:, n_pad:].reshape(bd * t_new, d)

        for i, name in enumerate(('cmp_k', 'cmp_v', 'slc_k', 'slc_v')):
            outs_s[name].append(rows_s[:, :, i])
        outs_s['win_k'].append(rows_s[:, :, 4])
        outs_s['win_v'].append(rows_s[:, :, 5])
        outs_s['dn_state'].append(dn_state_s)
        outs_s['dn_conv'].append(jnp.concatenate([state_dn_conv[l], zs3[:, :, OFF_QKV:OFF_QKV + 3 * DN_WIDTH]], axis=1)[:, t_new:])
        outs_s['ffn_conv'].append(jnp.concatenate([state_ffn_conv[l], us3], axis=1)[:, t_new:])

    order = ('cmp_k', 'cmp_v', 'slc_k', 'slc_v', 'win_k', 'win_v', 'dn_state', 'dn_conv', 'ffn_conv')
    stacked_s = {k: jnp.stack(outs_s[k]) for k in order}
    for name, cache in (('win_k', cache_win_k), ('win_v', cache_win_v)):
        stacked_s[name] = jnp.concatenate([cache, stacked_s[name]], axis=2)[:, :, t_new:]
    return ((xp.reshape(bp, t_len, d), xs.reshape(bd, t_new, d))
            + tuple(jnp.stack(outs_p[k]) for k in order) + tuple(stacked_s[k] for k in order))
```
